```python
import jax, jax.numpy as jnp
from jax import lax
import numpy as np

D_MODEL = 1024
BATCH = 8
SEQ = 4096
DEPTH = 1

GRID_W = 64
EPS = 1e-6
SSD_HEADS = 16
SSD_HEADDIM = 64
D_SSD = SSD_HEADS * SSD_HEADDIM
SSD_GROUPS = 4
SSD_HPG = SSD_HEADS // SSD_GROUPS
SSD_STATE = 128
SSD_CHUNK = 128
CONV_W = 5
D_XBC = D_SSD + 2 * SSD_GROUPS * SSD_STATE
NA_HEADS = 16
NA_HEADDIM = 64
D_NA = NA_HEADS * NA_HEADDIM
NA_KH = 8
NA_KW = 16
NA_QB = 16
NA_KB = NA_QB + NA_KW
PEER_HEADS = 8
PEER_NKEYS = 128
PEER_EXPERTS = PEER_NKEYS * PEER_NKEYS
PEER_TOPK = 16
PEER_DKEY = 256
PEER_DHALF = PEER_DKEY // 2
PEER_CHUNK = 128
IN_SIZES = (D_SSD, D_XBC, SSD_HEADS, SSD_HEADS, 3 * D_NA, D_MODEL, D_MODEL)
D_IN = sum(IN_SIZES)
IN_OFFSETS = [int(o) for o in np.cumsum(IN_SIZES)[:-1]]

kernel_name = "hybrid_ssd_natten_peer_block"


def rms_norm(x, g):
    xf = x.astype(jnp.float32)
    y = xf * lax.rsqrt(jnp.mean(xf * xf, axis=-1, keepdims=True) + EPS)
    return (y * g.astype(jnp.float32)).astype(x.dtype)


def modulate(h, g, shift, scale):
    return rms_norm(h, g) * (1 + scale[:, None, :]) + shift[:, None, :]


def ssd_direction(xh, dt, a, bm, cm):
    b, l, g, k, p = xh.shape
    nc, L = l // SSD_CHUNK, SSD_CHUNK
    f32 = jnp.float32
    xdt = (xh.astype(f32) * dt[..., None]).reshape(b, nc, L, g, k, p)
    bc = bm.astype(f32).reshape(b, nc, L, g, SSD_STATE)
    cc = cm.astype(f32).reshape(b, nc, L, g, SSD_STATE)
    da = (dt * a).reshape(b, nc, L, g, k).transpose(0, 3, 4, 1, 2)
    cs = jnp.cumsum(da, axis=-1)
    lower = jnp.tril(jnp.ones((L, L), dtype=bool))
    seg = jnp.exp(jnp.where(lower, cs[..., :, None] - cs[..., None, :], -jnp.inf))
    cb = jnp.einsum('bclgn,bcsgn->bcgls', cc, bc)
    y_diag = jnp.einsum('bcgls,bgkcls,bcsgkp->bclgkp', cb, seg, xdt)
    decay_in = jnp.exp(cs[..., -1:] - cs)
    states = jnp.einsum('bclgn,bgkcl,bclgkp->cbgkpn', bc, decay_in, xdt)
    chunk_decay = jnp.exp(cs[..., -1]).transpose(3, 0, 1, 2)

    def step(hs, inp):
        s_c, d_c = inp
        return d_c[..., None, None] * hs + s_c, hs

    _, prev = lax.scan(step, jnp.zeros(states.shape[1:], f32), (states, chunk_decay))
    y_off = jnp.einsum('bclgn,cbgkpn,bgkcl->bclgkp', cc, prev, jnp.exp(cs))
    return (y_diag + y_off).reshape(b, l, g, k, p)


def ssd_branch(z, xbc, dt_raw_f, dt_raw_b, conv_w, conv_b, dt_bias_f, dt_bias_b,
               a_log_f, a_log_b, d_skip, norm_g):
    b, l, _ = xbc.shape
    f32 = jnp.float32
    xbc = lax.conv_general_dilated(xbc, conv_w[:, None, :].astype(xbc.dtype), (1,),
                                   [(CONV_W // 2, CONV_W // 2)],
                                   dimension_numbers=('NWC', 'WIO', 'NWC'),
                                   feature_group_count=D_XBC)
    xbc = jax.nn.silu(xbc + conv_b)
    xs, bm, cm = jnp.split(xbc, [D_SSD, D_SSD + SSD_GROUPS * SSD_STATE], axis=-1)
    xh = xs.reshape(b, l, SSD_GROUPS, SSD_HPG, SSD_HEADDIM)
    bm = bm.reshape(b, l, SSD_GROUPS, SSD_STATE)
    cm = cm.reshape(b, l, SSD_GROUPS, SSD_STATE)
    gk = (b, l, SSD_GROUPS, SSD_HPG)
    dt_f = jax.nn.softplus(dt_raw_f.astype(f32) + dt_bias_f.astype(f32)).reshape(gk)
    dt_b = jax.nn.softplus(dt_raw_b.astype(f32) + dt_bias_b.astype(f32)).reshape(gk)
    a_f = -jnp.exp(a_log_f.astype(f32)).reshape(SSD_GROUPS, SSD_HPG)
    a_b = -jnp.exp(a_log_b.astype(f32)).reshape(SSD_GROUPS, SSD_HPG)
    y_f = ssd_direction(xh, dt_f, a_f, bm, cm)
    y_b = ssd_direction(xh[:, ::-1], dt_b[:, ::-1], a_b, bm[:, ::-1], cm[:, ::-1])[:, ::-1]
    y = y_f + y_b + d_skip.astype(f32).reshape(SSD_GROUPS, SSD_HPG, 1) * xh.astype(f32)
    y = y.reshape(b, l, D_SSD).astype(z.dtype)
    return rms_norm(y * jax.nn.silu(z), norm_g)


def neighbourhood_attention(q, k, v, rpb):
    b, l, _ = q.shape
    rows = l // GRID_W
    kh = min(NA_KH, rows)
    ncb = GRID_W // NA_QB

    def grid(t):
        return t.reshape(b, rows, GRID_W, NA_HEADS, NA_HEADDIM).transpose(0, 3, 1, 2, 4)

    qg = grid(q) * (NA_HEADDIM ** -0.5)
    kg, vg = grid(k), grid(v)
    cols = np.arange(GRID_W)
    win_start = np.clip(cols - NA_KW // 2, 0, GRID_W - NA_KW).reshape(ncb, NA_QB)
    blk_start = np.clip(np.arange(ncb) * NA_QB - NA_KW // 2, 0, GRID_W - NA_KB)
    key_cols = blk_start[:, None] + np.arange(NA_KB)
    q_cols = cols.reshape(ncb, NA_QB)
    kc = key_cols[:, None, :]
    in_win = (kc >= win_start[:, :, None]) & (kc < win_start[:, :, None] + NA_KW)
    dcol = np.clip(kc - q_cols[:, :, None] + NA_KW - 1, 0, 2 * NA_KW - 2)
    col_bias = jnp.where(in_win, rpb.astype(jnp.float32)[:, :, dcol], -jnp.inf)

    def row(r):
        rs = jnp.clip(r - kh // 2, 0, rows - kh)
        drow = rs + jnp.arange(kh) - r + NA_KH - 1
        bias = col_bias[:, drow].transpose(0, 2, 3, 1, 4)
        qr = lax.dynamic_index_in_dim(qg, r, axis=2, keepdims=False)
        qr = qr.reshape(b, NA_HEADS, ncb, NA_QB, NA_HEADDIM)
        kr = lax.dynamic_slice_in_dim(kg, rs, kh, axis=2)[:, :, :, key_cols]
        vr = lax.dynamic_slice_in_dim(vg, rs, kh, axis=2)[:, :, :, key_cols]
        s = jnp.einsum('bhmqd,bhimjd->bhmqij', qr, kr).astype(jnp.float32) + bias
        sh = s.shape
        p = jax.nn.softmax(s.reshape(sh[0], sh[1], sh[2], sh[3], kh * NA_KB), axis=-1)
        p = p.reshape(sh).astype(vr.dtype)
        o = jnp.einsum('bhmqij,bhimjd->bhmqd', p, vr)
        return o.reshape(b, NA_HEADS, GRID_W, NA_HEADDIM)

    out = lax.map(row, jnp.arange(rows))
    return out.transpose(1, 0, 3, 2, 4).reshape(b, l, D_NA)


def peer(xn, w_q, sub_keys, u, v):
    b, l, d = xn.shape
    xt = xn.reshape(-1, PEER_CHUNK, d)

    def chunk(xc):
        q = (xc @ w_q).reshape(PEER_CHUNK, PEER_HEADS, 2, PEER_DHALF)
        s = jnp.einsum('thzd,znd->thzn', q, sub_keys).astype(jnp.float32)
        s1, i1 = lax.top_k(s[:, :, 0], PEER_TOPK)
        s2, i2 = lax.top_k(s[:, :, 1], PEER_TOPK)
        cand = (s1[..., :, None] + s2[..., None, :]).reshape(PEER_CHUNK, PEER_HEADS, PEER_TOPK * PEER_TOPK)
        cidx = (i1[..., :, None] * PEER_NKEYS + i2[..., None, :]).reshape(PEER_CHUNK, PEER_HEADS, PEER_TOPK * PEER_TOPK)
        top, pos = lax.top_k(cand, PEER_TOPK)
        eidx = jnp.take_along_axis(cidx, pos, axis=-1)
        gate = jax.nn.softmax(top, axis=-1)
        act = jax.nn.gelu(jnp.einsum('thkd,td->thk', u[eidx], xc).astype(jnp.float32), approximate=False)
        return jnp.einsum('thk,thkd->td', (gate * act).astype(v.dtype), v[eidx])

    return lax.map(chunk, xt).reshape(b, l, d)


def hybrid_layer(h, c, w_ada, b_ada, norm1_g, w_in, conv_w, conv_b, dt_bias_f, dt_bias_b,
                 a_log_f, a_log_b, d_skip, ssd_norm_g, w_ssd_br, na_rpb, w_na_br, w_out,
                 norm2_g, peer_wq, peer_keys, peer_u, peer_v):
    mod = jax.nn.silu(c) @ w_ada + b_ada
    sh1, sc1, g1, sh2, sc2, g2 = jnp.split(mod, 6, axis=-1)
    n1 = modulate(h, norm1_g, sh1, sc1)
    proj = n1 @ w_in
    z, xbc, dtf, dtb, qkv, gate_ssd, gate_na = jnp.split(proj, IN_OFFSETS, axis=-1)
    y_ssd = ssd_branch(z, xbc, dtf, dtb, conv_w, conv_b, dt_bias_f, dt_bias_b,
                       a_log_f, a_log_b, d_skip, ssd_norm_g) @ w_ssd_br
    q, k, v = jnp.split(qkv, 3, axis=-1)
    y_na = neighbourhood_attention(q, k, v, na_rpb) @ w_na_br
    mixed = jax.nn.sigmoid(gate_ssd) * y_ssd + jax.nn.sigmoid(gate_na) * y_na
    h = h + g1[:, None, :] * (mixed @ w_out)
    n2 = modulate(h, norm2_g, sh2, sc2)
    h = h + g2[:, None, :] * peer(n2, peer_wq, peer_keys, peer_u, peer_v)
    return h


def setup_inputs(seed: int = 0) -> dict:
    key = jax.random.key(seed)
    ks = jax.random.split(key, 32)
    f32 = jnp.float32
    nrm = lambda k, shape, s: jax.random.normal(k, shape, f32) * s
    dt0 = jnp.exp(jax.random.uniform(ks[8], (DEPTH, 2, SSD_HEADS), f32) * (np.log(0.1) - np.log(0.001)) + np.log(0.001))
    dt_bias = dt0 + jnp.log(-jnp.expm1(-dt0))
    a_log = jnp.log(jax.random.uniform(ks[9], (DEPTH, 2, SSD_HEADS), f32, 1.0, 16.0))
    return {
        "x": nrm(ks[0], (BATCH, SEQ, D_MODEL), 1.0),
        "c": nrm(ks[1], (BATCH, D_MODEL), 1.0),
        "w_ada": nrm(ks[2], (DEPTH, D_MODEL, 6 * D_MODEL), D_MODEL ** -0.5),
        "b_ada": nrm(ks[3], (DEPTH, 6 * D_MODEL), 0.01),
        "norm1_g": 1.0 + nrm(ks[4], (DEPTH, D_MODEL), 0.05),
        "w_in": nrm(ks[5], (DEPTH, D_MODEL, D_IN), D_MODEL ** -0.5),
        "conv_w": nrm(ks[6], (DEPTH, CONV_W, D_XBC), CONV_W ** -0.5),
        "conv_b": nrm(ks[7], (DEPTH, D_XBC), 0.01),
        "dt_bias_f": dt_bias[:, 0],
        "dt_bias_b": dt_bias[:, 1],
        "a_log_f": a_log[:, 0],
        "a_log_b": a_log[:, 1],
        "d_skip": 1.0 + nrm(ks[10], (DEPTH, SSD_HEADS), 0.1),
        "ssd_norm_g": 1.0 + nrm(ks[11], (DEPTH, D_SSD), 0.05),
        "w_ssd_br": nrm(ks[12], (DEPTH, D_SSD, D_MODEL), D_SSD ** -0.5),
        "na_rpb": nrm(ks[13], (DEPTH, NA_HEADS, 2 * NA_KH - 1, 2 * NA_KW - 1), 0.1),
        "w_na_br": nrm(ks[14], (DEPTH, D_NA, D_MODEL), D_NA ** -0.5),
        "w_out": nrm(ks[15], (DEPTH, D_MODEL, D_MODEL), D_MODEL ** -0.5),
        "norm2_g": 1.0 + nrm(ks[16], (DEPTH, D_MODEL), 0.05),
        "peer_wq": nrm(ks[17], (DEPTH, D_MODEL, PEER_HEADS * PEER_DKEY), D_MODEL ** -0.5),
        "peer_keys": nrm(ks[18], (DEPTH, 2, PEER_NKEYS, PEER_DHALF), PEER_DHALF ** -0.5),
        "peer_u": nrm(ks[19], (DEPTH, PEER_EXPERTS, D_MODEL), D_MODEL ** -0.5),
        "peer_v": nrm(ks[20], (DEPTH, PEER_EXPERTS, D_MODEL), PEER_HEADS ** -0.5),
        "final_g": 1.0 + nrm(ks[21], (D_MODEL,), 0.05),
    }


def reference(x, c, w_ada, b_ada, norm1_g, w_in, conv_w, conv_b, dt_bias_f, dt_bias_b,
              a_log_f, a_log_b, d_skip, ssd_norm_g, w_ssd_br, na_rpb, w_na_br, w_out,
              norm2_g, peer_wq, peer_keys, peer_u, peer_v, final_g):
    h = x
    for i in range(DEPTH):
        h = hybrid_layer(h, c, w_ada[i], b_ada[i], norm1_g[i], w_in[i], conv_w[i], conv_b[i],
                         dt_bias_f[i], dt_bias_b[i], a_log_f[i], a_log_b[i], d_skip[i],
                         ssd_norm_g[i], w_ssd_br[i], na_rpb[i], w_na_br[i], w_out[i],
                         norm2_g[i], peer_wq[i], peer_keys[i], peer_u[i], peer_v[i])
    return rms_norm(h, final_g)
```

```python
import functools

import numpy as np
import jax
import jax.numpy as jnp
from jax import lax
from jax.experimental import pallas as pl
from jax.experimental.pallas import tpu as pltpu

F32 = jnp.float32
BF16 = jnp.bfloat16
I32 = jnp.int32

EPS = 1e-6
GRID_W = 64
SSD_HEADS = 16
SSD_HEADDIM = 64
SSD_GROUPS = 4
SSD_STATE = 128
SSD_CHUNK = 128
CONV_W = 5
NA_HEADS = 16
NA_HEADDIM = 64
NA_KH = 8
NA_KW = 16
NA_QROWS = 4
NA_KROWS = 12
PEER_HEADS = 8
PEER_NKEYS = 128
PEER_TOPK = 16
PEER_DHALF = 128

LANES = 128
HALO = 16
VMEM_LIMIT = 56 * 1024 * 1024


def _cparams(sem):
    return pltpu.CompilerParams(dimension_semantics=sem, vmem_limit_bytes=VMEM_LIMIT)


def _sigmoid(x):
    return 1.0 / (1.0 + jnp.exp(-x))


def _rms_mod(x, g, shift, scale):
    ms = jnp.mean(x * x, axis=-1, keepdims=True)
    y = x * lax.rsqrt(ms + EPS) * g
    return y * (1.0 + scale) + shift


def _ada_kernel(c_ref, w_ref, b_ref, o_ref):
    c = c_ref[...]
    sc = c * _sigmoid(c)
    o_ref[...] = jnp.dot(sc, w_ref[...], preferred_element_type=F32,
                         precision=lax.Precision.HIGHEST) + b_ref[...]


def _ada(c, w, b):
    bsz, d = c.shape
    n = w.shape[1]
    tn = 1024
    return pl.pallas_call(
        _ada_kernel,
        grid=(n // tn,),
        in_specs=[pl.BlockSpec((bsz, d), lambda j: (0, 0)),
                  pl.BlockSpec((d, tn), lambda j: (0, j)),
                  pl.BlockSpec((1, tn), lambda j: (0, j))],
        out_specs=pl.BlockSpec((bsz, tn), lambda j: (0, j)),
        out_shape=jax.ShapeDtypeStruct((bsz, n), F32),
        compiler_params=_cparams(("arbitrary",)),
        name="ada",
    )(c, w, b.reshape(1, n))


def _inproj_kernel(x_ref, sh_ref, sc_ref, g_ref, w_ref, wdt_ref, proj_ref, dt_ref, n1_ref):
    @pl.when(pl.program_id(1) == 0)
    def _():
        n1 = _rms_mod(x_ref[...], g_ref[...], sh_ref[0], sc_ref[0]).astype(BF16)
        n1_ref[...] = n1
        dt_ref[...] = jnp.dot(n1, wdt_ref[...], preferred_element_type=F32)

    proj_ref[...] = jnp.dot(n1_ref[...], w_ref[...], preferred_element_type=F32).astype(BF16)


def _inproj(x2, sh, sc, g, w, wdt, seq):
    t, d = x2.shape
    n = w.shape[1]
    tm = min(1024, seq)
    tn = 1024
    bidx = lambda i, j: ((i * tm) // seq, 0, 0)
    return pl.pallas_call(
        _inproj_kernel,
        grid=(t // tm, n // tn),
        in_specs=[pl.BlockSpec((tm, d), lambda i, j: (i, 0)),
                  pl.BlockSpec((1, 1, d), bidx),
                  pl.BlockSpec((1, 1, d), bidx),
                  pl.BlockSpec((1, d), lambda i, j: (0, 0)),
                  pl.BlockSpec((d, tn), lambda i, j: (0, j)),
                  pl.BlockSpec((d, LANES), lambda i, j: (0, 0))],
        out_specs=[pl.BlockSpec((tm, tn), lambda i, j: (i, j)),
                   pl.BlockSpec((tm, LANES), lambda i, j: (i, 0))],
        out_shape=[jax.ShapeDtypeStruct((t, n), BF16),
                   jax.ShapeDtypeStruct((t, LANES), F32)],
        scratch_shapes=[pltpu.VMEM((tm, d), BF16)],
        compiler_params=_cparams(("arbitrary", "arbitrary")),
        name="inproj",
    )(x2, sh, sc, g, w, wdt)


def _expand_heads(v, e2_ref):
    hi = v.astype(BF16).astype(F32)
    comb = (hi + pltpu.roll(v - hi, 64, 1)).astype(BF16)
    return jnp.dot(comb, e2_ref[...], preferred_element_type=F32)


def _ssd_kernel(*refs, rev):
    if rev:
        (xm_ref, xp_ref, xn_ref, dt_ref, z_ref, yf_ref, cw_ref, cb_ref, dtb_ref, a_ref,
         e2_ref, dsk_ref, ng_ref, out_ref, ext_ref, st_ref) = refs
    else:
        (xm_ref, xp_ref, xn_ref, dt_ref, cw_ref, cb_ref, dtb_ref, a_ref,
         e2_ref, out_ref, ext_ref, st_ref) = refs
    ck = SSD_CHUNK
    d_ssd = SSD_HEADS * SSD_HEADDIM
    gw = d_ssd // SSD_GROUPS
    c = pl.program_id(1)
    nc = pl.num_programs(1)
    ce = nc - 1 - c if rev else c

    @pl.when(c == 0)
    def _():
        st_ref[...] = jnp.zeros_like(st_ref)

    prev = jnp.where(ce == 0, 0.0, xp_ref[0].astype(F32))
    nxt = jnp.where(ce == nc - 1, 0.0, xn_ref[0].astype(F32))
    ext_ref[0:HALO, :] = prev
    ext_ref[HALO:HALO + ck, :] = xm_ref[...].astype(F32)
    ext_ref[HALO + ck:HALO + ck + HALO, :] = nxt
    acc = jnp.zeros((ck, xm_ref.shape[1]), F32) + cb_ref[...]
    for w in range(CONV_W):
        o = HALO - CONV_W // 2 + w
        acc = acc + ext_ref[o:o + ck, :] * cw_ref[w:w + 1, :]
    xbc = acc * _sigmoid(acc)
    xs = xbc[:, :d_ssd]
    bm = xbc[:, d_ssd:d_ssd + SSD_GROUPS * SSD_STATE]
    cm = xbc[:, d_ssd + SSD_GROUPS * SSD_STATE:]

    lane = lax.broadcasted_iota(I32, (ck, LANES), 1)
    row = lax.broadcasted_iota(I32, (ck, LANES), 0)
    off = SSD_HEADS if rev else 0
    valid = (lane >= off) & (lane < off + SSD_HEADS)
    dtr = dt_ref[...] + dtb_ref[...]
    dt = jnp.where(valid, jnp.maximum(dtr, 0.0) + jnp.log1p(jnp.exp(-jnp.abs(dtr))), 0.0)
    da = dt * a_ref[...]
    tri = (lane >= row) if rev else (lane <= row)
    trib = jnp.where(tri, 1.0, 0.0).astype(BF16)
    hi = da.astype(BF16)
    r1 = da - hi.astype(F32)
    mid = r1.astype(BF16)
    lo = (r1 - mid.astype(F32)).astype(BF16)
    cs = (jnp.dot(trib, hi, preferred_element_type=F32)
          + jnp.dot(trib, mid, preferred_element_type=F32)
          + jnp.dot(trib, lo, preferred_element_type=F32))
    cst = cs.T
    tot_row = 0 if rev else ck - 1
    tot = cs[tot_row:tot_row + 1, :]
    dec = jnp.where(valid, jnp.exp(tot - cs), 0.0)
    ecs = jnp.where(valid, jnp.exp(cs), 0.0)
    ecs_e = _expand_heads(ecs, e2_ref)
    xdt = xs * _expand_heads(dt, e2_ref)
    xdec = (xs * _expand_heads(dt * dec, e2_ref)).astype(BF16)

    lo_half = lax.broadcasted_iota(I32, (ck, LANES), 1) < SSD_HEADDIM
    hpg = SSD_HEADS // SSD_GROUPS
    ys = []
    for g in range(SSD_GROUPS):
        bg = bm[:, g * SSD_STATE:(g + 1) * SSD_STATE]
        cg = cm[:, g * SSD_STATE:(g + 1) * SSD_STATE].astype(BF16)
        cbg = lax.dot_general(cg, bg.astype(BF16), (((1,), (1,)), ((), ())),
                              preferred_element_type=F32)
        yg = []
        for q in range(hpg // 2):
            ls = []
            for e in range(2):
                col = off + g * hpg + q * 2 + e
                dlt = cs[:, col:col + 1] - cst[col:col + 1, :]
                seg = jnp.exp(jnp.where(tri, dlt, -jnp.inf))
                ls.append((cbg * seg).astype(BF16))
            blk = g * (hpg // 2) + q
            xq = xdt[:, blk * LANES:(blk + 1) * LANES]
            rhs = jnp.concatenate([jnp.where(lo_half, xq, 0.0), jnp.where(lo_half, 0.0, xq)],
                                  axis=0).astype(BF16)
            yg.append(jnp.dot(jnp.concatenate(ls, axis=1), rhs, preferred_element_type=F32))
        y_diag = jnp.concatenate(yg, axis=1)
        st = st_ref[g]
        e_g = ecs_e[:, g * gw:(g + 1) * gw]
        y_off = jnp.dot(cg, st.astype(BF16), preferred_element_type=F32) * e_g
        ys.append(y_diag + y_off)
        upd = jnp.dot(bg.T.astype(BF16), xdec[:, g * gw:(g + 1) * gw], preferred_element_type=F32)
        st_ref[g] = st * e_g[tot_row:tot_row + 1, :] + upd
    y = jnp.concatenate(ys, axis=1)

    if rev:
        y = yf_ref[...] + y + dsk_ref[...] * xs
        z = z_ref[...].astype(F32)
        u = y * (z * _sigmoid(z))
        ms = jnp.mean(u * u, axis=-1, keepdims=True)
        out_ref[...] = (u * lax.rsqrt(ms + EPS) * ng_ref[...]).astype(BF16)
    else:
        out_ref[...] = y


def _ssd(proj, dt_raw, z_col, yf, params, bsz, seq, rev):
    t = proj.shape[0]
    ck = SSD_CHUNK
    nc = seq // ck
    d_xbc = params["cw"].shape[1]
    d_ssd = SSD_HEADS * SSD_HEADDIM
    proj16 = proj.reshape(t // HALO, HALO, proj.shape[1])
    per = ck // HALO
    last16 = t // HALO - 1

    def ce_of(c):
        return nc - 1 - c if rev else c

    main = lambda b, c: (b * nc + ce_of(c), 0)
    prev = lambda b, c: (jnp.maximum((b * nc + ce_of(c)) * per - 1, 0), 0, 0)
    nxt = lambda b, c: (jnp.minimum((b * nc + ce_of(c)) * per + per, last16), 0, 0)
    const2 = lambda b, c: (0, 0)
    in_specs = [pl.BlockSpec((ck, d_xbc), main),
                pl.BlockSpec((1, HALO, d_xbc), prev),
                pl.BlockSpec((1, HALO, d_xbc), nxt),
                pl.BlockSpec((ck, LANES), main)]
    args = [proj, proj16, proj16, dt_raw]
    if rev:
        in_specs += [pl.BlockSpec((ck, d_ssd), lambda b, c: (b * nc + ce_of(c), z_col)),
                     pl.BlockSpec((ck, d_ssd), main)]
        args += [proj, yf]
    in_specs += [pl.BlockSpec((8, d_xbc), const2), pl.BlockSpec((1, d_xbc), const2),
                 pl.BlockSpec((1, LANES), const2), pl.BlockSpec((1, LANES), const2),
                 pl.BlockSpec((LANES, d_ssd), const2)]
    d = "b" if rev else "f"
    args += [params["cw"], params["cb"], params["dtb_" + d], params["a_" + d], params["e2_" + d]]
    if rev:
        in_specs += [pl.BlockSpec((1, d_ssd), const2), pl.BlockSpec((1, d_ssd), const2)]
        args += [params["dsk"], params["ng"]]
    return pl.pallas_call(
        functools.partial(_ssd_kernel, rev=rev),
        grid=(bsz, nc),
        in_specs=in_specs,
        out_specs=pl.BlockSpec((ck, d_ssd), main),
        out_shape=jax.ShapeDtypeStruct((t, d_ssd), BF16 if rev else F32),
        scratch_shapes=[pltpu.VMEM((ck + 2 * HALO, d_xbc), F32),
                        pltpu.VMEM((SSD_GROUPS, SSD_STATE, d_ssd // SSD_GROUPS), F32)],
        compiler_params=_cparams(("arbitrary", "arbitrary")),
        name="ssd_bwd" if rev else "ssd_fwd",
    )(*args)


def _na_plan(rows):
    nblk = rows // NA_QROWS
    variants, var_of_blk, ks_of_blk = [], [], []
    for i in range(nblk):
        ks = int(np.clip(i * NA_QROWS - NA_KH // 2, 0, rows - NA_KROWS))
        drow = np.zeros((NA_QROWS, NA_KROWS), np.int64)
        ok = np.zeros((NA_QROWS, NA_KROWS), bool)
        for dr in range(NA_QROWS):
            r = i * NA_QROWS + dr
            rs = int(np.clip(r - NA_KH // 2, 0, rows - NA_KH))
            for j in range(NA_KH):
                w = rs + j - ks
                drow[dr, w] = rs + j - r + NA_KH - 1
                ok[dr, w] = True
        key = (drow.tobytes(), ok.tobytes())
        for vi, (k2, _, _) in enumerate(variants):
            if k2 == key:
                var_of_blk.append(vi)
                break
        else:
            var_of_blk.append(len(variants))
            variants.append((key, drow, ok))
        ks_of_blk.append(ks)
    return [(d, o) for _, d, o in variants], var_of_blk, ks_of_blk


def _na_bias(rpb, rows):
    variants, var_of_blk, ks_of_blk = _na_plan(rows)
    cols = np.arange(GRID_W)
    win_start = np.clip(cols - NA_KW // 2, 0, GRID_W - NA_KW)
    in_win = (cols[None, :] >= win_start[:, None]) & (cols[None, :] < win_start[:, None] + NA_KW)
    dcol = np.clip(cols[None, :] - cols[:, None] + NA_KW - 1, 0, 2 * NA_KW - 2)
    tabs = []
    rpb = rpb.astype(F32)
    for drow, ok in variants:
        g = rpb[:, drow[:, None, :, None], dcol[None, :, None, :]]
        m = ok[:, None, :, None] & in_win[None, :, None, :]
        g = jnp.where(m[None], g, -jnp.inf)
        tabs.append(g.reshape(rpb.shape[0], NA_QROWS * GRID_W, NA_KROWS * GRID_W))
    return jnp.stack(tabs), var_of_blk, ks_of_blk


def _na_kernel(var_ref, ks_ref, q_ref, k_ref, v_ref, bias_ref, o_ref):
    mq = NA_QROWS * GRID_W
    nk = NA_KROWS * GRID_W
    nblk = q_ref.shape[0] // mq
    lo_half = lax.broadcasted_iota(I32, (mq, LANES), 1) < NA_HEADDIM

    def blk(i, carry):
        q0 = pl.multiple_of(i * mq, mq)
        k0 = pl.multiple_of(ks_ref[i] * GRID_W, GRID_W)
        qb = q_ref[pl.ds(q0, mq), :] * (NA_HEADDIM ** -0.5)
        zero = jnp.zeros_like(qb)
        q2 = jnp.concatenate([jnp.where(lo_half, qb, zero), jnp.where(lo_half, zero, qb)], axis=0)
        kb = k_ref[pl.ds(k0, nk), :]
        vb = v_ref[pl.ds(k0, nk), :]
        s = lax.dot_general(q2, kb, (((1,), (1,)), ((), ())), preferred_element_type=F32)
        s = s + bias_ref[var_ref[i]].reshape(2 * mq, nk)
        m = jnp.max(s, axis=-1, keepdims=True)
        p = jnp.exp(s - m)
        l = jnp.sum(p, axis=-1, keepdims=True)
        o2 = jnp.dot(p.astype(BF16), vb, preferred_element_type=F32) / l
        o_ref[pl.ds(q0, mq), :] = jnp.where(lo_half, o2[:mq], o2[mq:]).astype(BF16)
        return carry

    lax.fori_loop(0, nblk, blk, 0)


def _na(proj, bias, var_of_blk, ks_of_blk, q_col, bsz, seq):
    t = proj.shape[0]
    npair = NA_HEADS // 2
    nv = bias.shape[0]
    mq = NA_QROWS * GRID_W
    nk = NA_KROWS * GRID_W
    grid_spec = pltpu.PrefetchScalarGridSpec(
        num_scalar_prefetch=2,
        grid=(bsz, npair),
        in_specs=[pl.BlockSpec((seq, LANES), lambda b, p, *_: (b, q_col + p)),
                  pl.BlockSpec((seq, LANES), lambda b, p, *_: (b, q_col + npair + p)),
                  pl.BlockSpec((seq, LANES), lambda b, p, *_: (b, q_col + 2 * npair + p)),
                  pl.BlockSpec((nv, 2, mq, nk), lambda b, p, *_: (0, p, 0, 0))],
        out_specs=pl.BlockSpec((seq, LANES), lambda b, p, *_: (b, p)),
    )
    return pl.pallas_call(
        _na_kernel,
        grid_spec=grid_spec,
        out_shape=jax.ShapeDtypeStruct((t, NA_HEADS * NA_HEADDIM), BF16),
        compiler_params=_cparams(("arbitrary", "arbitrary")),
        name="na",
    )(jnp.asarray(var_of_blk, I32), jnp.asarray(ks_of_blk, I32), proj, proj, proj, bias)


def _mix_kernel(x_ref, yn_ref, at_ref, gs_ref, gn_ref, g1_ref, sh_ref, sc_ref, ng_ref,
                ws_ref, wn_ref, wo_ref, wq_ref, keys_ref, h_ref, s_ref):
    y1 = jnp.dot(yn_ref[...], ws_ref[...], preferred_element_type=F32)
    y2 = jnp.dot(at_ref[...], wn_ref[...], preferred_element_type=F32)
    mixed = _sigmoid(gs_ref[...].astype(F32)) * y1 + _sigmoid(gn_ref[...].astype(F32)) * y2
    h = x_ref[...] + g1_ref[0] * jnp.dot(mixed.astype(BF16), wo_ref[...], preferred_element_type=F32)
    h_ref[...] = h
    n2 = _rms_mod(h, ng_ref[...], sh_ref[0], sc_ref[0]).astype(BF16)
    q = jnp.dot(n2, wq_ref[...], preferred_element_type=F32).astype(BF16)
    for hz in range(2 * PEER_HEADS):
        qs = q[:, hz * PEER_DHALF:(hz + 1) * PEER_DHALF]
        s_ref[hz] = lax.dot_general(keys_ref[hz % 2], qs, (((1,), (1,)), ((), ())),
                                    preferred_element_type=F32)


def _mix(x2, ynorm, attn, proj, gs_col, g1, sh2, sc2, ng2, ws, wn, wo, wq, keys, seq):
    t, d = x2.shape
    tm = min(256, seq)
    bidx = lambda i: ((i * tm) // seq, 0, 0)
    row = lambda i: (i, 0)
    const2 = lambda i: (0, 0)
    return pl.pallas_call(
        _mix_kernel,
        grid=(t // tm,),
        in_specs=[pl.BlockSpec((tm, d), row), pl.BlockSpec((tm, d), row), pl.BlockSpec((tm, d), row),
                  pl.BlockSpec((tm, d), lambda i: (i, gs_col)),
                  pl.BlockSpec((tm, d), lambda i: (i, gs_col + 1)),
                  pl.BlockSpec((1, 1, d), bidx), pl.BlockSpec((1, 1, d), bidx),
                  pl.BlockSpec((1, 1, d), bidx), pl.BlockSpec((1, d), const2),
                  pl.BlockSpec((d, d), const2), pl.BlockSpec((d, d), const2),
                  pl.BlockSpec((d, d), const2), pl.BlockSpec(wq.shape, const2),
                  pl.BlockSpec(keys.shape, lambda i: (0, 0, 0))],
        out_specs=[pl.BlockSpec((tm, d), row),
                   pl.BlockSpec((2 * PEER_HEADS, PEER_NKEYS, tm), lambda i: (0, 0, i))],
        out_shape=[jax.ShapeDtypeStruct((t, d), F32),
                   jax.ShapeDtypeStruct((2 * PEER_HEADS, PEER_NKEYS, t), F32)],
        compiler_params=_cparams(("arbitrary",)),
        name="mix",
    )(x2, ynorm, attn, proj, proj, g1, sh2, sc2, ng2, ws, wn, wo, wq, keys)


def _hyperbola():
    return [(i, k) for i in range(PEER_TOPK) for k in range(PEER_TOPK)
            if (i + 1) * (k + 1) <= PEER_TOPK]


def _select_kernel(s_ref, eidx_ref, gate_ref, val_ref, tv_ref, ti_ref, cv_ref, ce_ref):
    nk = PEER_NKEYS
    big = jnp.int32(1 << 30)
    ninf = jnp.float32(-jnp.inf)
    nacc = 8

    def tree(vals, op):
        while len(vals) > 1:
            vals = [op(vals[i], vals[i + 1]) if i + 1 < len(vals) else vals[i]
                    for i in range(0, len(vals), 2)]
        return vals[0]

    def extract(ref, n, ids, out):
        def rnd(r, carry):
            accs = [ref[j] for j in range(nacc)]
            for j in range(nacc, n):
                accs[j % nacc] = jnp.maximum(accs[j % nacc], ref[j])
            m = tree(accs, jnp.maximum)
            sels = [jnp.where(ref[j] == m, ids[j], big) for j in range(nacc)]
            for j in range(nacc, n):
                sels[j % nacc] = jnp.minimum(sels[j % nacc], jnp.where(ref[j] == m, ids[j], big))
            sel = tree(sels, jnp.minimum)
            for j in range(n):
                ref[j] = jnp.where(sel == ids[j], ninf, ref[j])
            out(r, m, sel)
            return carry
        lax.fori_loop(0, PEER_TOPK, rnd, 0)

    for z in range(2):
        val_ref[...] = s_ref[z]

        def out_half(r, m, sel, z=z):
            tv_ref[z, r] = m
            ti_ref[z, r] = sel
        extract(val_ref, nk, list(range(nk)), out_half)

    pairs = _hyperbola()
    for p, (i, k) in enumerate(pairs):
        cv_ref[p] = tv_ref[0, i] + tv_ref[1, k]
        ce_ref[p] = ti_ref[0, i] * nk + ti_ref[1, k]
    flat = [i * PEER_TOPK + k for i, k in pairs]

    def out_final(r, m, sel):
        e = jnp.zeros_like(sel)
        for p in range(len(pairs)):
            e = jnp.where(sel == flat[p], ce_ref[p], e)
        eidx_ref[0, r] = e
        gate_ref[0, r] = m
    extract(cv_ref, len(pairs), flat, out_final)

    top0 = gate_ref[0, 0]
    ex = [jnp.exp(gate_ref[0, r] - top0) for r in range(PEER_TOPK)]
    zsum = ex[0]
    for r in range(1, PEER_TOPK):
        zsum = zsum + ex[r]
    for r in range(PEER_TOPK):
        gate_ref[0, r] = ex[r] / zsum


def _select(s4):
    nhz, nk, tb, _ = s4.shape
    r = min(8, tb)
    npairs = len(_hyperbola())
    oshape = (PEER_HEADS, PEER_TOPK, tb, LANES)
    return pl.pallas_call(
        _select_kernel,
        grid=(tb // r, PEER_HEADS),
        in_specs=[pl.BlockSpec((2, nk, r, LANES), lambda i, h: (h, 0, i, 0))],
        out_specs=[pl.BlockSpec((1, PEER_TOPK, r, LANES), lambda i, h: (h, 0, i, 0)),
                   pl.BlockSpec((1, PEER_TOPK, r, LANES), lambda i, h: (h, 0, i, 0))],
        out_shape=[jax.ShapeDtypeStruct(oshape, I32), jax.ShapeDtypeStruct(oshape, F32)],
        scratch_shapes=[pltpu.VMEM((nk, r, LANES), F32),
                        pltpu.VMEM((2, PEER_TOPK, r, LANES), F32),
                        pltpu.VMEM((2, PEER_TOPK, r, LANES), I32),
                        pltpu.VMEM((npairs, r, LANES), F32),
                        pltpu.VMEM((npairs, r, LANES), I32)],
        compiler_params=_cparams(("arbitrary", "arbitrary")),
        name="select",
    )(s4)


def _peer_kernel(h_ref, sh_ref, sc_ref, g2_ref, ng_ref, fg_ref, eidx_ref, gate_ref, u_ref, v_ref,
                 out_ref, n2_ref, act_ref, wv_ref, w3_ref, acc_ref, *, ns):
    nk = PEER_NKEYS
    tm = h_ref.shape[0]
    nblk = u_ref.shape[0] // nk
    s = pl.program_id(1)

    @pl.when(s == 0)
    def _():
        n2_ref[...] = _rms_mod(h_ref[...], ng_ref[...], sh_ref[0], sc_ref[0]).astype(BF16)
        act_ref[...] = jnp.zeros_like(act_ref)

    @pl.when(s < ns)
    def _():
        a_all = lax.dot_general(n2_ref[...], u_ref[...], (((1,), (1,)), ((), ())),
                                preferred_element_type=F32)
        e = eidx_ref[...]
        ai = jnp.right_shift(e, nk.bit_length() - 1)
        bi = jnp.bitwise_and(e, nk - 1)
        act = act_ref[...]
        for a in range(nblk):
            got = jnp.take_along_axis(a_all[:, a * nk:(a + 1) * nk], bi, axis=1)
            act = jnp.where(ai == s * nblk + a, got, act)
        act_ref[...] = act

    @pl.when(s == ns - 1)
    def _():
        act = act_ref[...]
        gelu = 0.5 * act * (1.0 + lax.erf(act * (2.0 ** -0.5)))
        wv_ref[...] = gate_ref[...] * gelu
        sub = lax.broadcasted_iota(I32, (nk, LANES), 0)

        def tok(t, carry):
            e = eidx_ref[pl.ds(t, 1), :]
            w = wv_ref[pl.ds(t, 1), :]
            ar = jnp.right_shift(e, nk.bit_length() - 1)
            br = jnp.bitwise_and(e, nk - 1)
            pt = jnp.where(sub == ar, w, 0.0).astype(BF16)
            qt = jnp.where(sub == br, 1.0, 0.0).astype(BF16)
            w3_ref[pl.ds(pl.multiple_of(t * nk, nk), nk), :] = lax.dot_general(
                pt, qt, (((1,), (1,)), ((), ())), preferred_element_type=F32)
            return carry
        lax.fori_loop(0, tm, tok, 0)
        acc_ref[...] = jnp.zeros_like(acc_ref)

    @pl.when(s >= ns)
    def _():
        base = (s - ns) * nblk
        cols = [w3_ref[pl.ds(base + a, tm, stride=nk), :].astype(BF16) for a in range(nblk)]
        acc_ref[...] += jnp.dot(jnp.concatenate(cols, axis=1), v_ref[...],
                                preferred_element_type=F32)

    @pl.when(s == 2 * ns - 1)
    def _():
        hh = h_ref[...] + g2_ref[0] * acc_ref[...]
        ms = jnp.mean(hh * hh, axis=-1, keepdims=True)
        out_ref[...] = hh * lax.rsqrt(ms + EPS) * fg_ref[...]


def _peer(h, sh2, sc2, g2, ng2, fg, eidx_t, gate_t, u, v, seq):
    t, d = h.shape
    ne = u.shape[0]
    tm = min(256, seq)
    eb = 2048
    ns = ne // eb
    nj = eidx_t.shape[1]
    bidx = lambda i, s: ((i * tm) // seq, 0, 0)
    row = lambda i, s: (i, 0)
    const2 = lambda i, s: (0, 0)
    return pl.pallas_call(
        functools.partial(_peer_kernel, ns=ns),
        grid=(t // tm, 2 * ns),
        in_specs=[pl.BlockSpec((tm, d), row),
                  pl.BlockSpec((1, 1, d), bidx), pl.BlockSpec((1, 1, d), bidx),
                  pl.BlockSpec((1, 1, d), bidx),
                  pl.BlockSpec((1, d), const2), pl.BlockSpec((1, d), const2),
                  pl.BlockSpec((tm, nj), row), pl.BlockSpec((tm, nj), row),
                  pl.BlockSpec((eb, d), lambda i, s: (jnp.minimum(s, ns - 1), 0)),
                  pl.BlockSpec((eb, d), lambda i, s: (jnp.maximum(s - ns, 0), 0))],
        out_specs=pl.BlockSpec((tm, d), row),
        out_shape=jax.ShapeDtypeStruct((t, d), F32),
        scratch_shapes=[pltpu.VMEM((tm, d), BF16),
                        pltpu.VMEM((tm, nj), F32),
                        pltpu.VMEM((tm, nj), F32),
                        pltpu.VMEM((tm * PEER_NKEYS, PEER_NKEYS), F32),
                        pltpu.VMEM((tm, d), F32)],
        compiler_params=_cparams(("arbitrary", "arbitrary")),
        name="peer",
    )(h, sh2, sc2, g2, ng2, fg, eidx_t, gate_t, u, v)


def _layer(x2, c, bsz, seq, w_ada, b_ada, norm1_g, w_in, conv_w, conv_b, dt_bias_f, dt_bias_b,
           a_log_f, a_log_b, d_skip, ssd_norm_g, w_ssd_br, na_rpb, w_na_br, w_out, norm2_g,
           peer_wq, peer_keys, peer_u, peer_v, out_g):
    t, d = x2.shape
    d_ssd = SSD_HEADS * SSD_HEADDIM
    d_xbc = d_ssd + 2 * SSD_GROUPS * SSD_STATE
    d_na = NA_HEADS * NA_HEADDIM
    assert seq % SSD_CHUNK == 0 and seq % (GRID_W * NA_QROWS) == 0
    assert seq // GRID_W >= NA_KROWS and d == d_ssd == d_na

    mod = _ada(c, w_ada, b_ada)
    sh1, sc1, g1, sh2, sc2, g2 = [m.reshape(bsz, 1, d) for m in jnp.split(mod, 6, axis=-1)]

    o = np.cumsum([0, d_ssd, d_xbc, SSD_HEADS, SSD_HEADS, 3 * d_na, d, d])
    w_main = jnp.concatenate([w_in[:, o[1]:o[2]], w_in[:, o[0]:o[1]], w_in[:, o[4]:o[7]]],
                             axis=1).astype(BF16)
    w_dt = jnp.pad(w_in[:, o[2]:o[4]], ((0, 0), (0, LANES - 2 * SSD_HEADS))).astype(BF16)
    z_col = d_xbc // d_ssd
    q_col = (d_xbc + d_ssd) // LANES
    gs_col = (d_xbc + d_ssd + 3 * d_na) // d
    proj, dt_raw = _inproj(x2, sh1, sc1, norm1_g.reshape(1, d), w_main, w_dt, seq)

    heads = np.arange(d_ssd) // SSD_HEADDIM

    def e2(off):
        m = np.zeros((LANES, d_ssd), np.float32)
        m[off + heads, np.arange(d_ssd)] = 1.0
        m[off + 64 + heads, np.arange(d_ssd)] = 1.0
        return jnp.asarray(m, BF16)

    def lanes16(vec, off):
        return jnp.zeros((1, LANES), F32).at[0, off:off + SSD_HEADS].set(vec.astype(F32))

    params = {
        "cw": jnp.pad(conv_w.astype(F32), ((0, 8 - CONV_W), (0, 0))),
        "cb": conv_b.astype(F32).reshape(1, d_xbc),
        "dtb_f": lanes16(dt_bias_f, 0), "dtb_b": lanes16(dt_bias_b, SSD_HEADS),
        "a_f": lanes16(-jnp.exp(a_log_f.astype(F32)), 0),
        "a_b": lanes16(-jnp.exp(a_log_b.astype(F32)), SSD_HEADS),
        "e2_f": e2(0), "e2_b": e2(SSD_HEADS),
        "dsk": jnp.repeat(d_skip.astype(F32), SSD_HEADDIM).reshape(1, d_ssd),
        "ng": ssd_norm_g.astype(F32).reshape(1, d_ssd),
    }
    yf = _ssd(proj, dt_raw, z_col, None, params, bsz, seq, rev=False)
    ynorm = _ssd(proj, dt_raw, z_col, yf, params, bsz, seq, rev=True)

    bias, var_of_blk, ks_of_blk = _na_bias(na_rpb, seq // GRID_W)
    attn = _na(proj, bias, var_of_blk, ks_of_blk, q_col, bsz, seq)

    h, scores = _mix(x2, ynorm, attn, proj, gs_col, g1, sh2, sc2, norm2_g.reshape(1, d),
                     w_ssd_br.astype(BF16), w_na_br.astype(BF16), w_out.astype(BF16),
                     peer_wq.astype(BF16), peer_keys.astype(BF16), seq)

    eidx, gate = _select(scores.reshape(2 * PEER_HEADS, PEER_NKEYS, t // LANES, LANES))
    nj = PEER_HEADS * PEER_TOPK
    eidx_t = eidx.reshape(nj, t).T
    gate_t = gate.reshape(nj, t).T
    return _peer(h, sh2, sc2, g2, norm2_g.reshape(1, d), out_g, eidx_t, gate_t,
                 peer_u.astype(BF16), peer_v.astype(BF16), seq)


def kernel(x, c, w_ada, b_ada, norm1_g, w_in, conv_w, conv_b, dt_bias_f, dt_bias_b, a_log_f, a_log_b,
           d_skip, ssd_norm_g, w_ssd_br, na_rpb, w_na_br, w_out, norm2_g, peer_wq, peer_keys, peer_u,
           peer_v, final_g):
    bsz, seq, d = x.shape
    depth = w_ada.shape[0]
    assert depth == 1, "the final RMSNorm is fused into the last layer's PEER kernel"
    i = 0
    out = _layer(x.reshape(bsz * seq, d), c, bsz, seq, w_ada[i], b_ada[i], norm1_g[i], w_in[i],
                 conv_w[i], conv_b[i], dt_bias_f[i], dt_bias_b[i], a_log_f[i], a_log_b[i], d_skip[i],
                 ssd_norm_g[i], w_ssd_br[i], na_rpb[i], w_na_br[i], w_out[i], norm2_g[i],
                 peer_wq[i], peer_keys[i], peer_u[i], peer_v[i], final_g.reshape(1, d))
    return out.reshape(bsz, seq, d)
```

```python
import functools

import numpy as np
import jax
import jax.numpy as jnp
from jax import lax
from jax.experimental import pallas as pl
from jax.experimental.pallas import tpu as pltpu

F32 = jnp.float32
BF16 = jnp.bfloat16
I32 = jnp.int32

EPS = 1e-6
GRID_W = 64
SSD_HEADS = 16
SSD_HEADDIM = 64
SSD_GROUPS = 4
SSD_STATE = 128
SSD_CHUNK = 128
CONV_W = 5
NA_HEADS = 16
NA_HEADDIM = 64
NA_KH = 8
NA_KW = 16
NA_QROWS = 4
NA_KROWS = 12
PEER_HEADS = 8
PEER_NKEYS = 128
PEER_TOPK = 16
PEER_DHALF = 128

LANES = 128
HALO = 16
VMEM_LIMIT = 56 * 1024 * 1024


def _cparams(sem):
    return pltpu.CompilerParams(dimension_semantics=sem, vmem_limit_bytes=VMEM_LIMIT)


def _sigmoid(x):
    return 1.0 / (1.0 + jnp.exp(-x))


def _rms_mod(x, g, shift, scale):
    ms = jnp.mean(x * x, axis=-1, keepdims=True)
    y = x * lax.rsqrt(ms + EPS) * g
    return y * (1.0 + scale) + shift


def _ada_kernel(c_ref, w_ref, b_ref, o_ref):
    c = c_ref[...]
    sc = c * _sigmoid(c)
    o_ref[...] = jnp.dot(sc, w_ref[...], preferred_element_type=F32,
                         precision=lax.Precision.HIGHEST) + b_ref[...]


def _ada(c, w, b):
    bsz, d = c.shape
    n = w.shape[1]
    tn = 1024
    return pl.pallas_call(
        _ada_kernel,
        grid=(n // tn,),
        in_specs=[pl.BlockSpec((bsz, d), lambda j: (0, 0)),
                  pl.BlockSpec((d, tn), lambda j: (0, j)),
                  pl.BlockSpec((1, tn), lambda j: (0, j))],
        out_specs=pl.BlockSpec((bsz, tn), lambda j: (0, j)),
        out_shape=jax.ShapeDtypeStruct((bsz, n), F32),
        compiler_params=_cparams(("arbitrary",)),
        name="ada",
    )(c, w, b.reshape(1, n))


def _inproj_kernel(x_ref, sh_ref, sc_ref, g_ref, w_ref, wdt_ref, proj_ref, dt_ref, n1_ref):
    @pl.when(pl.program_id(1) == 0)
    def _():
        n1 = _rms_mod(x_ref[...], g_ref[...], sh_ref[0], sc_ref[0]).astype(BF16)
        n1_ref[...] = n1
        dt_ref[...] = jnp.dot(n1, wdt_ref[...], preferred_element_type=F32)

    proj_ref[...] = jnp.dot(n1_ref[...], w_ref[...], preferred_element_type=F32).astype(BF16)


def _inproj(x2, sh, sc, g, w, wdt, seq):
    t, d = x2.shape
    n = w.shape[1]
    tm = min(1024, seq)
    tn = 1024
    bidx = lambda i, j: ((i * tm) // seq, 0, 0)
    return pl.pallas_call(
        _inproj_kernel,
        grid=(t // tm, n // tn),
        in_specs=[pl.BlockSpec((tm, d), lambda i, j: (i, 0)),
                  pl.BlockSpec((1, 1, d), bidx),
                  pl.BlockSpec((1, 1, d), bidx),
                  pl.BlockSpec((1, d), lambda i, j: (0, 0)),
                  pl.BlockSpec((d, tn), lambda i, j: (0, j)),
                  pl.BlockSpec((d, LANES), lambda i, j: (0, 0))],
        out_specs=[pl.BlockSpec((tm, tn), lambda i, j: (i, j)),
                   pl.BlockSpec((tm, LANES), lambda i, j: (i, 0))],
        out_shape=[jax.ShapeDtypeStruct((t, n), BF16),
                   jax.ShapeDtypeStruct((t, LANES), F32)],
        scratch_shapes=[pltpu.VMEM((tm, d), BF16)],
        compiler_params=_cparams(("arbitrary", "arbitrary")),
        name="inproj",
    )(x2, sh, sc, g, w, wdt)


def _expand_heads(v, e2_ref):
    hi = v.astype(BF16).astype(F32)
    comb = (hi + pltpu.roll(v - hi, 64, 1)).astype(BF16)
    return jnp.dot(comb, e2_ref[...], preferred_element_type=F32)


def _ssd_kernel(*refs, rev):
    if rev:
        (xm_ref, xp_ref, xn_ref, dt_ref, z_ref, yf_ref, cw_ref, cb_ref, dtb_ref, a_ref,
         e2_ref, dsk_ref, ng_ref, out_ref, ext_ref, st_ref) = refs
    else:
        (xm_ref, xp_ref, xn_ref, dt_ref, cw_ref, cb_ref, dtb_ref, a_ref,
         e2_ref, out_ref, ext_ref, st_ref) = refs
    ck = SSD_CHUNK
    d_ssd = SSD_HEADS * SSD_HEADDIM
    gw = d_ssd // SSD_GROUPS
    c = pl.program_id(1)
    nc = pl.num_programs(1)
    ce = nc - 1 - c if rev else c

    @pl.when(c == 0)
    def _():
        st_ref[...] = jnp.zeros_like(st_ref)

    prev = jnp.where(ce == 0, 0.0, xp_ref[0].astype(F32))
    nxt = jnp.where(ce == nc - 1, 0.0, xn_ref[0].astype(F32))
    ext_ref[0:HALO, :] = prev
    ext_ref[HALO:HALO + ck, :] = xm_ref[...].astype(F32)
    ext_ref[HALO + ck:HALO + ck + HALO, :] = nxt
    acc = jnp.zeros((ck, xm_ref.shape[1]), F32) + cb_ref[...]
    for w in range(CONV_W):
        o = HALO - CONV_W // 2 + w
        acc = acc + ext_ref[o:o + ck, :] * cw_ref[w:w + 1, :]
    xbc = acc * _sigmoid(acc)
    xs = xbc[:, :d_ssd]
    bm = xbc[:, d_ssd:d_ssd + SSD_GROUPS * SSD_STATE]
    cm = xbc[:, d_ssd + SSD_GROUPS * SSD_STATE:]

    lane = lax.broadcasted_iota(I32, (ck, LANES), 1)
    row = lax.broadcasted_iota(I32, (ck, LANES), 0)
    off = SSD_HEADS if rev else 0
    valid = (lane >= off) & (lane < off + SSD_HEADS)
    dtr = dt_ref[...] + dtb_ref[...]
    dt = jnp.where(valid, jnp.maximum(dtr, 0.0) + jnp.log1p(jnp.exp(-jnp.abs(dtr))), 0.0)
    da = dt * a_ref[...]
    tri = (lane >= row) if rev else (lane <= row)
    trib = jnp.where(tri, 1.0, 0.0).astype(BF16)
    hi = da.astype(BF16)
    r1 = da - hi.astype(F32)
    mid = r1.astype(BF16)
    lo = (r1 - mid.astype(F32)).astype(BF16)
    cs = (jnp.dot(trib, hi, preferred_element_type=F32)
          + jnp.dot(trib, mid, preferred_element_type=F32)
          + jnp.dot(trib, lo, preferred_element_type=F32))
    cst = cs.T
    tot_row = 0 if rev else ck - 1
    tot = cs[tot_row:tot_row + 1, :]
    dec = jnp.where(valid, jnp.exp(tot - cs), 0.0)
    ecs = jnp.where(valid, jnp.exp(cs), 0.0)
    ecs_e = _expand_heads(ecs, e2_ref)
    xdt = xs * _expand_heads(dt, e2_ref)
    xdec = (xs * _expand_heads(dt * dec, e2_ref)).astype(BF16)

    lo_half = lax.broadcasted_iota(I32, (ck, LANES), 1) < SSD_HEADDIM
    hpg = SSD_HEADS // SSD_GROUPS
    ys = []
    for g in range(SSD_GROUPS):
        bg = bm[:, g * SSD_STATE:(g + 1) * SSD_STATE]
        cg = cm[:, g * SSD_STATE:(g + 1) * SSD_STATE].astype(BF16)
        cbg = lax.dot_general(cg, bg.astype(BF16), (((1,), (1,)), ((), ())),
                              preferred_element_type=F32)
        yg = []
        for q in range(hpg // 2):
            ls = []
            for e in range(2):
                col = off + g * hpg + q * 2 + e
                dlt = cs[:, col:col + 1] - cst[col:col + 1, :]
                seg = jnp.exp(jnp.where(tri, dlt, -jnp.inf))
                ls.append((cbg * seg).astype(BF16))
            blk = g * (hpg // 2) + q
            xq = xdt[:, blk * LANES:(blk + 1) * LANES]
            rhs = jnp.concatenate([jnp.where(lo_half, xq, 0.0), jnp.where(lo_half, 0.0, xq)],
                                  axis=0).astype(BF16)
            yg.append(jnp.dot(jnp.concatenate(ls, axis=1), rhs, preferred_element_type=F32))
        y_diag = jnp.concatenate(yg, axis=1)
        st = st_ref[g]
        e_g = ecs_e[:, g * gw:(g + 1) * gw]
        y_off = jnp.dot(cg, st.astype(BF16), preferred_element_type=F32) * e_g
        ys.append(y_diag + y_off)
        upd = jnp.dot(bg.T.astype(BF16), xdec[:, g * gw:(g + 1) * gw], preferred_element_type=F32)
        st_ref[g] = st * e_g[tot_row:tot_row + 1, :] + upd
    y = jnp.concatenate(ys, axis=1)

    if rev:
        y = yf_ref[...] + y + dsk_ref[...] * xs
        z = z_ref[...].astype(F32)
        u = y * (z * _sigmoid(z))
        ms = jnp.mean(u * u, axis=-1, keepdims=True)
        out_ref[...] = (u * lax.rsqrt(ms + EPS) * ng_ref[...]).astype(BF16)
    else:
        out_ref[...] = y


def _ssd(proj, dt_raw, z_col, yf, params, bsz, seq, rev):
    t = proj.shape[0]
    ck = SSD_CHUNK
    nc = seq // ck
    d_xbc = params["cw"].shape[1]
    d_ssd = SSD_HEADS * SSD_HEADDIM
    proj16 = proj.reshape(t // HALO, HALO, proj.shape[1])
    per = ck // HALO
    last16 = t // HALO - 1

    def ce_of(c):
        return nc - 1 - c if rev else c

    main = lambda b, c: (b * nc + ce_of(c), 0)
    prev = lambda b, c: (jnp.maximum((b * nc + ce_of(c)) * per - 1, 0), 0, 0)
    nxt = lambda b, c: (jnp.minimum((b * nc + ce_of(c)) * per + per, last16), 0, 0)
    const2 = lambda b, c: (0, 0)
    in_specs = [pl.BlockSpec((ck, d_xbc), main),
                pl.BlockSpec((1, HALO, d_xbc), prev),
                pl.BlockSpec((1, HALO, d_xbc), nxt),
                pl.BlockSpec((ck, LANES), main)]
    args = [proj, proj16, proj16, dt_raw]
    if rev:
        in_specs += [pl.BlockSpec((ck, d_ssd), lambda b, c: (b * nc + ce_of(c), z_col)),
                     pl.BlockSpec((ck, d_ssd), main)]
        args += [proj, yf]
    in_specs += [pl.BlockSpec((8, d_xbc), const2), pl.BlockSpec((1, d_xbc), const2),
                 pl.BlockSpec((1, LANES), const2), pl.BlockSpec((1, LANES), const2),
                 pl.BlockSpec((LANES, d_ssd), const2)]
    d = "b" if rev else "f"
    args += [params["cw"], params["cb"], params["dtb_" + d], params["a_" + d], params["e2_" + d]]
    if rev:
        in_specs += [pl.BlockSpec((1, d_ssd), const2), pl.BlockSpec((1, d_ssd), const2)]
        args += [params["dsk"], params["ng"]]
    return pl.pallas_call(
        functools.partial(_ssd_kernel, rev=rev),
        grid=(bsz, nc),
        in_specs=in_specs,
        out_specs=pl.BlockSpec((ck, d_ssd), main),
        out_shape=jax.ShapeDtypeStruct((t, d_ssd), BF16 if rev else F32),
        scratch_shapes=[pltpu.VMEM((ck + 2 * HALO, d_xbc), F32),
                        pltpu.VMEM((SSD_GROUPS, SSD_STATE, d_ssd // SSD_GROUPS), F32)],
        compiler_params=_cparams(("arbitrary", "arbitrary")),
        name="ssd_bwd" if rev else "ssd_fwd",
    )(*args)


def _na_plan(rows):
    nblk = rows // NA_QROWS
    variants, var_of_blk, ks_of_blk = [], [], []
    for i in range(nblk):
        ks = int(np.clip(i * NA_QROWS - NA_KH // 2, 0, rows - NA_KROWS))
        drow = np.zeros((NA_QROWS, NA_KROWS), np.int64)
        ok = np.zeros((NA_QROWS, NA_KROWS), bool)
        for dr in range(NA_QROWS):
            r = i * NA_QROWS + dr
            rs = int(np.clip(r - NA_KH // 2, 0, rows - NA_KH))
            for j in range(NA_KH):
                w = rs + j - ks
                drow[dr, w] = rs + j - r + NA_KH - 1
                ok[dr, w] = True
        key = (drow.tobytes(), ok.tobytes())
        for vi, (k2, _, _) in enumerate(variants):
            if k2 == key:
                var_of_blk.append(vi)
                break
        else:
            var_of_blk.append(len(variants))
            variants.append((key, drow, ok))
        ks_of_blk.append(ks)
    return [(d, o) for _, d, o in variants], var_of_blk, ks_of_blk


def _na_bias(rpb, rows):
    variants, var_of_blk, ks_of_blk = _na_plan(rows)
    cols = np.arange(GRID_W)
    win_start = np.clip(cols - NA_KW // 2, 0, GRID_W - NA_KW)
    in_win = (cols[None, :] >= win_start[:, None]) & (cols[None, :] < win_start[:, None] + NA_KW)
    dcol = np.clip(cols[None, :] - cols[:, None] + NA_KW - 1, 0, 2 * NA_KW - 2)
    onehot = (dcol[:, :, None] == np.arange(2 * NA_KW - 1)).astype(np.float32)
    band = jnp.einsum("hab,qkb->haqk", rpb.astype(F32), onehot, precision=lax.Precision.HIGHEST)
    band = jnp.where(in_win[None, None], band, -jnp.inf)
    neg = jnp.full((rpb.shape[0], GRID_W, GRID_W), -jnp.inf, F32)
    tabs = []
    for drow, ok in variants:
        qrows = [jnp.concatenate([band[:, drow[dr, w]] if ok[dr, w] else neg
                                  for w in range(NA_KROWS)], axis=2) for dr in range(NA_QROWS)]
        tabs.append(jnp.concatenate(qrows, axis=1))
    return jnp.stack(tabs), var_of_blk, ks_of_blk


def _na_kernel(var_ref, ks_ref, q_ref, k_ref, v_ref, bias_ref, o_ref):
    mq = NA_QROWS * GRID_W
    nk = NA_KROWS * GRID_W
    nblk = q_ref.shape[0] // mq
    lo_half = lax.broadcasted_iota(I32, (mq, LANES), 1) < NA_HEADDIM

    def blk(i, carry):
        q0 = pl.multiple_of(i * mq, mq)
        k0 = pl.multiple_of(ks_ref[i] * GRID_W, GRID_W)
        qb = q_ref[pl.ds(q0, mq), :] * (NA_HEADDIM ** -0.5)
        zero = jnp.zeros_like(qb)
        q2 = jnp.concatenate([jnp.where(lo_half, qb, zero), jnp.where(lo_half, zero, qb)], axis=0)
        kb = k_ref[pl.ds(k0, nk), :]
        vb = v_ref[pl.ds(k0, nk), :]
        s = lax.dot_general(q2, kb, (((1,), (1,)), ((), ())), preferred_element_type=F32)
        s = s + bias_ref[var_ref[i]].reshape(2 * mq, nk)
        m = jnp.max(s, axis=-1, keepdims=True)
        p = jnp.exp(s - m)
        l = jnp.sum(p, axis=-1, keepdims=True)
        o2 = jnp.dot(p.astype(BF16), vb, preferred_element_type=F32) / l
        o_ref[pl.ds(q0, mq), :] = jnp.where(lo_half, o2[:mq], o2[mq:]).astype(BF16)
        return carry

    lax.fori_loop(0, nblk, blk, 0)


def _na(proj, bias, var_of_blk, ks_of_blk, q_col, bsz, seq):
    t = proj.shape[0]
    npair = NA_HEADS // 2
    nv = bias.shape[0]
    mq = NA_QROWS * GRID_W
    nk = NA_KROWS * GRID_W
    grid_spec = pltpu.PrefetchScalarGridSpec(
        num_scalar_prefetch=2,
        grid=(bsz, npair),
        in_specs=[pl.BlockSpec((seq, LANES), lambda b, p, *_: (b, q_col + p)),
                  pl.BlockSpec((seq, LANES), lambda b, p, *_: (b, q_col + npair + p)),
                  pl.BlockSpec((seq, LANES), lambda b, p, *_: (b, q_col + 2 * npair + p)),
                  pl.BlockSpec((nv, 2, mq, nk), lambda b, p, *_: (0, p, 0, 0))],
        out_specs=pl.BlockSpec((seq, LANES), lambda b, p, *_: (b, p)),
    )
    return pl.pallas_call(
        _na_kernel,
        grid_spec=grid_spec,
        out_shape=jax.ShapeDtypeStruct((t, NA_HEADS * NA_HEADDIM), BF16),
        compiler_params=_cparams(("arbitrary", "arbitrary")),
        name="na",
    )(jnp.asarray(var_of_blk, I32), jnp.asarray(ks_of_blk, I32), proj, proj, proj, bias)


def _mix_kernel(x_ref, yn_ref, at_ref, gs_ref, gn_ref, g1_ref, sh_ref, sc_ref, ng_ref,
                ws_ref, wn_ref, wo_ref, wq_ref, keys_ref, h_ref, s_ref):
    y1 = jnp.dot(yn_ref[...], ws_ref[...], preferred_element_type=F32)
    y2 = jnp.dot(at_ref[...], wn_ref[...], preferred_element_type=F32)
    mixed = _sigmoid(gs_ref[...].astype(F32)) * y1 + _sigmoid(gn_ref[...].astype(F32)) * y2
    h = x_ref[...] + g1_ref[0] * jnp.dot(mixed.astype(BF16), wo_ref[...], preferred_element_type=F32)
    h_ref[...] = h
    n2 = _rms_mod(h, ng_ref[...], sh_ref[0], sc_ref[0]).astype(BF16)
    q = jnp.dot(n2, wq_ref[...], preferred_element_type=F32).astype(BF16)
    for hz in range(2 * PEER_HEADS):
        qs = q[:, hz * PEER_DHALF:(hz + 1) * PEER_DHALF]
        s_ref[hz] = lax.dot_general(keys_ref[hz % 2], qs, (((1,), (1,)), ((), ())),
                                    preferred_element_type=F32)


def _mix(x2, ynorm, attn, proj, gs_col, g1, sh2, sc2, ng2, ws, wn, wo, wq, keys, seq):
    t, d = x2.shape
    tm = min(256, seq)
    bidx = lambda i: ((i * tm) // seq, 0, 0)
    row = lambda i: (i, 0)
    const2 = lambda i: (0, 0)
    return pl.pallas_call(
        _mix_kernel,
        grid=(t // tm,),
        in_specs=[pl.BlockSpec((tm, d), row), pl.BlockSpec((tm, d), row), pl.BlockSpec((tm, d), row),
                  pl.BlockSpec((tm, d), lambda i: (i, gs_col)),
                  pl.BlockSpec((tm, d), lambda i: (i, gs_col + 1)),
                  pl.BlockSpec((1, 1, d), bidx), pl.BlockSpec((1, 1, d), bidx),
                  pl.BlockSpec((1, 1, d), bidx), pl.BlockSpec((1, d), const2),
                  pl.BlockSpec((d, d), const2), pl.BlockSpec((d, d), const2),
                  pl.BlockSpec((d, d), const2), pl.BlockSpec(wq.shape, const2),
                  pl.BlockSpec(keys.shape, lambda i: (0, 0, 0))],
        out_specs=[pl.BlockSpec((tm, d), row),
                   pl.BlockSpec((2 * PEER_HEADS, PEER_NKEYS, tm), lambda i: (0, 0, i))],
        out_shape=[jax.ShapeDtypeStruct((t, d), F32),
                   jax.ShapeDtypeStruct((2 * PEER_HEADS, PEER_NKEYS, t), F32)],
        compiler_params=_cparams(("arbitrary",)),
        name="mix",
    )(x2, ynorm, attn, proj, proj, g1, sh2, sc2, ng2, ws, wn, wo, wq, keys)


def _hyperbola():
    return [(i, k) for i in range(PEER_TOPK) for k in range(PEER_TOPK)
            if (i + 1) * (k + 1) <= PEER_TOPK]


def _sort_network(n):
    pairs = []
    p = 1
    while p < n:
        k = p
        while k >= 1:
            for j in range(k % p, n - k, 2 * k):
                for i in range(min(k, n - j - k)):
                    if (i + j) // (2 * p) == (i + j + k) // (2 * p):
                        pairs.append((i + j, i + j + k))
            k //= 2
        p *= 2
    return pairs


def _precedes(va, pa, vb, pb):
    return (va > vb) | ((va == vb) & (pa < pb))


def _compare_exchange(v, p, i, j):
    c = _precedes(v[i], p[i], v[j], p[j])
    v[i], v[j] = jnp.where(c, v[i], v[j]), jnp.where(c, v[j], v[i])
    p[i], p[j] = jnp.where(c, p[i], p[j]), jnp.where(c, p[j], p[i])


def _top16(get, n, sv_ref, sp_ref):
    k = PEER_TOPK
    ng = n // k
    net = _sort_network(k)
    for g in range(ng):
        items = [get(g * k + j) for j in range(k)]
        v = [it[0] for it in items]
        p = [it[1] for it in items]
        for i, j in net:
            _compare_exchange(v, p, i, j)
        if ng == 1:
            return v, p
        for j in range(k):
            sv_ref[g, j] = v[j]
            sp_ref[g, j] = p[j]
    step = 1
    while True:
        for g in range(0, ng, 2 * step):
            v, p = [], []
            for j in range(k):
                xv, xp = sv_ref[g, j], sp_ref[g, j]
                yv, yp = sv_ref[g + step, k - 1 - j], sp_ref[g + step, k - 1 - j]
                c = _precedes(xv, xp, yv, yp)
                v.append(jnp.where(c, xv, yv))
                p.append(jnp.where(c, xp, yp))
            stride = k // 2
            while stride >= 1:
                for i in range(k):
                    if i & stride == 0:
                        _compare_exchange(v, p, i, i + stride)
                stride //= 2
            if 2 * step >= ng:
                return v, p
            for j in range(k):
                sv_ref[g, j] = v[j]
                sp_ref[g, j] = p[j]
        step *= 2


def _select_kernel(s_ref, eidx_ref, gate_ref, sv_ref, sp_ref):
    nk = PEER_NKEYS
    k = PEER_TOPK
    shape = s_ref.shape[2:]
    ebits = (nk * nk - 1).bit_length()

    tops = []
    for z in range(2):
        tops.append(_top16(lambda j, z=z: (s_ref[z, j], jnp.full(shape, j, I32)), nk, sv_ref, sp_ref))
    (tv0, ti0), (tv1, ti1) = tops

    pairs = _hyperbola()
    npad = -len(pairs) % k

    def cand(j):
        if j >= len(pairs):
            return jnp.full(shape, -jnp.inf, F32), jnp.full(shape, (k * k) << ebits, I32)
        i, kk = pairs[j]
        return tv0[i] + tv1[kk], ((i * k + kk) << ebits) + ti0[i] * nk + ti1[kk]

    top, ids = _top16(cand, len(pairs) + npad, sv_ref, sp_ref)
    ex = [jnp.exp(t - top[0]) for t in top]
    zsum = ex[0]
    for r in range(1, k):
        zsum = zsum + ex[r]
    for r in range(k):
        eidx_ref[0, r] = ids[r] & ((1 << ebits) - 1)
        gate_ref[0, r] = ex[r] / zsum


def _select(s4):
    nhz, nk, tb, _ = s4.shape
    r = min(8, tb)
    oshape = (PEER_HEADS, PEER_TOPK, tb, LANES)
    return pl.pallas_call(
        _select_kernel,
        grid=(tb // r, PEER_HEADS),
        in_specs=[pl.BlockSpec((2, nk, r, LANES), lambda i, h: (h, 0, i, 0))],
        out_specs=[pl.BlockSpec((1, PEER_TOPK, r, LANES), lambda i, h: (h, 0, i, 0)),
                   pl.BlockSpec((1, PEER_TOPK, r, LANES), lambda i, h: (h, 0, i, 0))],
        out_shape=[jax.ShapeDtypeStruct(oshape, I32), jax.ShapeDtypeStruct(oshape, F32)],
        scratch_shapes=[pltpu.VMEM((nk // PEER_TOPK, PEER_TOPK, r, LANES), F32),
                        pltpu.VMEM((nk // PEER_TOPK, PEER_TOPK, r, LANES), I32)],
        compiler_params=_cparams(("arbitrary", "arbitrary")),
        name="select",
    )(s4)


def _peer_kernel(h_ref, sh_ref, sc_ref, g2_ref, ng_ref, fg_ref, eidx_ref, gate_ref, u_ref, v0_ref,
                 v1_ref, out_ref, n2_ref, act_ref, wv_ref, w3_ref, acc_ref, *, ns):
    nk = PEER_NKEYS
    half = nk // 2
    hi_mask = jnp.uint32(0xFFFF0000)
    tm = h_ref.shape[0]
    nblk = u_ref.shape[0] // nk
    nblk2 = v0_ref.shape[0] // nk
    s = pl.program_id(1)

    @pl.when(s == 0)
    def _():
        n2_ref[...] = _rms_mod(h_ref[...], ng_ref[...], sh_ref[0], sc_ref[0]).astype(BF16)
        act_ref[...] = jnp.zeros_like(act_ref)

    @pl.when(s < ns)
    def _():
        a_all = lax.dot_general(n2_ref[...], u_ref[...], (((1,), (1,)), ((), ())),
                                preferred_element_type=F32)
        e = eidx_ref[...]
        ai = jnp.right_shift(e, nk.bit_length() - 1)
        bi = jnp.bitwise_and(e, nk - 1)
        act = act_ref[...]
        for a in range(nblk):
            got = jnp.take_along_axis(a_all[:, a * nk:(a + 1) * nk], bi, axis=1)
            act = jnp.where(ai == s * nblk + a, got, act)
        act_ref[...] = act

    @pl.when(s == ns - 1)
    def _():
        act = act_ref[...]
        gelu = 0.5 * act * (1.0 + lax.erf(act * (2.0 ** -0.5)))
        wv_ref[...] = gate_ref[...] * gelu
        sub = lax.broadcasted_iota(I32, (nk, LANES), 0)

        def tok(t, carry):
            e = eidx_ref[pl.ds(t, 1), :]
            w = wv_ref[pl.ds(t, 1), :]
            ar = jnp.right_shift(e, nk.bit_length() - 1)
            br = jnp.bitwise_and(e, nk - 1)
            pt = jnp.where(sub == ar, w, 0.0).astype(BF16)
            qt = jnp.where(sub == br, 1.0, 0.0).astype(BF16)
            wt = lax.dot_general(pt, qt, (((1,), (1,)), ((), ())), preferred_element_type=F32)
            bits = lax.bitcast_convert_type(wt.astype(BF16).astype(F32), jnp.uint32)
            word = jnp.right_shift(bits[:half], 16) | (bits[half:] & hi_mask)
            w3_ref[pl.ds(pl.multiple_of(t * half, half), half), :] = word
            return carry
        lax.fori_loop(0, tm, tok, 0, unroll=8)

    @pl.when(s >= ns)
    def _():
        base = (s - ns) * nblk2
        words = [w3_ref[pl.ds(base + a, tm, stride=half), :] for a in range(nblk2)]
        lo = [lax.bitcast_convert_type(jnp.left_shift(w, 16), F32).astype(BF16) for w in words]
        hi = [lax.bitcast_convert_type(w & hi_mask, F32).astype(BF16) for w in words]
        part = (jnp.dot(jnp.concatenate(lo, axis=1), v0_ref[...], preferred_element_type=F32)
                + jnp.dot(jnp.concatenate(hi, axis=1), v1_ref[...], preferred_element_type=F32))

        @pl.when(s == ns)
        def _():
            acc_ref[...] = part

        @pl.when(s > ns)
        def _():
            acc_ref[...] += part

    @pl.when(s == 2 * ns - 1)
    def _():
        hh = h_ref[...] + g2_ref[0] * acc_ref[...]
        ms = jnp.mean(hh * hh, axis=-1, keepdims=True)
        out_ref[...] = hh * lax.rsqrt(ms + EPS) * fg_ref[...]


def _peer(h, sh2, sc2, g2, ng2, fg, eidx_t, gate_t, u, v, seq):
    t, d = h.shape
    ne = u.shape[0]
    tm = min(512, seq)
    eb = 2048
    ns = ne // eb
    nj = eidx_t.shape[1]
    bidx = lambda i, s: ((i * tm) // seq, 0, 0)
    row = lambda i, s: (i, 0)
    const2 = lambda i, s: (0, 0)
    return pl.pallas_call(
        functools.partial(_peer_kernel, ns=ns),
        grid=(t // tm, 2 * ns),
        in_specs=[pl.BlockSpec((tm, d), row),
                  pl.BlockSpec((1, 1, d), bidx), pl.BlockSpec((1, 1, d), bidx),
                  pl.BlockSpec((1, 1, d), bidx),
                  pl.BlockSpec((1, d), const2), pl.BlockSpec((1, d), const2),
                  pl.BlockSpec((tm, nj), row), pl.BlockSpec((tm, nj), row),
                  pl.BlockSpec((eb, d), lambda i, s: (jnp.minimum(s, ns - 1), 0)),
                  pl.BlockSpec((eb // 2, d), lambda i, s: (jnp.maximum(s - ns, 0), 0)),
                  pl.BlockSpec((eb // 2, d), lambda i, s: (ns + jnp.maximum(s - ns, 0), 0))],
        out_specs=pl.BlockSpec((tm, d), row),
        out_shape=jax.ShapeDtypeStruct((t, d), F32),
        scratch_shapes=[pltpu.VMEM((tm, d), BF16),
                        pltpu.VMEM((tm, nj), F32),
                        pltpu.VMEM((tm, nj), F32),
                        pltpu.VMEM((tm * PEER_NKEYS // 2, PEER_NKEYS), jnp.uint32),
                        pltpu.VMEM((tm, d), F32)],
        compiler_params=_cparams(("arbitrary", "arbitrary")),
        name="peer",
    )(h, sh2, sc2, g2, ng2, fg, eidx_t, gate_t, u, v, v)


def _layer(x2, c, bsz, seq, w_ada, b_ada, norm1_g, w_in, conv_w, conv_b, dt_bias_f, dt_bias_b,
           a_log_f, a_log_b, d_skip, ssd_norm_g, w_ssd_br, na_rpb, w_na_br, w_out, norm2_g,
           peer_wq, peer_keys, peer_u, peer_v, out_g):
    t, d = x2.shape
    d_ssd = SSD_HEADS * SSD_HEADDIM
    d_xbc = d_ssd + 2 * SSD_GROUPS * SSD_STATE
    d_na = NA_HEADS * NA_HEADDIM
    assert seq % SSD_CHUNK == 0 and seq % (GRID_W * NA_QROWS) == 0
    assert seq // GRID_W >= NA_KROWS and d == d_ssd == d_na

    mod = _ada(c, w_ada, b_ada)
    sh1, sc1, g1, sh2, sc2, g2 = [m.reshape(bsz, 1, d) for m in jnp.split(mod, 6, axis=-1)]

    o = np.cumsum([0, d_ssd, d_xbc, SSD_HEADS, SSD_HEADS, 3 * d_na, d, d])
    w_main = jnp.concatenate([w_in[:, o[1]:o[2]], w_in[:, o[0]:o[1]], w_in[:, o[4]:o[7]]],
                             axis=1).astype(BF16)
    w_dt = jnp.pad(w_in[:, o[2]:o[4]], ((0, 0), (0, LANES - 2 * SSD_HEADS))).astype(BF16)
    z_col = d_xbc // d_ssd
    q_col = (d_xbc + d_ssd) // LANES
    gs_col = (d_xbc + d_ssd + 3 * d_na) // d
    proj, dt_raw = _inproj(x2, sh1, sc1, norm1_g.reshape(1, d), w_main, w_dt, seq)

    heads = np.arange(d_ssd) // SSD_HEADDIM

    def e2(off):
        m = np.zeros((LANES, d_ssd), np.float32)
        m[off + heads, np.arange(d_ssd)] = 1.0
        m[off + 64 + heads, np.arange(d_ssd)] = 1.0
        return jnp.asarray(m, BF16)

    def lanes16(vec, off):
        return jnp.zeros((1, LANES), F32).at[0, off:off + SSD_HEADS].set(vec.astype(F32))

    params = {
        "cw": jnp.pad(conv_w.astype(F32), ((0, 8 - CONV_W), (0, 0))),
        "cb": conv_b.astype(F32).reshape(1, d_xbc),
        "dtb_f": lanes16(dt_bias_f, 0), "dtb_b": lanes16(dt_bias_b, SSD_HEADS),
        "a_f": lanes16(-jnp.exp(a_log_f.astype(F32)), 0),
        "a_b": lanes16(-jnp.exp(a_log_b.astype(F32)), SSD_HEADS),
        "e2_f": e2(0), "e2_b": e2(SSD_HEADS),
        "dsk": jnp.repeat(d_skip.astype(F32), SSD_HEADDIM).reshape(1, d_ssd),
        "ng": ssd_norm_g.astype(F32).reshape(1, d_ssd),
    }
    yf = _ssd(proj, dt_raw, z_col, None, params, bsz, seq, rev=False)
    ynorm = _ssd(proj, dt_raw, z_col, yf, params, bsz, seq, rev=True)

    bias, var_of_blk, ks_of_blk = _na_bias(na_rpb, seq // GRID_W)
    attn = _na(proj, bias, var_of_blk, ks_of_blk, q_col, bsz, seq)

    h, scores = _mix(x2, ynorm, attn, proj, gs_col, g1, sh2, sc2, norm2_g.reshape(1, d),
                     w_ssd_br.astype(BF16), w_na_br.astype(BF16), w_out.astype(BF16),
                     peer_wq.astype(BF16), peer_keys.astype(BF16), seq)

    eidx, gate = _select(scores.reshape(2 * PEER_HEADS, PEER_NKEYS, t // LANES, LANES))
    nj = PEER_HEADS * PEER_TOPK
    eidx_t = eidx.reshape(nj, t).T
    gate_t = gate.reshape(nj, t).T
    return _peer(h, sh2, sc2, g2, norm2_g.reshape(1, d), out_g, eidx_t, gate_t,
                 peer_u.astype(BF16), peer_v.astype(BF16), seq)


def kernel(x, c, w_ada, b_ada, norm1_g, w_in, conv_w, conv_b, dt_bias_f, dt_bias_b, a_log_f, a_log_b,
           d_skip, ssd_norm_g, w_ssd_br, na_rpb, w_na_br, w_out, norm2_g, peer_wq, peer_keys, peer_u,
           peer_v, final_g):
    bsz, seq, d = x.shape
    depth = w_ada.shape[0]
    assert depth == 1, "the final RMSNorm is fused into the last layer's PEER kernel"
    i = 0
    out = _layer(x.reshape(bsz * seq, d), c, bsz, seq, w_ada[i], b_ada[i], norm1_g[i], w_in[i],
                 conv_w[i], conv_b[i], dt_bias_f[i], dt_bias_b[i], a_log_f[i], a_log_b[i], d_skip[i],
                 ssd_norm_g[i], w_ssd_br[i], na_rpb[i], w_na_br[i], w_out[i], norm2_g[i],
                 peer_wq[i], peer_keys[i], peer_u[i], peer_v[i], final_g.reshape(1, d))
    return out.reshape(bsz, seq, d)
```

```python
import functools

import numpy as np
import jax
import jax.numpy as jnp
from jax import lax
from jax.experimental import pallas as pl
from jax.experimental.pallas import tpu as pltpu

F32 = jnp.float32
BF16 = jnp.bfloat16
I32 = jnp.int32

EPS = 1e-6
GRID_W = 64
SSD_HEADS = 16
SSD_HEADDIM = 64
SSD_GROUPS = 4
SSD_STATE = 128
SSD_CHUNK = 128
CONV_W = 5
NA_HEADS = 16
NA_HEADDIM = 64
NA_KH = 8
NA_KW = 16
NA_QROWS = 4
NA_KROWS = 12
PEER_HEADS = 8
PEER_NKEYS = 128
PEER_TOPK = 16
PEER_DHALF = 128
W3_PITCH = PEER_NKEYS // 2 + 8

LANES = 128
HALO = 16
VMEM_LIMIT = 56 * 1024 * 1024


def _cparams(sem):
    return pltpu.CompilerParams(dimension_semantics=sem, vmem_limit_bytes=VMEM_LIMIT)


def _sigmoid(x):
    return 1.0 / (1.0 + jnp.exp(-x))


def _rms_mod(x, g, shift, scale):
    ms = jnp.mean(x * x, axis=-1, keepdims=True)
    y = x * lax.rsqrt(ms + EPS) * g
    return y * (1.0 + scale) + shift


def _ada_kernel(c_ref, w_ref, b_ref, o_ref):
    c = c_ref[...]
    sc = c * _sigmoid(c)
    o_ref[...] = jnp.dot(sc, w_ref[...], preferred_element_type=F32,
                         precision=lax.Precision.HIGHEST) + b_ref[...]


def _ada(c, w, b):
    bsz, d = c.shape
    n = w.shape[1]
    tn = 1024
    return pl.pallas_call(
        _ada_kernel,
        grid=(n // tn,),
        in_specs=[pl.BlockSpec((bsz, d), lambda j: (0, 0)),
                  pl.BlockSpec((d, tn), lambda j: (0, j)),
                  pl.BlockSpec((1, tn), lambda j: (0, j))],
        out_specs=pl.BlockSpec((bsz, tn), lambda j: (0, j)),
        out_shape=jax.ShapeDtypeStruct((bsz, n), F32),
        compiler_params=_cparams(("arbitrary",)),
        name="ada",
    )(c, w, b.reshape(1, n))


def _inproj_kernel(x_ref, sh_ref, sc_ref, g_ref, w_ref, wdt_ref, proj_ref, dt_ref, n1_ref):
    @pl.when(pl.program_id(1) == 0)
    def _():
        n1 = _rms_mod(x_ref[...], g_ref[...], sh_ref[0], sc_ref[0]).astype(BF16)
        n1_ref[...] = n1
        dt_ref[...] = jnp.dot(n1, wdt_ref[...], preferred_element_type=F32)

    proj_ref[...] = jnp.dot(n1_ref[...], w_ref[...], preferred_element_type=F32).astype(BF16)


def _inproj(x2, sh, sc, g, w, wdt, seq):
    t, d = x2.shape
    n = w.shape[1]
    tm = min(1024, seq)
    tn = 1024
    bidx = lambda i, j: ((i * tm) // seq, 0, 0)
    return pl.pallas_call(
        _inproj_kernel,
        grid=(t // tm, n // tn),
        in_specs=[pl.BlockSpec((tm, d), lambda i, j: (i, 0)),
                  pl.BlockSpec((1, 1, d), bidx),
                  pl.BlockSpec((1, 1, d), bidx),
                  pl.BlockSpec((1, d), lambda i, j: (0, 0)),
                  pl.BlockSpec((d, tn), lambda i, j: (0, j)),
                  pl.BlockSpec((d, LANES), lambda i, j: (0, 0))],
        out_specs=[pl.BlockSpec((tm, tn), lambda i, j: (i, j)),
                   pl.BlockSpec((tm, LANES), lambda i, j: (i, 0))],
        out_shape=[jax.ShapeDtypeStruct((t, n), BF16),
                   jax.ShapeDtypeStruct((t, LANES), F32)],
        scratch_shapes=[pltpu.VMEM((tm, d), BF16)],
        compiler_params=_cparams(("arbitrary", "arbitrary")),
        name="inproj",
    )(x2, sh, sc, g, w, wdt)


def _expand_heads(v, e2_ref):
    hi = v.astype(BF16).astype(F32)
    comb = (hi + pltpu.roll(v - hi, 64, 1)).astype(BF16)
    return jnp.dot(comb, e2_ref[...], preferred_element_type=F32)


def _ssd_kernel(*refs, rev):
    if rev:
        (xm_ref, xp_ref, xn_ref, dt_ref, z_ref, yf_ref, cw_ref, cb_ref, dtb_ref, a_ref,
         e2_ref, dsk_ref, ng_ref, out_ref, st_ref) = refs
    else:
        (xm_ref, xp_ref, xn_ref, dt_ref, cw_ref, cb_ref, dtb_ref, a_ref,
         e2_ref, out_ref, st_ref) = refs
    ck = SSD_CHUNK
    d_ssd = SSD_HEADS * SSD_HEADDIM
    gw = d_ssd // SSD_GROUPS
    c = pl.program_id(1)
    nc = pl.num_programs(1)
    ce = nc - 1 - c if rev else c

    @pl.when(c == 0)
    def _():
        st_ref[...] = jnp.zeros_like(st_ref)

    main = xm_ref[...]
    zero_halo = jnp.zeros((HALO, main.shape[1]), main.dtype)
    ext = jnp.concatenate([jnp.where(ce == 0, zero_halo, xp_ref[...]), main,
                           jnp.where(ce == nc - 1, zero_halo, xn_ref[...])], axis=0)
    erow = lax.broadcasted_iota(I32, (ck, ck + 2 * HALO), 0)
    ecol = lax.broadcasted_iota(I32, (ck, ck + 2 * HALO), 1)
    mid = CONV_W // 2
    acc = cb_ref[...] + main.astype(F32) * cw_ref[mid:mid + 1, :]
    for w in range(CONV_W):
        if w != mid:
            shift = jnp.where(ecol == erow + (HALO + w - mid), 1.0, 0.0).astype(BF16)
            acc = acc + jnp.dot(shift, ext, preferred_element_type=F32) * cw_ref[w:w + 1, :]
    xbc = acc * _sigmoid(acc)
    xs = xbc[:, :d_ssd]
    bm = xbc[:, d_ssd:d_ssd + SSD_GROUPS * SSD_STATE]
    cm = xbc[:, d_ssd + SSD_GROUPS * SSD_STATE:]

    lane = lax.broadcasted_iota(I32, (ck, LANES), 1)
    row = lax.broadcasted_iota(I32, (ck, LANES), 0)
    off = SSD_HEADS if rev else 0
    valid = (lane >= off) & (lane < off + SSD_HEADS)
    dtr = dt_ref[...] + dtb_ref[...]
    dt = jnp.where(valid, jnp.maximum(dtr, 0.0) + jnp.log1p(jnp.exp(-jnp.abs(dtr))), 0.0)
    da = dt * a_ref[...]
    tri = (lane >= row) if rev else (lane <= row)
    trib = jnp.where(tri, 1.0, 0.0).astype(BF16)
    hi = da.astype(BF16)
    r1 = da - hi.astype(F32)
    mid = r1.astype(BF16)
    lo = (r1 - mid.astype(F32)).astype(BF16)
    cs = (jnp.dot(trib, hi, preferred_element_type=F32)
          + jnp.dot(trib, mid, preferred_element_type=F32)
          + jnp.dot(trib, lo, preferred_element_type=F32))
    cst = cs.T
    tot_row = 0 if rev else ck - 1
    tot = cs[tot_row:tot_row + 1, :]
    dec = jnp.where(valid, jnp.exp(tot - cs), 0.0)
    ecs = jnp.where(valid, jnp.exp(cs), 0.0)
    ecs_e = _expand_heads(ecs, e2_ref)
    xdt = xs * _expand_heads(dt, e2_ref)
    xdec = (xs * _expand_heads(dt * dec, e2_ref)).astype(BF16)

    lo_half = lax.broadcasted_iota(I32, (ck, LANES), 1) < SSD_HEADDIM
    hpg = SSD_HEADS // SSD_GROUPS
    ys = []
    for g in range(SSD_GROUPS):
        bg = bm[:, g * SSD_STATE:(g + 1) * SSD_STATE]
        cg = cm[:, g * SSD_STATE:(g + 1) * SSD_STATE].astype(BF16)
        cbg = lax.dot_general(cg, bg.astype(BF16), (((1,), (1,)), ((), ())),
                              preferred_element_type=F32)
        yg = []
        for q in range(hpg // 2):
            ls = []
            for e in range(2):
                col = off + g * hpg + q * 2 + e
                dlt = cs[:, col:col + 1] - cst[col:col + 1, :]
                seg = jnp.exp(jnp.where(tri, dlt, -jnp.inf))
                ls.append((cbg * seg).astype(BF16))
            blk = g * (hpg // 2) + q
            xq = xdt[:, blk * LANES:(blk + 1) * LANES]
            rhs = jnp.concatenate([jnp.where(lo_half, xq, 0.0), jnp.where(lo_half, 0.0, xq)],
                                  axis=0).astype(BF16)
            yg.append(jnp.dot(jnp.concatenate(ls, axis=1), rhs, preferred_element_type=F32))
        y_diag = jnp.concatenate(yg, axis=1)
        st = st_ref[g]
        e_g = ecs_e[:, g * gw:(g + 1) * gw]
        y_off = jnp.dot(cg, st.astype(BF16), preferred_element_type=F32) * e_g
        ys.append(y_diag + y_off)
        upd = jnp.dot(bg.T.astype(BF16), xdec[:, g * gw:(g + 1) * gw], preferred_element_type=F32)
        st_ref[g] = st * e_g[tot_row:tot_row + 1, :] + upd
    y = jnp.concatenate(ys, axis=1)

    if rev:
        y = yf_ref[...] + y + dsk_ref[...] * xs
        z = z_ref[...].astype(F32)
        u = y * (z * _sigmoid(z))
        ms = jnp.mean(u * u, axis=-1, keepdims=True)
        out_ref[...] = (u * lax.rsqrt(ms + EPS) * ng_ref[...]).astype(BF16)
    else:
        out_ref[...] = y


def _ssd(proj, dt_raw, z_col, yf, params, bsz, seq, rev):
    t = proj.shape[0]
    ck = SSD_CHUNK
    nc = seq // ck
    d_xbc = params["cw"].shape[1]
    d_ssd = SSD_HEADS * SSD_HEADDIM
    per = ck // HALO
    last_halo = t // HALO - 1

    def ce_of(c):
        return nc - 1 - c if rev else c

    main = lambda b, c: (b * nc + ce_of(c), 0)
    prev = lambda b, c: (jnp.maximum((b * nc + ce_of(c)) * per - 1, 0), 0)
    nxt = lambda b, c: (jnp.minimum((b * nc + ce_of(c)) * per + per, last_halo), 0)
    const2 = lambda b, c: (0, 0)
    in_specs = [pl.BlockSpec((ck, d_xbc), main),
                pl.BlockSpec((HALO, d_xbc), prev),
                pl.BlockSpec((HALO, d_xbc), nxt),
                pl.BlockSpec((ck, LANES), main)]
    args = [proj, proj, proj, dt_raw]
    if rev:
        in_specs += [pl.BlockSpec((ck, d_ssd), lambda b, c: (b * nc + ce_of(c), z_col)),
                     pl.BlockSpec((ck, d_ssd), main)]
        args += [proj, yf]
    in_specs += [pl.BlockSpec((8, d_xbc), const2), pl.BlockSpec((1, d_xbc), const2),
                 pl.BlockSpec((1, LANES), const2), pl.BlockSpec((1, LANES), const2),
                 pl.BlockSpec((LANES, d_ssd), const2)]
    d = "b" if rev else "f"
    args += [params["cw"], params["cb"], params["dtb_" + d], params["a_" + d], params["e2_" + d]]
    if rev:
        in_specs += [pl.BlockSpec((1, d_ssd), const2), pl.BlockSpec((1, d_ssd), const2)]
        args += [params["dsk"], params["ng"]]
    return pl.pallas_call(
        functools.partial(_ssd_kernel, rev=rev),
        grid=(bsz, nc),
        in_specs=in_specs,
        out_specs=pl.BlockSpec((ck, d_ssd), main),
        out_shape=jax.ShapeDtypeStruct((t, d_ssd), BF16 if rev else F32),
        scratch_shapes=[pltpu.VMEM((SSD_GROUPS, SSD_STATE, d_ssd // SSD_GROUPS), F32)],
        compiler_params=_cparams(("arbitrary", "arbitrary")),
        name="ssd_bwd" if rev else "ssd_fwd",
    )(*args)


def _na_plan(rows):
    nblk = rows // NA_QROWS
    variants, var_of_blk, ks_of_blk = [], [], []
    for i in range(nblk):
        ks = int(np.clip(i * NA_QROWS - NA_KH // 2, 0, rows - NA_KROWS))
        drow = np.zeros((NA_QROWS, NA_KROWS), np.int64)
        ok = np.zeros((NA_QROWS, NA_KROWS), bool)
        for dr in range(NA_QROWS):
            r = i * NA_QROWS + dr
            rs = int(np.clip(r - NA_KH // 2, 0, rows - NA_KH))
            for j in range(NA_KH):
                w = rs + j - ks
                drow[dr, w] = rs + j - r + NA_KH - 1
                ok[dr, w] = True
        key = (drow.tobytes(), ok.tobytes())
        for vi, (k2, _, _) in enumerate(variants):
            if k2 == key:
                var_of_blk.append(vi)
                break
        else:
            var_of_blk.append(len(variants))
            variants.append((key, drow, ok))
        ks_of_blk.append(ks)
    return [(d, o) for _, d, o in variants], var_of_blk, ks_of_blk


def _na_bias(rpb, rows):
    variants, var_of_blk, ks_of_blk = _na_plan(rows)
    cols = np.arange(GRID_W)
    win_start = np.clip(cols - NA_KW // 2, 0, GRID_W - NA_KW)
    in_win = (cols[None, :] >= win_start[:, None]) & (cols[None, :] < win_start[:, None] + NA_KW)
    dcol = np.clip(cols[None, :] - cols[:, None] + NA_KW - 1, 0, 2 * NA_KW - 2)
    onehot = (dcol[:, :, None] == np.arange(2 * NA_KW - 1)).astype(np.float32)
    band = jnp.einsum("hab,qkb->haqk", rpb.astype(F32), onehot, precision=lax.Precision.HIGHEST)
    band = jnp.where(in_win[None, None], band, -jnp.inf)
    neg = jnp.full((rpb.shape[0], GRID_W, GRID_W), -jnp.inf, F32)
    tabs = []
    for drow, ok in variants:
        qrows = [jnp.concatenate([band[:, drow[dr, w]] if ok[dr, w] else neg
                                  for w in range(NA_KROWS)], axis=2) for dr in range(NA_QROWS)]
        tabs.append(jnp.concatenate(qrows, axis=1))
    return jnp.stack(tabs), var_of_blk, ks_of_blk


def _na_kernel(var_ref, ks_ref, q_ref, k_ref, v_ref, bias_ref, o_ref):
    mq = NA_QROWS * GRID_W
    nk = NA_KROWS * GRID_W
    nblk = q_ref.shape[0] // mq
    lo_half = lax.broadcasted_iota(I32, (mq, LANES), 1) < NA_HEADDIM

    def blk(i, carry):
        q0 = pl.multiple_of(i * mq, mq)
        k0 = pl.multiple_of(ks_ref[i] * GRID_W, GRID_W)
        qb = q_ref[pl.ds(q0, mq), :] * (NA_HEADDIM ** -0.5)
        zero = jnp.zeros_like(qb)
        q2 = jnp.concatenate([jnp.where(lo_half, qb, zero), jnp.where(lo_half, zero, qb)], axis=0)
        kb = k_ref[pl.ds(k0, nk), :]
        vb = v_ref[pl.ds(k0, nk), :]
        s = lax.dot_general(q2, kb, (((1,), (1,)), ((), ())), preferred_element_type=F32)
        s = s + bias_ref[var_ref[i]].reshape(2 * mq, nk)
        m = jnp.max(s, axis=-1, keepdims=True)
        p = jnp.exp(s - m)
        l = jnp.sum(p, axis=-1, keepdims=True)
        o2 = jnp.dot(p.astype(BF16), vb, preferred_element_type=F32) / l
        o_ref[pl.ds(q0, mq), :] = jnp.where(lo_half, o2[:mq], o2[mq:]).astype(BF16)
        return carry

    lax.fori_loop(0, nblk, blk, 0, unroll=2)


def _na(proj, bias, var_of_blk, ks_of_blk, q_col, bsz, seq):
    t = proj.shape[0]
    npair = NA_HEADS // 2
    nv = bias.shape[0]
    mq = NA_QROWS * GRID_W
    nk = NA_KROWS * GRID_W
    grid_spec = pltpu.PrefetchScalarGridSpec(
        num_scalar_prefetch=2,
        grid=(bsz, npair),
        in_specs=[pl.BlockSpec((seq, LANES), lambda b, p, *_: (b, q_col + p)),
                  pl.BlockSpec((seq, LANES), lambda b, p, *_: (b, q_col + npair + p)),
                  pl.BlockSpec((seq, LANES), lambda b, p, *_: (b, q_col + 2 * npair + p)),
                  pl.BlockSpec((nv, 2, mq, nk), lambda b, p, *_: (0, p, 0, 0))],
        out_specs=pl.BlockSpec((seq, LANES), lambda b, p, *_: (b, p)),
    )
    return pl.pallas_call(
        _na_kernel,
        grid_spec=grid_spec,
        out_shape=jax.ShapeDtypeStruct((t, NA_HEADS * NA_HEADDIM), BF16),
        compiler_params=_cparams(("arbitrary", "arbitrary")),
        name="na",
    )(jnp.asarray(var_of_blk, I32), jnp.asarray(ks_of_blk, I32), proj, proj, proj, bias)


def _mix_kernel(x_ref, yn_ref, at_ref, gs_ref, gn_ref, g1_ref, sh_ref, sc_ref, ng_ref,
                ws_ref, wn_ref, wo_ref, wq_ref, keys_ref, h_ref, s_ref):
    y1 = jnp.dot(yn_ref[...], ws_ref[...], preferred_element_type=F32)
    y2 = jnp.dot(at_ref[...], wn_ref[...], preferred_element_type=F32)
    mixed = _sigmoid(gs_ref[...].astype(F32)) * y1 + _sigmoid(gn_ref[...].astype(F32)) * y2
    h = x_ref[...] + g1_ref[0] * jnp.dot(mixed.astype(BF16), wo_ref[...], preferred_element_type=F32)
    h_ref[...] = h
    n2 = _rms_mod(h, ng_ref[...], sh_ref[0], sc_ref[0]).astype(BF16)
    q = jnp.dot(n2, wq_ref[...], preferred_element_type=F32).astype(BF16)
    for hz in range(2 * PEER_HEADS):
        qs = q[:, hz * PEER_DHALF:(hz + 1) * PEER_DHALF]
        s_ref[hz] = lax.dot_general(keys_ref[hz % 2], qs, (((1,), (1,)), ((), ())),
                                    preferred_element_type=F32)


def _mix(x2, ynorm, attn, proj, gs_col, g1, sh2, sc2, ng2, ws, wn, wo, wq, keys, seq):
    t, d = x2.shape
    tm = min(256, seq)
    bidx = lambda i: ((i * tm) // seq, 0, 0)
    row = lambda i: (i, 0)
    const2 = lambda i: (0, 0)
    return pl.pallas_call(
        _mix_kernel,
        grid=(t // tm,),
        in_specs=[pl.BlockSpec((tm, d), row), pl.BlockSpec((tm, d), row), pl.BlockSpec((tm, d), row),
                  pl.BlockSpec((tm, d), lambda i: (i, gs_col)),
                  pl.BlockSpec((tm, d), lambda i: (i, gs_col + 1)),
                  pl.BlockSpec((1, 1, d), bidx), pl.BlockSpec((1, 1, d), bidx),
                  pl.BlockSpec((1, 1, d), bidx), pl.BlockSpec((1, d), const2),
                  pl.BlockSpec((d, d), const2), pl.BlockSpec((d, d), const2),
                  pl.BlockSpec((d, d), const2), pl.BlockSpec(wq.shape, const2),
                  pl.BlockSpec(keys.shape, lambda i: (0, 0, 0))],
        out_specs=[pl.BlockSpec((tm, d), row),
                   pl.BlockSpec((2 * PEER_HEADS, PEER_NKEYS, tm), lambda i: (0, 0, i))],
        out_shape=[jax.ShapeDtypeStruct((t, d), F32),
                   jax.ShapeDtypeStruct((2 * PEER_HEADS, PEER_NKEYS, t), F32)],
        compiler_params=_cparams(("arbitrary",)),
        name="mix",
    )(x2, ynorm, attn, proj, proj, g1, sh2, sc2, ng2, ws, wn, wo, wq, keys)


def _hyperbola():
    return [(i, k) for i in range(PEER_TOPK) for k in range(PEER_TOPK)
            if (i + 1) * (k + 1) <= PEER_TOPK]


def _sort_network(n):
    pairs = []
    p = 1
    while p < n:
        k = p
        while k >= 1:
            for j in range(k % p, n - k, 2 * k):
                for i in range(min(k, n - j - k)):
                    if (i + j) // (2 * p) == (i + j + k) // (2 * p):
                        pairs.append((i + j, i + j + k))
            k //= 2
        p *= 2
    return pairs


def _precedes(va, pa, vb, pb):
    return (va > vb) | ((va == vb) & (pa < pb))


def _compare_exchange(v, p, i, j):
    c = _precedes(v[i], p[i], v[j], p[j])
    v[i], v[j] = jnp.where(c, v[i], v[j]), jnp.where(c, v[j], v[i])
    p[i], p[j] = jnp.where(c, p[i], p[j]), jnp.where(c, p[j], p[i])


def _top16(get, n, sv_ref, sp_ref):
    k = PEER_TOPK
    ng = n // k
    net = _sort_network(k)
    for g in range(ng):
        items = [get(g * k + j) for j in range(k)]
        v = [it[0] for it in items]
        p = [it[1] for it in items]
        for i, j in net:
            _compare_exchange(v, p, i, j)
        if ng == 1:
            return v, p
        for j in range(k):
            sv_ref[g, j] = v[j]
            sp_ref[g, j] = p[j]
    step = 1
    while True:
        for g in range(0, ng, 2 * step):
            v, p = [], []
            for j in range(k):
                xv, xp = sv_ref[g, j], sp_ref[g, j]
                yv, yp = sv_ref[g + step, k - 1 - j], sp_ref[g + step, k - 1 - j]
                c = _precedes(xv, xp, yv, yp)
                v.append(jnp.where(c, xv, yv))
                p.append(jnp.where(c, xp, yp))
            stride = k // 2
            while stride >= 1:
                for i in range(k):
                    if i & stride == 0:
                        _compare_exchange(v, p, i, i + stride)
                stride //= 2
            if 2 * step >= ng:
                return v, p
            for j in range(k):
                sv_ref[g, j] = v[j]
                sp_ref[g, j] = p[j]
        step *= 2


def _select_kernel(s_ref, eidx_ref, gate_ref, sv_ref, sp_ref):
    nk = PEER_NKEYS
    k = PEER_TOPK
    shape = s_ref.shape[2:]
    ebits = (nk * nk - 1).bit_length()

    tops = []
    for z in range(2):
        tops.append(_top16(lambda j, z=z: (s_ref[z, j], jnp.full(shape, j, I32)), nk, sv_ref, sp_ref))
    (tv0, ti0), (tv1, ti1) = tops

    pairs = _hyperbola()
    npad = -len(pairs) % k

    def cand(j):
        if j >= len(pairs):
            return jnp.full(shape, -jnp.inf, F32), jnp.full(shape, (k * k) << ebits, I32)
        i, kk = pairs[j]
        return tv0[i] + tv1[kk], ((i * k + kk) << ebits) + ti0[i] * nk + ti1[kk]

    top, ids = _top16(cand, len(pairs) + npad, sv_ref, sp_ref)
    ex = [jnp.exp(t - top[0]) for t in top]
    zsum = ex[0]
    for r in range(1, k):
        zsum = zsum + ex[r]
    for r in range(k):
        eidx_ref[0, r] = ids[r] & ((1 << ebits) - 1)
        gate_ref[0, r] = ex[r] / zsum


def _select(s4):
    nhz, nk, tb, _ = s4.shape
    r = min(8, tb)
    oshape = (PEER_HEADS, PEER_TOPK, tb, LANES)
    return pl.pallas_call(
        _select_kernel,
        grid=(tb // r, PEER_HEADS),
        in_specs=[pl.BlockSpec((2, nk, r, LANES), lambda i, h: (h, 0, i, 0))],
        out_specs=[pl.BlockSpec((1, PEER_TOPK, r, LANES), lambda i, h: (h, 0, i, 0)),
                   pl.BlockSpec((1, PEER_TOPK, r, LANES), lambda i, h: (h, 0, i, 0))],
        out_shape=[jax.ShapeDtypeStruct(oshape, I32), jax.ShapeDtypeStruct(oshape, F32)],
        scratch_shapes=[pltpu.VMEM((nk // PEER_TOPK, PEER_TOPK, r, LANES), F32),
                        pltpu.VMEM((nk // PEER_TOPK, PEER_TOPK, r, LANES), I32)],
        compiler_params=_cparams(("arbitrary", "arbitrary")),
        name="select",
    )(s4)


def _peer_kernel(h_ref, sh_ref, sc_ref, g2_ref, ng_ref, fg_ref, eidx_ref, gate_ref, u_ref, v0_ref,
                 v1_ref, out_ref, n2_ref, act_ref, wv_ref, w3_ref, acc_ref, *, ns):
    nk = PEER_NKEYS
    half = nk // 2
    hi_mask = jnp.uint32(0xFFFF0000)
    tm = h_ref.shape[0]
    nblk = u_ref.shape[0] // nk
    nblk2 = v0_ref.shape[0] // nk
    s = pl.program_id(1)

    @pl.when(s == 0)
    def _():
        n2_ref[...] = _rms_mod(h_ref[...], ng_ref[...], sh_ref[0], sc_ref[0]).astype(BF16)
        act_ref[...] = jnp.zeros_like(act_ref)

    @pl.when(s < ns)
    def _():
        e = eidx_ref[...]
        ai = jnp.right_shift(e, nk.bit_length() - 1)
        bi = jnp.bitwise_and(e, nk - 1)
        act = act_ref[...]
        n2 = n2_ref[...]
        for a in range(0, nblk, 2):
            sc = lax.dot_general(n2, u_ref[a * nk:(a + 2) * nk, :], (((1,), (1,)), ((), ())),
                                 preferred_element_type=F32)
            for j in range(2):
                got = jnp.take_along_axis(sc[:, j * nk:(j + 1) * nk], bi, axis=1)
                act = jnp.where(ai == s * nblk + a + j, got, act)
        act_ref[...] = act

    @pl.when(s == ns - 1)
    def _():
        act = act_ref[...]
        gelu = 0.5 * act * (1.0 + lax.erf(act * (2.0 ** -0.5)))
        wv_ref[...] = gate_ref[...] * gelu
        sub = lax.broadcasted_iota(I32, (nk, LANES), 0)

        def tok(t, carry):
            e = eidx_ref[pl.ds(t, 1), :]
            w = wv_ref[pl.ds(t, 1), :]
            ar = jnp.right_shift(e, nk.bit_length() - 1)
            br = jnp.bitwise_and(e, nk - 1)
            pt = jnp.where(sub == ar, w, 0.0).astype(BF16)
            qt = jnp.where(sub == br, 1.0, 0.0).astype(BF16)
            wt = lax.dot_general(pt, qt, (((1,), (1,)), ((), ())), preferred_element_type=F32)
            bits = lax.bitcast_convert_type(wt.astype(BF16).astype(F32), jnp.uint32)
            word = jnp.right_shift(bits[:half], 16) | (bits[half:] & hi_mask)
            w3_ref[pl.ds(pl.multiple_of(t * W3_PITCH, 8), half), :] = word
            return carry
        lax.fori_loop(0, tm, tok, 0, unroll=16)

    @pl.when(s >= ns)
    def _():
        base = (s - ns) * nblk2
        words = [w3_ref[pl.ds(base + a, tm, stride=W3_PITCH), :] for a in range(nblk2)]
        lo = [lax.bitcast_convert_type(jnp.left_shift(w, 16), F32).astype(BF16) for w in words]
        hi = [lax.bitcast_convert_type(w & hi_mask, F32).astype(BF16) for w in words]
        part = (jnp.dot(jnp.concatenate(lo, axis=1), v0_ref[...], preferred_element_type=F32)
                + jnp.dot(jnp.concatenate(hi, axis=1), v1_ref[...], preferred_element_type=F32))

        @pl.when(s == ns)
        def _():
            acc_ref[...] = part

        @pl.when(s > ns)
        def _():
            acc_ref[...] += part

    @pl.when(s == 2 * ns - 1)
    def _():
        hh = h_ref[...] + g2_ref[0] * acc_ref[...]
        ms = jnp.mean(hh * hh, axis=-1, keepdims=True)
        out_ref[...] = hh * lax.rsqrt(ms + EPS) * fg_ref[...]


def _peer(h, sh2, sc2, g2, ng2, fg, eidx_t, gate_t, u, v, seq):
    t, d = h.shape
    ne = u.shape[0]
    tm = min(512, seq)
    eb = 2048
    ns = ne // eb
    nj = eidx_t.shape[1]
    bidx = lambda i, s: ((i * tm) // seq, 0, 0)
    row = lambda i, s: (i, 0)
    const2 = lambda i, s: (0, 0)
    return pl.pallas_call(
        functools.partial(_peer_kernel, ns=ns),
        grid=(t // tm, 2 * ns),
        in_specs=[pl.BlockSpec((tm, d), row),
                  pl.BlockSpec((1, 1, d), bidx), pl.BlockSpec((1, 1, d), bidx),
                  pl.BlockSpec((1, 1, d), bidx),
                  pl.BlockSpec((1, d), const2), pl.BlockSpec((1, d), const2),
                  pl.BlockSpec((tm, nj), row), pl.BlockSpec((tm, nj), row),
                  pl.BlockSpec((eb, d), lambda i, s: (jnp.minimum(s, ns - 1), 0)),
                  pl.BlockSpec((eb // 2, d), lambda i, s: (jnp.maximum(s - ns, 0), 0)),
                  pl.BlockSpec((eb // 2, d), lambda i, s: (ns + jnp.maximum(s - ns, 0), 0))],
        out_specs=pl.BlockSpec((tm, d), row),
        out_shape=jax.ShapeDtypeStruct((t, d), F32),
        scratch_shapes=[pltpu.VMEM((tm, d), BF16),
                        pltpu.VMEM((tm, nj), F32),
                        pltpu.VMEM((tm, nj), F32),
                        pltpu.VMEM((tm * W3_PITCH, PEER_NKEYS), jnp.uint32),
                        pltpu.VMEM((tm, d), F32)],
        compiler_params=_cparams(("arbitrary", "arbitrary")),
        name="peer",
    )(h, sh2, sc2, g2, ng2, fg, eidx_t, gate_t, u, v, v)


def _layer(x2, c, bsz, seq, w_ada, b_ada, norm1_g, w_in, conv_w, conv_b, dt_bias_f, dt_bias_b,
           a_log_f, a_log_b, d_skip, ssd_norm_g, w_ssd_br, na_rpb, w_na_br, w_out, norm2_g,
           peer_wq, peer_keys, peer_u, peer_v, out_g):
    t, d = x2.shape
    d_ssd = SSD_HEADS * SSD_HEADDIM
    d_xbc = d_ssd + 2 * SSD_GROUPS * SSD_STATE
    d_na = NA_HEADS * NA_HEADDIM
    assert seq % SSD_CHUNK == 0 and seq % (GRID_W * NA_QROWS) == 0
    assert seq // GRID_W >= NA_KROWS and d == d_ssd == d_na

    mod = _ada(c, w_ada, b_ada)
    sh1, sc1, g1, sh2, sc2, g2 = [m.reshape(bsz, 1, d) for m in jnp.split(mod, 6, axis=-1)]

    o = np.cumsum([0, d_ssd, d_xbc, SSD_HEADS, SSD_HEADS, 3 * d_na, d, d])
    w_main = jnp.concatenate([w_in[:, o[1]:o[2]], w_in[:, o[0]:o[1]], w_in[:, o[4]:o[7]]],
                             axis=1).astype(BF16)
    w_dt = jnp.pad(w_in[:, o[2]:o[4]], ((0, 0), (0, LANES - 2 * SSD_HEADS))).astype(BF16)
    z_col = d_xbc // d_ssd
    q_col = (d_xbc + d_ssd) // LANES
    gs_col = (d_xbc + d_ssd + 3 * d_na) // d
    proj, dt_raw = _inproj(x2, sh1, sc1, norm1_g.reshape(1, d), w_main, w_dt, seq)

    heads = np.arange(d_ssd) // SSD_HEADDIM

    def e2(off):
        m = np.zeros((LANES, d_ssd), np.float32)
        m[off + heads, np.arange(d_ssd)] = 1.0
        m[off + 64 + heads, np.arange(d_ssd)] = 1.0
        return jnp.asarray(m, BF16)

    def lanes16(vec, off):
        return jnp.zeros((1, LANES), F32).at[0, off:off + SSD_HEADS].set(vec.astype(F32))

    params = {
        "cw": jnp.pad(conv_w.astype(F32), ((0, 8 - CONV_W), (0, 0))),
        "cb": conv_b.astype(F32).reshape(1, d_xbc),
        "dtb_f": lanes16(dt_bias_f, 0), "dtb_b": lanes16(dt_bias_b, SSD_HEADS),
        "a_f": lanes16(-jnp.exp(a_log_f.astype(F32)), 0),
        "a_b": lanes16(-jnp.exp(a_log_b.astype(F32)), SSD_HEADS),
        "e2_f": e2(0), "e2_b": e2(SSD_HEADS),
        "dsk": jnp.repeat(d_skip.astype(F32), SSD_HEADDIM).reshape(1, d_ssd),
        "ng": ssd_norm_g.astype(F32).reshape(1, d_ssd),
    }
    yf = _ssd(proj, dt_raw, z_col, None, params, bsz, seq, rev=False)
    ynorm = _ssd(proj, dt_raw, z_col, yf, params, bsz, seq, rev=True)

    bias, var_of_blk, ks_of_blk = _na_bias(na_rpb, seq // GRID_W)
    attn = _na(proj, bias, var_of_blk, ks_of_blk, q_col, bsz, seq)

    h, scores = _mix(x2, ynorm, attn, proj, gs_col, g1, sh2, sc2, norm2_g.reshape(1, d),
                     w_ssd_br.astype(BF16), w_na_br.astype(BF16), w_out.astype(BF16),
                     peer_wq.astype(BF16), peer_keys.astype(BF16), seq)

    eidx, gate = _select(scores.reshape(2 * PEER_HEADS, PEER_NKEYS, t // LANES, LANES))
    nj = PEER_HEADS * PEER_TOPK
    eidx_t = eidx.reshape(nj, t).T
    gate_t = gate.reshape(nj, t).T
    return _peer(h, sh2, sc2, g2, norm2_g.reshape(1, d), out_g, eidx_t, gate_t,
                 peer_u.astype(BF16), peer_v.astype(BF16), seq)


def kernel(x, c, w_ada, b_ada, norm1_g, w_in, conv_w, conv_b, dt_bias_f, dt_bias_b, a_log_f, a_log_b,
           d_skip, ssd_norm_g, w_ssd_br, na_rpb, w_na_br, w_out, norm2_g, peer_wq, peer_keys, peer_u,
           peer_v, final_g):
    bsz, seq, d = x.shape
    depth = w_ada.shape[0]
    assert depth == 1, "the final RMSNorm is fused into the last layer's PEER kernel"
    i = 0
    out = _layer(x.reshape(bsz * seq, d), c, bsz, seq, w_ada[i], b_ada[i], norm1_g[i], w_in[i],
                 conv_w[i], conv_b[i], dt_bias_f[i], dt_bias_b[i], a_log_f[i], a_log_b[i], d_skip[i],
                 ssd_norm_g[i], w_ssd_br[i], na_rpb[i], w_na_br[i], w_out[i], norm2_g[i],
                 peer_wq[i], peer_keys[i], peer_u[i], peer_v[i], final_g.reshape(1, d))
    return out.reshape(bsz, seq, d)
```

```python
import functools

import numpy as np
import jax
import jax.numpy as jnp
from jax import lax
from jax.experimental import pallas as pl
from jax.experimental.pallas import tpu as pltpu

F32 = jnp.float32
BF16 = jnp.bfloat16
I32 = jnp.int32

EPS = 1e-6
GRID_W = 64
SSD_HEADS = 16
SSD_HEADDIM = 64
SSD_GROUPS = 4
SSD_STATE = 128
SSD_CHUNK = 128
CONV_W = 5
NA_HEADS = 16
NA_HEADDIM = 64
NA_KH = 8
NA_KW = 16
NA_QROWS = 4
NA_KROWS = 12
PEER_HEADS = 8
PEER_NKEYS = 128
PEER_TOPK = 16
PEER_DHALF = 128
W3_PITCH = PEER_NKEYS // 2 + 8

LANES = 128
HALO = 16
VMEM_LIMIT = 56 * 1024 * 1024


def _cparams(sem):
    return pltpu.CompilerParams(dimension_semantics=sem, vmem_limit_bytes=VMEM_LIMIT)


def _sigmoid(x):
    return 1.0 / (1.0 + jnp.exp(-x))


def _rms_mod(x, g, shift, scale):
    ms = jnp.mean(x * x, axis=-1, keepdims=True)
    y = x * lax.rsqrt(ms + EPS) * g
    return y * (1.0 + scale) + shift


def _ada_kernel(c_ref, w_ref, b_ref, o_ref):
    c = c_ref[...]
    sc = c * _sigmoid(c)
    o_ref[...] = jnp.dot(sc, w_ref[...], preferred_element_type=F32,
                         precision=lax.Precision.HIGHEST) + b_ref[...]


def _ada(c, w, b):
    bsz, d = c.shape
    n = w.shape[1]
    tn = 1024
    return pl.pallas_call(
        _ada_kernel,
        grid=(n // tn,),
        in_specs=[pl.BlockSpec((bsz, d), lambda j: (0, 0)),
                  pl.BlockSpec((d, tn), lambda j: (0, j)),
                  pl.BlockSpec((1, tn), lambda j: (0, j))],
        out_specs=pl.BlockSpec((bsz, tn), lambda j: (0, j)),
        out_shape=jax.ShapeDtypeStruct((bsz, n), F32),
        compiler_params=_cparams(("arbitrary",)),
        name="ada",
    )(c, w, b.reshape(1, n))


def _inproj_kernel(x_ref, sh_ref, sc_ref, g_ref, w_ref, wdt_ref, proj_ref, dt_ref):
    n1 = _rms_mod(x_ref[...], g_ref[...], sh_ref[0], sc_ref[0]).astype(BF16)
    dt_ref[...] = jnp.dot(n1, wdt_ref[...], preferred_element_type=F32)
    tn = 1024
    for j in range(w_ref.shape[1] // tn):
        proj_ref[:, j * tn:(j + 1) * tn] = jnp.dot(
            n1, w_ref[:, j * tn:(j + 1) * tn], preferred_element_type=F32).astype(BF16)


def _inproj(x2, sh, sc, g, w, wdt, seq):
    t, d = x2.shape
    n = w.shape[1]
    tm = min(512, seq)
    bidx = lambda i: ((i * tm) // seq, 0, 0)
    const2 = lambda i: (0, 0)
    return pl.pallas_call(
        _inproj_kernel,
        grid=(t // tm,),
        in_specs=[pl.BlockSpec((tm, d), lambda i: (i, 0)),
                  pl.BlockSpec((1, 1, d), bidx),
                  pl.BlockSpec((1, 1, d), bidx),
                  pl.BlockSpec((1, d), const2),
                  pl.BlockSpec((d, n), const2, pipeline_mode=pl.Buffered(1)),
                  pl.BlockSpec((d, LANES), const2, pipeline_mode=pl.Buffered(1))],
        out_specs=[pl.BlockSpec((tm, n), lambda i: (i, 0)),
                   pl.BlockSpec((tm, LANES), lambda i: (i, 0))],
        out_shape=[jax.ShapeDtypeStruct((t, n), BF16),
                   jax.ShapeDtypeStruct((t, LANES), F32)],
        compiler_params=_cparams(("arbitrary",)),
        name="inproj",
    )(x2, sh, sc, g, w, wdt)


def _expand_heads(v, e2_ref):
    hi = v.astype(BF16).astype(F32)
    comb = (hi + pltpu.roll(v - hi, 64, 1)).astype(BF16)
    return jnp.dot(comb, e2_ref[...], preferred_element_type=F32)


def _ssd_kernel(*refs, rev):
    if rev:
        (xm_ref, xp_ref, xn_ref, dt_ref, z_ref, yf_ref, cw_ref, cb_ref, dtb_ref, a_ref,
         e2_ref, dsk_ref, ng_ref, out_ref, st_ref) = refs
    else:
        (xm_ref, xp_ref, xn_ref, dt_ref, cw_ref, cb_ref, dtb_ref, a_ref,
         e2_ref, out_ref, st_ref) = refs
    ck = SSD_CHUNK
    d_ssd = SSD_HEADS * SSD_HEADDIM
    gw = d_ssd // SSD_GROUPS
    c = pl.program_id(1)
    nc = pl.num_programs(1)
    ce = nc - 1 - c if rev else c

    @pl.when(c == 0)
    def _():
        st_ref[...] = jnp.zeros_like(st_ref)

    main = xm_ref[...]
    zero_halo = jnp.zeros((HALO, main.shape[1]), main.dtype)
    ext = jnp.concatenate([jnp.where(ce == 0, zero_halo, xp_ref[...]), main,
                           jnp.where(ce == nc - 1, zero_halo, xn_ref[...])], axis=0)
    erow = lax.broadcasted_iota(I32, (ck, ck + 2 * HALO), 0)
    ecol = lax.broadcasted_iota(I32, (ck, ck + 2 * HALO), 1)
    mid = CONV_W // 2
    acc = cb_ref[...] + main.astype(F32) * cw_ref[mid:mid + 1, :]
    for w in range(CONV_W):
        if w != mid:
            shift = jnp.where(ecol == erow + (HALO + w - mid), 1.0, 0.0).astype(BF16)
            acc = acc + jnp.dot(shift, ext, preferred_element_type=F32) * cw_ref[w:w + 1, :]
    xbc = acc * _sigmoid(acc)
    xs = xbc[:, :d_ssd]
    bm = xbc[:, d_ssd:d_ssd + SSD_GROUPS * SSD_STATE]
    cm = xbc[:, d_ssd + SSD_GROUPS * SSD_STATE:]

    lane = lax.broadcasted_iota(I32, (ck, LANES), 1)
    row = lax.broadcasted_iota(I32, (ck, LANES), 0)
    off = SSD_HEADS if rev else 0
    valid = (lane >= off) & (lane < off + SSD_HEADS)
    dtr = dt_ref[...] + dtb_ref[...]
    dt = jnp.where(valid, jnp.maximum(dtr, 0.0) + jnp.log1p(jnp.exp(-jnp.abs(dtr))), 0.0)
    da = dt * a_ref[...]
    tri = (lane >= row) if rev else (lane <= row)
    trib = jnp.where(tri, 1.0, 0.0).astype(BF16)
    hi = da.astype(BF16)
    r1 = da - hi.astype(F32)
    mid = r1.astype(BF16)
    lo = (r1 - mid.astype(F32)).astype(BF16)
    cs = (jnp.dot(trib, hi, preferred_element_type=F32)
          + jnp.dot(trib, mid, preferred_element_type=F32)
          + jnp.dot(trib, lo, preferred_element_type=F32))
    cst = cs.T
    tot_row = 0 if rev else ck - 1
    tot = cs[tot_row:tot_row + 1, :]
    dec = jnp.where(valid, jnp.exp(tot - cs), 0.0)
    ecs = jnp.where(valid, jnp.exp(cs), 0.0)
    ecs_e = _expand_heads(ecs, e2_ref)
    xdt = xs * _expand_heads(dt, e2_ref)
    xdec = (xs * _expand_heads(dt * dec, e2_ref)).astype(BF16)

    lo_half = lax.broadcasted_iota(I32, (ck, LANES), 1) < SSD_HEADDIM
    hpg = SSD_HEADS // SSD_GROUPS
    ys = []
    for g in range(SSD_GROUPS):
        bg = bm[:, g * SSD_STATE:(g + 1) * SSD_STATE]
        cg = cm[:, g * SSD_STATE:(g + 1) * SSD_STATE].astype(BF16)
        cbg = lax.dot_general(cg, bg.astype(BF16), (((1,), (1,)), ((), ())),
                              preferred_element_type=F32)
        yg = []
        for q in range(hpg // 2):
            ls = []
            for e in range(2):
                col = off + g * hpg + q * 2 + e
                dlt = cs[:, col:col + 1] - cst[col:col + 1, :]
                seg = jnp.exp(jnp.where(tri, dlt, -jnp.inf))
                ls.append((cbg * seg).astype(BF16))
            blk = g * (hpg // 2) + q
            xq = xdt[:, blk * LANES:(blk + 1) * LANES]
            rhs = jnp.concatenate([jnp.where(lo_half, xq, 0.0), jnp.where(lo_half, 0.0, xq)],
                                  axis=0).astype(BF16)
            yg.append(jnp.dot(jnp.concatenate(ls, axis=1), rhs, preferred_element_type=F32))
        y_diag = jnp.concatenate(yg, axis=1)
        st = st_ref[g]
        e_g = ecs_e[:, g * gw:(g + 1) * gw]
        y_off = jnp.dot(cg, st.astype(BF16), preferred_element_type=F32) * e_g
        ys.append(y_diag + y_off)
        upd = jnp.dot(bg.T.astype(BF16), xdec[:, g * gw:(g + 1) * gw], preferred_element_type=F32)
        st_ref[g] = st * e_g[tot_row:tot_row + 1, :] + upd
    y = jnp.concatenate(ys, axis=1)

    if rev:
        y = yf_ref[...] + y + dsk_ref[...] * xs
        z = z_ref[...].astype(F32)
        u = y * (z * _sigmoid(z))
        ms = jnp.mean(u * u, axis=-1, keepdims=True)
        out_ref[...] = (u * lax.rsqrt(ms + EPS) * ng_ref[...]).astype(BF16)
    else:
        out_ref[...] = y


def _ssd(proj, dt_raw, z_col, yf, params, bsz, seq, rev):
    t = proj.shape[0]
    ck = SSD_CHUNK
    nc = seq // ck
    d_xbc = params["cw"].shape[1]
    d_ssd = SSD_HEADS * SSD_HEADDIM
    per = ck // HALO
    last_halo = t // HALO - 1

    def ce_of(c):
        return nc - 1 - c if rev else c

    main = lambda b, c: (b * nc + ce_of(c), 0)
    prev = lambda b, c: (jnp.maximum((b * nc + ce_of(c)) * per - 1, 0), 0)
    nxt = lambda b, c: (jnp.minimum((b * nc + ce_of(c)) * per + per, last_halo), 0)
    const2 = lambda b, c: (0, 0)
    in_specs = [pl.BlockSpec((ck, d_xbc), main),
                pl.BlockSpec((HALO, d_xbc), prev),
                pl.BlockSpec((HALO, d_xbc), nxt),
                pl.BlockSpec((ck, LANES), main)]
    args = [proj, proj, proj, dt_raw]
    if rev:
        in_specs += [pl.BlockSpec((ck, d_ssd), lambda b, c: (b * nc + ce_of(c), z_col)),
                     pl.BlockSpec((ck, d_ssd), main)]
        args += [proj, yf]
    in_specs += [pl.BlockSpec((8, d_xbc), const2), pl.BlockSpec((1, d_xbc), const2),
                 pl.BlockSpec((1, LANES), const2), pl.BlockSpec((1, LANES), const2),
                 pl.BlockSpec((LANES, d_ssd), const2)]
    d = "b" if rev else "f"
    args += [params["cw"], params["cb"], params["dtb_" + d], params["a_" + d], params["e2_" + d]]
    if rev:
        in_specs += [pl.BlockSpec((1, d_ssd), const2), pl.BlockSpec((1, d_ssd), const2)]
        args += [params["dsk"], params["ng"]]
    return pl.pallas_call(
        functools.partial(_ssd_kernel, rev=rev),
        grid=(bsz, nc),
        in_specs=in_specs,
        out_specs=pl.BlockSpec((ck, d_ssd), main),
        out_shape=jax.ShapeDtypeStruct((t, d_ssd), BF16 if rev else F32),
        scratch_shapes=[pltpu.VMEM((SSD_GROUPS, SSD_STATE, d_ssd // SSD_GROUPS), F32)],
        compiler_params=_cparams(("arbitrary", "arbitrary")),
        name="ssd_bwd" if rev else "ssd_fwd",
    )(*args)


def _na_plan(rows):
    nblk = rows // NA_QROWS
    variants, var_of_blk, ks_of_blk = [], [], []
    for i in range(nblk):
        ks = int(np.clip(i * NA_QROWS - NA_KH // 2, 0, rows - NA_KROWS))
        drow = np.zeros((NA_QROWS, NA_KROWS), np.int64)
        ok = np.zeros((NA_QROWS, NA_KROWS), bool)
        for dr in range(NA_QROWS):
            r = i * NA_QROWS + dr
            rs = int(np.clip(r - NA_KH // 2, 0, rows - NA_KH))
            for j in range(NA_KH):
                w = rs + j - ks
                drow[dr, w] = rs + j - r + NA_KH - 1
                ok[dr, w] = True
        key = (drow.tobytes(), ok.tobytes())
        for vi, (k2, _, _) in enumerate(variants):
            if k2 == key:
                var_of_blk.append(vi)
                break
        else:
            var_of_blk.append(len(variants))
            variants.append((key, drow, ok))
        ks_of_blk.append(ks)
    return [(d, o) for _, d, o in variants], var_of_blk, ks_of_blk


def _na_bias(rpb, rows):
    variants, var_of_blk, ks_of_blk = _na_plan(rows)
    cols = np.arange(GRID_W)
    win_start = np.clip(cols - NA_KW // 2, 0, GRID_W - NA_KW)
    in_win = (cols[None, :] >= win_start[:, None]) & (cols[None, :] < win_start[:, None] + NA_KW)
    dcol = np.clip(cols[None, :] - cols[:, None] + NA_KW - 1, 0, 2 * NA_KW - 2)
    onehot = (dcol[:, :, None] == np.arange(2 * NA_KW - 1)).astype(np.float32)
    band = jnp.einsum("hab,qkb->haqk", rpb.astype(F32), onehot, precision=lax.Precision.HIGHEST)
    band = jnp.where(in_win[None, None], band, -jnp.inf)
    neg = jnp.full((rpb.shape[0], GRID_W, GRID_W), -jnp.inf, F32)
    tabs = []
    for drow, ok in variants:
        qrows = [jnp.concatenate([band[:, drow[dr, w]] if ok[dr, w] else neg
                                  for w in range(NA_KROWS)], axis=2) for dr in range(NA_QROWS)]
        tabs.append(jnp.concatenate(qrows, axis=1))
    return jnp.stack(tabs), var_of_blk, ks_of_blk


def _na_kernel(var_ref, ks_ref, q_ref, k_ref, v_ref, bias_ref, o_ref):
    mq = NA_QROWS * GRID_W
    nk = NA_KROWS * GRID_W
    nblk = q_ref.shape[0] // mq
    lo_half = lax.broadcasted_iota(I32, (mq, LANES), 1) < NA_HEADDIM

    def blk(i, carry):
        q0 = pl.multiple_of(i * mq, mq)
        k0 = pl.multiple_of(ks_ref[i] * GRID_W, GRID_W)
        qb = q_ref[pl.ds(q0, mq), :] * (NA_HEADDIM ** -0.5)
        zero = jnp.zeros_like(qb)
        q2 = jnp.concatenate([jnp.where(lo_half, qb, zero), jnp.where(lo_half, zero, qb)], axis=0)
        kb = k_ref[pl.ds(k0, nk), :]
        vb = v_ref[pl.ds(k0, nk), :]
        s = lax.dot_general(q2, kb, (((1,), (1,)), ((), ())), preferred_element_type=F32)
        s = s + bias_ref[var_ref[i]].reshape(2 * mq, nk)
        m = jnp.max(s, axis=-1, keepdims=True)
        p = jnp.exp(s - m)
        l = jnp.sum(p, axis=-1, keepdims=True)
        o2 = jnp.dot(p.astype(BF16), vb, preferred_element_type=F32) / l
        o_ref[pl.ds(q0, mq), :] = jnp.where(lo_half, o2[:mq], o2[mq:]).astype(BF16)
        return carry

    lax.fori_loop(0, nblk, blk, 0, unroll=2)


def _na(proj, bias, var_of_blk, ks_of_blk, q_col, bsz, seq):
    t = proj.shape[0]
    npair = NA_HEADS // 2
    nv = bias.shape[0]
    mq = NA_QROWS * GRID_W
    nk = NA_KROWS * GRID_W
    grid_spec = pltpu.PrefetchScalarGridSpec(
        num_scalar_prefetch=2,
        grid=(bsz, npair),
        in_specs=[pl.BlockSpec((seq, LANES), lambda b, p, *_: (b, q_col + p)),
                  pl.BlockSpec((seq, LANES), lambda b, p, *_: (b, q_col + npair + p)),
                  pl.BlockSpec((seq, LANES), lambda b, p, *_: (b, q_col + 2 * npair + p)),
                  pl.BlockSpec((nv, 2, mq, nk), lambda b, p, *_: (0, p, 0, 0))],
        out_specs=pl.BlockSpec((seq, LANES), lambda b, p, *_: (b, p)),
    )
    return pl.pallas_call(
        _na_kernel,
        grid_spec=grid_spec,
        out_shape=jax.ShapeDtypeStruct((t, NA_HEADS * NA_HEADDIM), BF16),
        compiler_params=_cparams(("arbitrary", "arbitrary")),
        name="na",
    )(jnp.asarray(var_of_blk, I32), jnp.asarray(ks_of_blk, I32), proj, proj, proj, bias)


def _mix_kernel(x_ref, yn_ref, at_ref, gs_ref, gn_ref, g1_ref, sh_ref, sc_ref, ng_ref,
                ws_ref, wn_ref, wo_ref, wq_ref, keys_ref, h_ref, s_ref):
    y1 = jnp.dot(yn_ref[...], ws_ref[...], preferred_element_type=F32)
    y2 = jnp.dot(at_ref[...], wn_ref[...], preferred_element_type=F32)
    mixed = _sigmoid(gs_ref[...].astype(F32)) * y1 + _sigmoid(gn_ref[...].astype(F32)) * y2
    h = x_ref[...] + g1_ref[0] * jnp.dot(mixed.astype(BF16), wo_ref[...], preferred_element_type=F32)
    h_ref[...] = h
    n2 = _rms_mod(h, ng_ref[...], sh_ref[0], sc_ref[0]).astype(BF16)
    q = jnp.dot(n2, wq_ref[...], preferred_element_type=F32).astype(BF16)
    for hz in range(2 * PEER_HEADS):
        qs = q[:, hz * PEER_DHALF:(hz + 1) * PEER_DHALF]
        s_ref[hz] = lax.dot_general(keys_ref[hz % 2], qs, (((1,), (1,)), ((), ())),
                                    preferred_element_type=F32)


def _mix(x2, ynorm, attn, proj, gs_col, g1, sh2, sc2, ng2, ws, wn, wo, wq, keys, seq):
    t, d = x2.shape
    tm = min(512, seq)
    bidx = lambda i: ((i * tm) // seq, 0, 0)
    row = lambda i: (i, 0)
    const2 = lambda i: (0, 0)
    return pl.pallas_call(
        _mix_kernel,
        grid=(t // tm,),
        in_specs=[pl.BlockSpec((tm, d), row), pl.BlockSpec((tm, d), row), pl.BlockSpec((tm, d), row),
                  pl.BlockSpec((tm, d), lambda i: (i, gs_col)),
                  pl.BlockSpec((tm, d), lambda i: (i, gs_col + 1)),
                  pl.BlockSpec((1, 1, d), bidx), pl.BlockSpec((1, 1, d), bidx),
                  pl.BlockSpec((1, 1, d), bidx), pl.BlockSpec((1, d), const2),
                  pl.BlockSpec((d, d), const2, pipeline_mode=pl.Buffered(1)),
                  pl.BlockSpec((d, d), const2, pipeline_mode=pl.Buffered(1)),
                  pl.BlockSpec((d, d), const2, pipeline_mode=pl.Buffered(1)),
                  pl.BlockSpec(wq.shape, const2, pipeline_mode=pl.Buffered(1)),
                  pl.BlockSpec(keys.shape, lambda i: (0, 0, 0), pipeline_mode=pl.Buffered(1))],
        out_specs=[pl.BlockSpec((tm, d), row),
                   pl.BlockSpec((2 * PEER_HEADS, PEER_NKEYS, tm), lambda i: (0, 0, i))],
        out_shape=[jax.ShapeDtypeStruct((t, d), F32),
                   jax.ShapeDtypeStruct((2 * PEER_HEADS, PEER_NKEYS, t), F32)],
        compiler_params=_cparams(("arbitrary",)),
        name="mix",
    )(x2, ynorm, attn, proj, proj, g1, sh2, sc2, ng2, ws, wn, wo, wq, keys)


def _hyperbola():
    return [(i, k) for i in range(PEER_TOPK) for k in range(PEER_TOPK)
            if (i + 1) * (k + 1) <= PEER_TOPK]


def _sort_network(n):
    pairs = []
    p = 1
    while p < n:
        k = p
        while k >= 1:
            for j in range(k % p, n - k, 2 * k):
                for i in range(min(k, n - j - k)):
                    if (i + j) // (2 * p) == (i + j + k) // (2 * p):
                        pairs.append((i + j, i + j + k))
            k //= 2
        p *= 2
    return pairs


def _precedes(va, pa, vb, pb):
    return (va > vb) | ((va == vb) & (pa < pb))


def _compare_exchange(v, p, i, j):
    c = _precedes(v[i], p[i], v[j], p[j])
    v[i], v[j] = jnp.where(c, v[i], v[j]), jnp.where(c, v[j], v[i])
    p[i], p[j] = jnp.where(c, p[i], p[j]), jnp.where(c, p[j], p[i])


def _top16(get, n, sv_ref, sp_ref):
    k = PEER_TOPK
    ng = n // k
    net = _sort_network(k)
    for g in range(ng):
        items = [get(g * k + j) for j in range(k)]
        v = [it[0] for it in items]
        p = [it[1] for it in items]
        for i, j in net:
            _compare_exchange(v, p, i, j)
        if ng == 1:
            return v, p
        for j in range(k):
            sv_ref[g, j] = v[j]
            sp_ref[g, j] = p[j]
    step = 1
    while True:
        for g in range(0, ng, 2 * step):
            v, p = [], []
            for j in range(k):
                xv, xp = sv_ref[g, j], sp_ref[g, j]
                yv, yp = sv_ref[g + step, k - 1 - j], sp_ref[g + step, k - 1 - j]
                c = _precedes(xv, xp, yv, yp)
                v.append(jnp.where(c, xv, yv))
                p.append(jnp.where(c, xp, yp))
            stride = k // 2
            while stride >= 1:
                for i in range(k):
                    if i & stride == 0:
                        _compare_exchange(v, p, i, i + stride)
                stride //= 2
            if 2 * step >= ng:
                return v, p
            for j in range(k):
                sv_ref[g, j] = v[j]
                sp_ref[g, j] = p[j]
        step *= 2


def _select_kernel(s_ref, eidx_ref, gate_ref, sv_ref, sp_ref):
    nk = PEER_NKEYS
    k = PEER_TOPK
    shape = s_ref.shape[2:]
    ebits = (nk * nk - 1).bit_length()

    tops = []
    for z in range(2):
        tops.append(_top16(lambda j, z=z: (s_ref[z, j], jnp.full(shape, j, I32)), nk, sv_ref, sp_ref))
    (tv0, ti0), (tv1, ti1) = tops

    pairs = _hyperbola()
    npad = -len(pairs) % k

    def cand(j):
        if j >= len(pairs):
            return jnp.full(shape, -jnp.inf, F32), jnp.full(shape, (k * k) << ebits, I32)
        i, kk = pairs[j]
        return tv0[i] + tv1[kk], ((i * k + kk) << ebits) + ti0[i] * nk + ti1[kk]

    top, ids = _top16(cand, len(pairs) + npad, sv_ref, sp_ref)
    ex = [jnp.exp(t - top[0]) for t in top]
    zsum = ex[0]
    for r in range(1, k):
        zsum = zsum + ex[r]
    for r in range(k):
        eidx_ref[0, r] = ids[r] & ((1 << ebits) - 1)
        gate_ref[0, r] = ex[r] / zsum


def _select(s4):
    nhz, nk, tb, _ = s4.shape
    r = min(8, tb)
    oshape = (PEER_HEADS, PEER_TOPK, tb, LANES)
    return pl.pallas_call(
        _select_kernel,
        grid=(tb // r, PEER_HEADS),
        in_specs=[pl.BlockSpec((2, nk, r, LANES), lambda i, h: (h, 0, i, 0))],
        out_specs=[pl.BlockSpec((1, PEER_TOPK, r, LANES), lambda i, h: (h, 0, i, 0)),
                   pl.BlockSpec((1, PEER_TOPK, r, LANES), lambda i, h: (h, 0, i, 0))],
        out_shape=[jax.ShapeDtypeStruct(oshape, I32), jax.ShapeDtypeStruct(oshape, F32)],
        scratch_shapes=[pltpu.VMEM((nk // PEER_TOPK, PEER_TOPK, r, LANES), F32),
                        pltpu.VMEM((nk // PEER_TOPK, PEER_TOPK, r, LANES), I32)],
        compiler_params=_cparams(("arbitrary", "arbitrary")),
        name="select",
    )(s4)


def _peer_kernel(h_ref, sh_ref, sc_ref, g2_ref, ng_ref, fg_ref, eidx_ref, gate_ref, uv_ref,
                 out_ref, n2_ref, act_ref, wv_ref, w3_ref, acc_ref, *, ns):
    nk = PEER_NKEYS
    half = nk // 2
    hi_mask = jnp.uint32(0xFFFF0000)
    tm = h_ref.shape[0]
    eb = uv_ref.shape[0]
    nblk = eb // nk
    nblk2 = nblk // 2
    s = pl.program_id(1)

    @pl.when(s == 0)
    def _():
        n2_ref[...] = _rms_mod(h_ref[...], ng_ref[...], sh_ref[0], sc_ref[0]).astype(BF16)
        act_ref[...] = jnp.zeros_like(act_ref)

    @pl.when(s < ns)
    def _():
        e = eidx_ref[...]
        ai = jnp.right_shift(e, nk.bit_length() - 1)
        bi = jnp.bitwise_and(e, nk - 1)
        act = act_ref[...]
        n2 = n2_ref[...]
        for a in range(0, nblk, 2):
            sc = lax.dot_general(n2, uv_ref[a * nk:(a + 2) * nk, :], (((1,), (1,)), ((), ())),
                                 preferred_element_type=F32)
            for j in range(2):
                got = jnp.take_along_axis(sc[:, j * nk:(j + 1) * nk], bi, axis=1)
                act = jnp.where(ai == s * nblk + a + j, got, act)
        act_ref[...] = act

    @pl.when(s == ns - 1)
    def _():
        act = act_ref[...]
        gelu = 0.5 * act * (1.0 + lax.erf(act * (2.0 ** -0.5)))
        wv_ref[...] = gate_ref[...] * gelu
        sub = lax.broadcasted_iota(I32, (nk, LANES), 0)

        def tok(t, carry):
            e = eidx_ref[pl.ds(t, 1), :]
            w = wv_ref[pl.ds(t, 1), :]
            ar = jnp.right_shift(e, nk.bit_length() - 1)
            br = jnp.bitwise_and(e, nk - 1)
            pt = jnp.where(sub == ar, w, 0.0).astype(BF16)
            qt = jnp.where(sub == br, 1.0, 0.0).astype(BF16)
            wt = lax.dot_general(pt, qt, (((1,), (1,)), ((), ())), preferred_element_type=F32)
            bits = lax.bitcast_convert_type(wt.astype(BF16).astype(F32), jnp.uint32)
            word = jnp.right_shift(bits[:half], 16) | (bits[half:] & hi_mask)
            w3_ref[pl.ds(pl.multiple_of(t * W3_PITCH, 8), half), :] = word
            return carry
        lax.fori_loop(0, tm, tok, 0, unroll=32)

    @pl.when(s >= ns)
    def _():
        base = (s - ns) * nblk2
        words = [w3_ref[pl.ds(base + a, tm, stride=W3_PITCH), :] for a in range(nblk2)]
        lo = [lax.bitcast_convert_type(jnp.left_shift(w, 16), F32).astype(BF16) for w in words]
        hi = [lax.bitcast_convert_type(w & hi_mask, F32).astype(BF16) for w in words]
        part = jnp.dot(jnp.concatenate(lo + hi, axis=1), uv_ref[...], preferred_element_type=F32)

        @pl.when(s == ns)
        def _():
            acc_ref[...] = part

        @pl.when(s > ns)
        def _():
            acc_ref[...] += part

    @pl.when(s == 2 * ns - 1)
    def _():
        hh = h_ref[...] + g2_ref[0] * acc_ref[...]
        ms = jnp.mean(hh * hh, axis=-1, keepdims=True)
        out_ref[...] = hh * lax.rsqrt(ms + EPS) * fg_ref[...]


def _peer(h, sh2, sc2, g2, ng2, fg, eidx_t, gate_t, u, v, seq):
    t, d = h.shape
    ne = u.shape[0]
    tm = min(512, seq)
    eb = 4096
    ns = ne // eb
    nj = eidx_t.shape[1]
    v_blocks = v.reshape(2, ns, eb // 2, d).transpose(1, 0, 2, 3).reshape(ne, d)
    uv = jnp.concatenate([u, v_blocks], axis=0).astype(BF16)
    bidx = lambda i, s: ((i * tm) // seq, 0, 0)
    row = lambda i, s: (i, 0)
    const2 = lambda i, s: (0, 0)
    return pl.pallas_call(
        functools.partial(_peer_kernel, ns=ns),
        grid=(t // tm, 2 * ns),
        in_specs=[pl.BlockSpec((tm, d), row),
                  pl.BlockSpec((1, 1, d), bidx), pl.BlockSpec((1, 1, d), bidx),
                  pl.BlockSpec((1, 1, d), bidx),
                  pl.BlockSpec((1, d), const2), pl.BlockSpec((1, d), const2),
                  pl.BlockSpec((tm, nj), row), pl.BlockSpec((tm, nj), row),
                  pl.BlockSpec((eb, d), lambda i, s: (s, 0))],
        out_specs=pl.BlockSpec((tm, d), row),
        out_shape=jax.ShapeDtypeStruct((t, d), F32),
        scratch_shapes=[pltpu.VMEM((tm, d), BF16),
                        pltpu.VMEM((tm, nj), F32),
                        pltpu.VMEM((tm, nj), F32),
                        pltpu.VMEM((tm * W3_PITCH, PEER_NKEYS), jnp.uint32),
                        pltpu.VMEM((tm, d), F32)],
        compiler_params=_cparams(("arbitrary", "arbitrary")),
        name="peer",
    )(h, sh2, sc2, g2, ng2, fg, eidx_t, gate_t, uv)


def _layer(x2, c, bsz, seq, w_ada, b_ada, norm1_g, w_in, conv_w, conv_b, dt_bias_f, dt_bias_b,
           a_log_f, a_log_b, d_skip, ssd_norm_g, w_ssd_br, na_rpb, w_na_br, w_out, norm2_g,
           peer_wq, peer_keys, peer_u, peer_v, out_g):
    t, d = x2.shape
    d_ssd = SSD_HEADS * SSD_HEADDIM
    d_xbc = d_ssd + 2 * SSD_GROUPS * SSD_STATE
    d_na = NA_HEADS * NA_HEADDIM
    assert seq % SSD_CHUNK == 0 and seq % (GRID_W * NA_QROWS) == 0
    assert seq // GRID_W >= NA_KROWS and d == d_ssd == d_na

    mod = _ada(c, w_ada, b_ada)
    sh1, sc1, g1, sh2, sc2, g2 = [m.reshape(bsz, 1, d) for m in jnp.split(mod, 6, axis=-1)]

    o = np.cumsum([0, d_ssd, d_xbc, SSD_HEADS, SSD_HEADS, 3 * d_na, d, d])
    w_main = jnp.concatenate([w_in[:, o[1]:o[2]], w_in[:, o[0]:o[1]], w_in[:, o[4]:o[7]]],
                             axis=1).astype(BF16)
    w_dt = jnp.pad(w_in[:, o[2]:o[4]], ((0, 0), (0, LANES - 2 * SSD_HEADS))).astype(BF16)
    z_col = d_xbc // d_ssd
    q_col = (d_xbc + d_ssd) // LANES
    gs_col = (d_xbc + d_ssd + 3 * d_na) // d
    proj, dt_raw = _inproj(x2, sh1, sc1, norm1_g.reshape(1, d), w_main, w_dt, seq)

    heads = np.arange(d_ssd) // SSD_HEADDIM

    def e2(off):
        m = np.zeros((LANES, d_ssd), np.float32)
        m[off + heads, np.arange(d_ssd)] = 1.0
        m[off + 64 + heads, np.arange(d_ssd)] = 1.0
        return jnp.asarray(m, BF16)

    def lanes16(vec, off):
        return jnp.zeros((1, LANES), F32).at[0, off:off + SSD_HEADS].set(vec.astype(F32))

    params = {
        "cw": jnp.pad(conv_w.astype(F32), ((0, 8 - CONV_W), (0, 0))),
        "cb": conv_b.astype(F32).reshape(1, d_xbc),
        "dtb_f": lanes16(dt_bias_f, 0), "dtb_b": lanes16(dt_bias_b, SSD_HEADS),
        "a_f": lanes16(-jnp.exp(a_log_f.astype(F32)), 0),
        "a_b": lanes16(-jnp.exp(a_log_b.astype(F32)), SSD_HEADS),
        "e2_f": e2(0), "e2_b": e2(SSD_HEADS),
        "dsk": jnp.repeat(d_skip.astype(F32), SSD_HEADDIM).reshape(1, d_ssd),
        "ng": ssd_norm_g.astype(F32).reshape(1, d_ssd),
    }
    yf = _ssd(proj, dt_raw, z_col, None, params, bsz, seq, rev=False)
    ynorm = _ssd(proj, dt_raw, z_col, yf, params, bsz, seq, rev=True)

    bias, var_of_blk, ks_of_blk = _na_bias(na_rpb, seq // GRID_W)
    attn = _na(proj, bias, var_of_blk, ks_of_blk, q_col, bsz, seq)

    h, scores = _mix(x2, ynorm, attn, proj, gs_col, g1, sh2, sc2, norm2_g.reshape(1, d),
                     w_ssd_br.astype(BF16), w_na_br.astype(BF16), w_out.astype(BF16),
                     peer_wq.astype(BF16), peer_keys.astype(BF16), seq)

    eidx, gate = _select(scores.reshape(2 * PEER_HEADS, PEER_NKEYS, t // LANES, LANES))
    nj = PEER_HEADS * PEER_TOPK
    eidx_t = eidx.reshape(nj, t).T
    gate_t = gate.reshape(nj, t).T
    return _peer(h, sh2, sc2, g2, norm2_g.reshape(1, d), out_g, eidx_t, gate_t,
                 peer_u.astype(BF16), peer_v.astype(BF16), seq)


def kernel(x, c, w_ada, b_ada, norm1_g, w_in, conv_w, conv_b, dt_bias_f, dt_bias_b, a_log_f, a_log_b,
           d_skip, ssd_norm_g, w_ssd_br, na_rpb, w_na_br, w_out, norm2_g, peer_wq, peer_keys, peer_u,
           peer_v, final_g):
    bsz, seq, d = x.shape
    depth = w_ada.shape[0]
    assert depth == 1, "the final RMSNorm is fused into the last layer's PEER kernel"
    i = 0
    out = _layer(x.reshape(bsz * seq, d), c, bsz, seq, w_ada[i], b_ada[i], norm1_g[i], w_in[i],
                 conv_w[i], conv_b[i], dt_bias_f[i], dt_bias_b[i], a_log_f[i], a_log_b[i], d_skip[i],
                 ssd_norm_g[i], w_ssd_br[i], na_rpb[i], w_na_br[i], w_out[i], norm2_g[i],
                 peer_wq[i], peer_keys[i], peer_u[i], peer_v[i], final_g.reshape(1, d))
    return out.reshape(bsz, seq, d)
```

```python
import functools

import numpy as np
import jax
import jax.numpy as jnp
from jax import lax
from jax.experimental import pallas as pl
from jax.experimental.pallas import tpu as pltpu

F32 = jnp.float32
BF16 = jnp.bfloat16
I32 = jnp.int32

EPS = 1e-6
GRID_W = 64
SSD_HEADS = 16
SSD_HEADDIM = 64
SSD_GROUPS = 4
SSD_STATE = 128
SSD_CHUNK = 128
CONV_W = 5
NA_HEADS = 16
NA_HEADDIM = 64
NA_KH = 8
NA_KW = 16
NA_QROWS = 4
NA_KROWS = 12
PEER_HEADS = 8
PEER_NKEYS = 128
PEER_TOPK = 16
PEER_DHALF = 128
W3_PITCH = PEER_NKEYS // 2 + 8

LANES = 128
HALO = 16
VMEM_LIMIT = 56 * 1024 * 1024


def _cparams(sem):
    return pltpu.CompilerParams(dimension_semantics=sem, vmem_limit_bytes=VMEM_LIMIT)


def _sigmoid(x):
    return 1.0 / (1.0 + jnp.exp(-x))


def _rms_mod(x, g, shift, scale):
    ms = jnp.mean(x * x, axis=-1, keepdims=True)
    y = x * lax.rsqrt(ms + EPS) * g
    return y * (1.0 + scale) + shift


def _ada_kernel(c_ref, w_ref, b_ref, o_ref):
    c = c_ref[...]
    sc = c * _sigmoid(c)
    o_ref[...] = jnp.dot(sc, w_ref[...], preferred_element_type=F32,
                         precision=lax.Precision.HIGHEST) + b_ref[...]


def _ada(c, w, b):
    bsz, d = c.shape
    n = w.shape[1]
    tn = 1024
    return pl.pallas_call(
        _ada_kernel,
        grid=(n // tn,),
        in_specs=[pl.BlockSpec((bsz, d), lambda j: (0, 0)),
                  pl.BlockSpec((d, tn), lambda j: (0, j)),
                  pl.BlockSpec((1, tn), lambda j: (0, j))],
        out_specs=pl.BlockSpec((bsz, tn), lambda j: (0, j)),
        out_shape=jax.ShapeDtypeStruct((bsz, n), F32),
        compiler_params=_cparams(("arbitrary",)),
        name="ada",
    )(c, w, b.reshape(1, n))


def _inproj_kernel(x_ref, sh_ref, sc_ref, g_ref, w_ref, wdt_ref, proj_ref, dt_ref):
    n1 = _rms_mod(x_ref[...], g_ref[...], sh_ref[0], sc_ref[0]).astype(BF16)
    dt_ref[...] = jnp.dot(n1, wdt_ref[...], preferred_element_type=F32)
    tn = 1024
    for j in range(w_ref.shape[1] // tn):
        proj_ref[:, j * tn:(j + 1) * tn] = jnp.dot(
            n1, w_ref[:, j * tn:(j + 1) * tn], preferred_element_type=F32).astype(BF16)


def _inproj(x2, sh, sc, g, w, wdt, seq):
    t, d = x2.shape
    n = w.shape[1]
    tm = min(512, seq)
    bidx = lambda i: ((i * tm) // seq, 0, 0)
    const2 = lambda i: (0, 0)
    return pl.pallas_call(
        _inproj_kernel,
        grid=(t // tm,),
        in_specs=[pl.BlockSpec((tm, d), lambda i: (i, 0)),
                  pl.BlockSpec((1, 1, d), bidx),
                  pl.BlockSpec((1, 1, d), bidx),
                  pl.BlockSpec((1, d), const2),
                  pl.BlockSpec((d, n), const2, pipeline_mode=pl.Buffered(1)),
                  pl.BlockSpec((d, LANES), const2, pipeline_mode=pl.Buffered(1))],
        out_specs=[pl.BlockSpec((tm, n), lambda i: (i, 0)),
                   pl.BlockSpec((tm, LANES), lambda i: (i, 0))],
        out_shape=[jax.ShapeDtypeStruct((t, n), BF16),
                   jax.ShapeDtypeStruct((t, LANES), F32)],
        compiler_params=_cparams(("arbitrary",)),
        name="inproj",
    )(x2, sh, sc, g, w, wdt)


def _expand_heads(v, e2_ref):
    hi = v.astype(BF16).astype(F32)
    comb = (hi + pltpu.roll(v - hi, 64, 1)).astype(BF16)
    return jnp.dot(comb, e2_ref[...], preferred_element_type=F32)


def _ssd_kernel(*refs, rev):
    if rev:
        (xs_ref, bc_ref, dt_ref, z_ref, yf_ref, dtb_ref, a_ref, e2_ref, dsk_ref, ng_ref,
         out_ref, st_ref) = refs
    else:
        (xm_ref, xp_ref, xn_ref, dt_ref, cw_ref, cb_ref, dtb_ref, a_ref, e2_ref,
         out_ref, xs_out_ref, bc_out_ref, st_ref) = refs
    ck = SSD_CHUNK
    d_ssd = SSD_HEADS * SSD_HEADDIM
    d_bc = SSD_GROUPS * SSD_STATE
    gw = d_ssd // SSD_GROUPS
    c = pl.program_id(1)
    nc = pl.num_programs(1)
    ce = nc - 1 - c if rev else c

    @pl.when(c == 0)
    def _():
        st_ref[...] = jnp.zeros_like(st_ref)

    if rev:
        xs = xs_ref[...]
        bc = bc_ref[...]
    else:
        main = xm_ref[...]
        zero_halo = jnp.zeros((HALO, main.shape[1]), main.dtype)
        ext = jnp.concatenate([jnp.where(ce == 0, zero_halo, xp_ref[...]), main,
                               jnp.where(ce == nc - 1, zero_halo, xn_ref[...])], axis=0)
        erow = lax.broadcasted_iota(I32, (ck, ck + 2 * HALO), 0)
        ecol = lax.broadcasted_iota(I32, (ck, ck + 2 * HALO), 1)
        mid = CONV_W // 2
        acc = cb_ref[...] + main.astype(F32) * cw_ref[mid:mid + 1, :]
        for w in range(CONV_W):
            if w != mid:
                shift = jnp.where(ecol == erow + (HALO + w - mid), 1.0, 0.0).astype(BF16)
                acc = acc + jnp.dot(shift, ext, preferred_element_type=F32) * cw_ref[w:w + 1, :]
        xbc = acc * _sigmoid(acc)
        xs = xbc[:, :d_ssd]
        bc = xbc[:, d_ssd:].astype(BF16)
        xs_out_ref[...] = xs
        bc_out_ref[...] = bc

    lane = lax.broadcasted_iota(I32, (ck, LANES), 1)
    row = lax.broadcasted_iota(I32, (ck, LANES), 0)
    off = SSD_HEADS if rev else 0
    valid = (lane >= off) & (lane < off + SSD_HEADS)
    dtr = dt_ref[...] + dtb_ref[...]
    dt = jnp.where(valid, jnp.maximum(dtr, 0.0) + jnp.log1p(jnp.exp(-jnp.abs(dtr))), 0.0)
    da = dt * a_ref[...]
    tri = (lane >= row) if rev else (lane <= row)
    trib = jnp.where(tri, 1.0, 0.0).astype(BF16)
    hi = da.astype(BF16)
    r1 = da - hi.astype(F32)
    mid = r1.astype(BF16)
    lo = (r1 - mid.astype(F32)).astype(BF16)
    cs = (jnp.dot(trib, hi, preferred_element_type=F32)
          + jnp.dot(trib, mid, preferred_element_type=F32)
          + jnp.dot(trib, lo, preferred_element_type=F32))
    cst = cs.T
    tot_row = 0 if rev else ck - 1
    tot = cs[tot_row:tot_row + 1, :]
    dec = jnp.where(valid, jnp.exp(tot - cs), 0.0)
    ecs = jnp.where(valid, jnp.exp(cs), 0.0)
    ecs_e = _expand_heads(ecs, e2_ref)
    xdt = xs * _expand_heads(dt, e2_ref)
    xdec = (xs * _expand_heads(dt * dec, e2_ref)).astype(BF16)

    lo_half = lax.broadcasted_iota(I32, (ck, LANES), 1) < SSD_HEADDIM
    hpg = SSD_HEADS // SSD_GROUPS
    ys = []
    for g in range(SSD_GROUPS):
        bg = bc[:, g * SSD_STATE:(g + 1) * SSD_STATE]
        cg = bc[:, d_bc + g * SSD_STATE:d_bc + (g + 1) * SSD_STATE]
        cbg = lax.dot_general(cg, bg, (((1,), (1,)), ((), ())), preferred_element_type=F32)
        yg = []
        for q in range(hpg // 2):
            ls = []
            for e in range(2):
                col = off + g * hpg + q * 2 + e
                dlt = cs[:, col:col + 1] - cst[col:col + 1, :]
                seg = jnp.exp(jnp.where(tri, dlt, -jnp.inf))
                ls.append((cbg * seg).astype(BF16))
            blk = g * (hpg // 2) + q
            xq = xdt[:, blk * LANES:(blk + 1) * LANES]
            rhs = jnp.concatenate([jnp.where(lo_half, xq, 0.0), jnp.where(lo_half, 0.0, xq)],
                                  axis=0).astype(BF16)
            yg.append(jnp.dot(jnp.concatenate(ls, axis=1), rhs, preferred_element_type=F32))
        y_diag = jnp.concatenate(yg, axis=1)
        st = st_ref[g]
        e_g = ecs_e[:, g * gw:(g + 1) * gw]
        y_off = jnp.dot(cg, st.astype(BF16), preferred_element_type=F32) * e_g
        ys.append(y_diag + y_off)
        upd = jnp.dot(bg.astype(F32).T.astype(BF16), xdec[:, g * gw:(g + 1) * gw],
                      preferred_element_type=F32)
        st_ref[g] = st * e_g[tot_row:tot_row + 1, :] + upd
    y = jnp.concatenate(ys, axis=1)

    if rev:
        y = yf_ref[...] + y + dsk_ref[...] * xs
        z = z_ref[...].astype(F32)
        u = y * (z * _sigmoid(z))
        ms = jnp.mean(u * u, axis=-1, keepdims=True)
        out_ref[...] = (u * lax.rsqrt(ms + EPS) * ng_ref[...]).astype(BF16)
    else:
        out_ref[...] = y


def _ssd(proj, dt_raw, z_col, fwd, params, bsz, seq, rev):
    t = proj.shape[0]
    ck = SSD_CHUNK
    nc = seq // ck
    d_xbc = params["cw"].shape[1]
    d_ssd = SSD_HEADS * SSD_HEADDIM
    d_bc = d_xbc - d_ssd
    per = ck // HALO
    last_halo = t // HALO - 1

    def ce_of(c):
        return nc - 1 - c if rev else c

    main = lambda b, c: (b * nc + ce_of(c), 0)
    prev = lambda b, c: (jnp.maximum((b * nc + ce_of(c)) * per - 1, 0), 0)
    nxt = lambda b, c: (jnp.minimum((b * nc + ce_of(c)) * per + per, last_halo), 0)
    const2 = lambda b, c: (0, 0)
    d = "b" if rev else "f"
    dir_specs = [pl.BlockSpec((1, LANES), const2), pl.BlockSpec((1, LANES), const2),
                 pl.BlockSpec((LANES, d_ssd), const2)]
    dir_args = [params["dtb_" + d], params["a_" + d], params["e2_" + d]]
    if rev:
        yf, xs, bc = fwd
        in_specs = [pl.BlockSpec((ck, d_ssd), main), pl.BlockSpec((ck, d_bc), main),
                    pl.BlockSpec((ck, LANES), main),
                    pl.BlockSpec((ck, d_ssd), lambda b, c: (b * nc + ce_of(c), z_col)),
                    pl.BlockSpec((ck, d_ssd), main)] + dir_specs + [
                        pl.BlockSpec((1, d_ssd), const2), pl.BlockSpec((1, d_ssd), const2)]
        args = [xs, bc, dt_raw, proj, yf] + dir_args + [params["dsk"], params["ng"]]
        out_specs = pl.BlockSpec((ck, d_ssd), main)
        out_shape = jax.ShapeDtypeStruct((t, d_ssd), BF16)
    else:
        in_specs = [pl.BlockSpec((ck, d_xbc), main), pl.BlockSpec((HALO, d_xbc), prev),
                    pl.BlockSpec((HALO, d_xbc), nxt), pl.BlockSpec((ck, LANES), main),
                    pl.BlockSpec((8, d_xbc), const2), pl.BlockSpec((1, d_xbc), const2)] + dir_specs
        args = [proj, proj, proj, dt_raw, params["cw"], params["cb"]] + dir_args
        out_specs = [pl.BlockSpec((ck, d_ssd), main), pl.BlockSpec((ck, d_ssd), main),
                     pl.BlockSpec((ck, d_bc), main)]
        out_shape = [jax.ShapeDtypeStruct((t, d_ssd), F32), jax.ShapeDtypeStruct((t, d_ssd), F32),
                     jax.ShapeDtypeStruct((t, d_bc), BF16)]
    return pl.pallas_call(
        functools.partial(_ssd_kernel, rev=rev),
        grid=(bsz, nc),
        in_specs=in_specs,
        out_specs=out_specs,
        out_shape=out_shape,
        scratch_shapes=[pltpu.VMEM((SSD_GROUPS, SSD_STATE, d_ssd // SSD_GROUPS), F32)],
        compiler_params=_cparams(("arbitrary", "arbitrary")),
        name="ssd_bwd" if rev else "ssd_fwd",
    )(*args)


def _na_plan(rows):
    nblk = rows // NA_QROWS
    variants, var_of_blk, ks_of_blk = [], [], []
    for i in range(nblk):
        ks = int(np.clip(i * NA_QROWS - NA_KH // 2, 0, rows - NA_KROWS))
        drow = np.zeros((NA_QROWS, NA_KROWS), np.int64)
        ok = np.zeros((NA_QROWS, NA_KROWS), bool)
        for dr in range(NA_QROWS):
            r = i * NA_QROWS + dr
            rs = int(np.clip(r - NA_KH // 2, 0, rows - NA_KH))
            for j in range(NA_KH):
                w = rs + j - ks
                drow[dr, w] = rs + j - r + NA_KH - 1
                ok[dr, w] = True
        key = (drow.tobytes(), ok.tobytes())
        for vi, (k2, _, _) in enumerate(variants):
            if k2 == key:
                var_of_blk.append(vi)
                break
        else:
            var_of_blk.append(len(variants))
            variants.append((key, drow, ok))
        ks_of_blk.append(ks)
    return [(d, o) for _, d, o in variants], var_of_blk, ks_of_blk


def _na_bias(rpb, rows):
    variants, var_of_blk, ks_of_blk = _na_plan(rows)
    cols = np.arange(GRID_W)
    win_start = np.clip(cols - NA_KW // 2, 0, GRID_W - NA_KW)
    in_win = (cols[None, :] >= win_start[:, None]) & (cols[None, :] < win_start[:, None] + NA_KW)
    dcol = np.clip(cols[None, :] - cols[:, None] + NA_KW - 1, 0, 2 * NA_KW - 2)
    onehot = (dcol[:, :, None] == np.arange(2 * NA_KW - 1)).astype(np.float32)
    band = jnp.einsum("hab,qkb->haqk", rpb.astype(F32), onehot, precision=lax.Precision.HIGHEST)
    band = jnp.where(in_win[None, None], band, -jnp.inf)
    neg = jnp.full((rpb.shape[0], GRID_W, GRID_W), -jnp.inf, F32)
    tabs = []
    for drow, ok in variants:
        qrows = [jnp.concatenate([band[:, drow[dr, w]] if ok[dr, w] else neg
                                  for w in range(NA_KROWS)], axis=2) for dr in range(NA_QROWS)]
        tabs.append(jnp.concatenate(qrows, axis=1))
    return jnp.stack(tabs), var_of_blk, ks_of_blk


def _na_kernel(var_ref, ks_ref, q_ref, k_ref, v_ref, bias_ref, o_ref):
    mq = NA_QROWS * GRID_W
    nk = NA_KROWS * GRID_W
    nblk = q_ref.shape[0] // mq
    lo_half = lax.broadcasted_iota(I32, (mq, LANES), 1) < NA_HEADDIM

    def blk(i, carry):
        q0 = pl.multiple_of(i * mq, mq)
        k0 = pl.multiple_of(ks_ref[i] * GRID_W, GRID_W)
        qb = q_ref[pl.ds(q0, mq), :] * (NA_HEADDIM ** -0.5)
        zero = jnp.zeros_like(qb)
        kb = k_ref[pl.ds(k0, nk), :]
        vb = v_ref[pl.ds(k0, nk), :]
        var = var_ref[i]
        outs = []
        for hd in range(2):
            qh = jnp.where(lo_half, qb, zero) if hd == 0 else jnp.where(lo_half, zero, qb)
            s = lax.dot_general(qh, kb, (((1,), (1,)), ((), ())), preferred_element_type=F32)
            s = s + bias_ref[var, hd]
            m = jnp.max(s, axis=-1, keepdims=True)
            p = jnp.exp(s - m)
            l = jnp.sum(p, axis=-1, keepdims=True)
            outs.append(jnp.dot(p.astype(BF16), vb, preferred_element_type=F32) / l)
        o_ref[pl.ds(q0, mq), :] = jnp.where(lo_half, outs[0], outs[1]).astype(BF16)
        return carry

    lax.fori_loop(0, nblk, blk, 0, unroll=2)


def _na(proj, bias, var_of_blk, ks_of_blk, q_col, bsz, seq):
    t = proj.shape[0]
    npair = NA_HEADS // 2
    nv = bias.shape[0]
    mq = NA_QROWS * GRID_W
    nk = NA_KROWS * GRID_W
    grid_spec = pltpu.PrefetchScalarGridSpec(
        num_scalar_prefetch=2,
        grid=(bsz, npair),
        in_specs=[pl.BlockSpec((seq, LANES), lambda b, p, *_: (b, q_col + p)),
                  pl.BlockSpec((seq, LANES), lambda b, p, *_: (b, q_col + npair + p)),
                  pl.BlockSpec((seq, LANES), lambda b, p, *_: (b, q_col + 2 * npair + p)),
                  pl.BlockSpec((nv, 2, mq, nk), lambda b, p, *_: (0, p, 0, 0))],
        out_specs=pl.BlockSpec((seq, LANES), lambda b, p, *_: (b, p)),
    )
    return pl.pallas_call(
        _na_kernel,
        grid_spec=grid_spec,
        out_shape=jax.ShapeDtypeStruct((t, NA_HEADS * NA_HEADDIM), BF16),
        compiler_params=_cparams(("arbitrary", "arbitrary")),
        name="na",
    )(jnp.asarray(var_of_blk, I32), jnp.asarray(ks_of_blk, I32), proj, proj, proj, bias)


def _mix_kernel(x_ref, yn_ref, at_ref, gs_ref, gn_ref, g1_ref, sh_ref, sc_ref, ng_ref,
                ws_ref, wn_ref, wo_ref, wq_ref, keys_ref, h_ref, s_ref):
    y1 = jnp.dot(yn_ref[...], ws_ref[...], preferred_element_type=F32)
    y2 = jnp.dot(at_ref[...], wn_ref[...], preferred_element_type=F32)
    mixed = _sigmoid(gs_ref[...].astype(F32)) * y1 + _sigmoid(gn_ref[...].astype(F32)) * y2
    h = x_ref[...] + g1_ref[0] * jnp.dot(mixed.astype(BF16), wo_ref[...], preferred_element_type=F32)
    h_ref[...] = h
    n2 = _rms_mod(h, ng_ref[...], sh_ref[0], sc_ref[0]).astype(BF16)
    q = jnp.dot(n2, wq_ref[...], preferred_element_type=F32).astype(BF16)
    for hz in range(2 * PEER_HEADS):
        qs = q[:, hz * PEER_DHALF:(hz + 1) * PEER_DHALF]
        s_ref[hz] = lax.dot_general(keys_ref[hz % 2], qs, (((1,), (1,)), ((), ())),
                                    preferred_element_type=F32)


def _mix(x2, ynorm, attn, proj, gs_col, g1, sh2, sc2, ng2, ws, wn, wo, wq, keys, seq):
    t, d = x2.shape
    tm = min(512, seq)
    bidx = lambda i: ((i * tm) // seq, 0, 0)
    row = lambda i: (i, 0)
    const2 = lambda i: (0, 0)
    return pl.pallas_call(
        _mix_kernel,
        grid=(t // tm,),
        in_specs=[pl.BlockSpec((tm, d), row), pl.BlockSpec((tm, d), row), pl.BlockSpec((tm, d), row),
                  pl.BlockSpec((tm, d), lambda i: (i, gs_col)),
                  pl.BlockSpec((tm, d), lambda i: (i, gs_col + 1)),
                  pl.BlockSpec((1, 1, d), bidx), pl.BlockSpec((1, 1, d), bidx),
                  pl.BlockSpec((1, 1, d), bidx), pl.BlockSpec((1, d), const2),
                  pl.BlockSpec((d, d), const2, pipeline_mode=pl.Buffered(1)),
                  pl.BlockSpec((d, d), const2, pipeline_mode=pl.Buffered(1)),
                  pl.BlockSpec((d, d), const2, pipeline_mode=pl.Buffered(1)),
                  pl.BlockSpec(wq.shape, const2, pipeline_mode=pl.Buffered(1)),
                  pl.BlockSpec(keys.shape, lambda i: (0, 0, 0), pipeline_mode=pl.Buffered(1))],
        out_specs=[pl.BlockSpec((tm, d), row),
                   pl.BlockSpec((2 * PEER_HEADS, PEER_NKEYS, tm), lambda i: (0, 0, i))],
        out_shape=[jax.ShapeDtypeStruct((t, d), F32),
                   jax.ShapeDtypeStruct((2 * PEER_HEADS, PEER_NKEYS, t), F32)],
        compiler_params=_cparams(("arbitrary",)),
        name="mix",
    )(x2, ynorm, attn, proj, proj, g1, sh2, sc2, ng2, ws, wn, wo, wq, keys)


def _hyperbola():
    return [(i, k) for i in range(PEER_TOPK) for k in range(PEER_TOPK)
            if (i + 1) * (k + 1) <= PEER_TOPK]


def _sort_network(n):
    pairs = []
    p = 1
    while p < n:
        k = p
        while k >= 1:
            for j in range(k % p, n - k, 2 * k):
                for i in range(min(k, n - j - k)):
                    if (i + j) // (2 * p) == (i + j + k) // (2 * p):
                        pairs.append((i + j, i + j + k))
            k //= 2
        p *= 2
    return pairs


def _precedes(va, pa, vb, pb):
    return (va > vb) | ((va == vb) & (pa < pb))


def _compare_exchange(v, p, i, j):
    c = _precedes(v[i], p[i], v[j], p[j])
    v[i], v[j] = jnp.where(c, v[i], v[j]), jnp.where(c, v[j], v[i])
    p[i], p[j] = jnp.where(c, p[i], p[j]), jnp.where(c, p[j], p[i])


def _top16(get, n, sv_ref, sp_ref):
    k = PEER_TOPK
    ng = n // k
    net = _sort_network(k)
    for g in range(ng):
        items = [get(g * k + j) for j in range(k)]
        v = [it[0] for it in items]
        p = [it[1] for it in items]
        for i, j in net:
            _compare_exchange(v, p, i, j)
        if ng == 1:
            return v, p
        for j in range(k):
            sv_ref[g, j] = v[j]
            sp_ref[g, j] = p[j]
    step = 1
    while True:
        for g in range(0, ng, 2 * step):
            v, p = [], []
            for j in range(k):
                xv, xp = sv_ref[g, j], sp_ref[g, j]
                yv, yp = sv_ref[g + step, k - 1 - j], sp_ref[g + step, k - 1 - j]
                c = _precedes(xv, xp, yv, yp)
                v.append(jnp.where(c, xv, yv))
                p.append(jnp.where(c, xp, yp))
            stride = k // 2
            while stride >= 1:
                for i in range(k):
                    if i & stride == 0:
                        _compare_exchange(v, p, i, i + stride)
                stride //= 2
            if 2 * step >= ng:
                return v, p
            for j in range(k):
                sv_ref[g, j] = v[j]
                sp_ref[g, j] = p[j]
        step *= 2


def _select_kernel(s_ref, eidx_ref, gate_ref, sv_ref, sp_ref):
    nk = PEER_NKEYS
    k = PEER_TOPK
    shape = s_ref.shape[2:]
    ebits = (nk * nk - 1).bit_length()

    tops = []
    for z in range(2):
        tops.append(_top16(lambda j, z=z: (s_ref[z, j], jnp.full(shape, j, I32)), nk, sv_ref, sp_ref))
    (tv0, ti0), (tv1, ti1) = tops

    pairs = _hyperbola()
    npad = -len(pairs) % k

    def cand(j):
        if j >= len(pairs):
            return jnp.full(shape, -jnp.inf, F32), jnp.full(shape, (k * k) << ebits, I32)
        i, kk = pairs[j]
        return tv0[i] + tv1[kk], ((i * k + kk) << ebits) + ti0[i] * nk + ti1[kk]

    top, ids = _top16(cand, len(pairs) + npad, sv_ref, sp_ref)
    ex = [jnp.exp(t - top[0]) for t in top]
    zsum = ex[0]
    for r in range(1, k):
        zsum = zsum + ex[r]
    for r in range(k):
        eidx_ref[0, r] = ids[r] & ((1 << ebits) - 1)
        gate_ref[0, r] = ex[r] / zsum


def _select(s4):
    nhz, nk, tb, _ = s4.shape
    r = min(8, tb)
    oshape = (PEER_HEADS, PEER_TOPK, tb, LANES)
    return pl.pallas_call(
        _select_kernel,
        grid=(tb // r, PEER_HEADS),
        in_specs=[pl.BlockSpec((2, nk, r, LANES), lambda i, h: (h, 0, i, 0))],
        out_specs=[pl.BlockSpec((1, PEER_TOPK, r, LANES), lambda i, h: (h, 0, i, 0)),
                   pl.BlockSpec((1, PEER_TOPK, r, LANES), lambda i, h: (h, 0, i, 0))],
        out_shape=[jax.ShapeDtypeStruct(oshape, I32), jax.ShapeDtypeStruct(oshape, F32)],
        scratch_shapes=[pltpu.VMEM((nk // PEER_TOPK, PEER_TOPK, r, LANES), F32),
                        pltpu.VMEM((nk // PEER_TOPK, PEER_TOPK, r, LANES), I32)],
        compiler_params=_cparams(("arbitrary", "arbitrary")),
        name="select",
    )(s4)


def _peer_kernel(h_ref, sh_ref, sc_ref, g2_ref, ng_ref, fg_ref, eidx_ref, gate_ref, uv_ref,
                 out_ref, n2_ref, act_ref, wv_ref, w3_ref, acc_ref, *, ns):
    nk = PEER_NKEYS
    half = nk // 2
    hi_mask = jnp.uint32(0xFFFF0000)
    tm = h_ref.shape[0]
    eb = uv_ref.shape[0]
    nblk = eb // nk
    nblk2 = nblk // 2
    s = pl.program_id(1)

    @pl.when(s == 0)
    def _():
        n2_ref[...] = _rms_mod(h_ref[...], ng_ref[...], sh_ref[0], sc_ref[0]).astype(BF16)
        act_ref[...] = jnp.zeros_like(act_ref)

    @pl.when(s < ns)
    def _():
        e = eidx_ref[...]
        ai = jnp.right_shift(e, nk.bit_length() - 1)
        bi = jnp.bitwise_and(e, nk - 1)
        act = act_ref[...]
        n2 = n2_ref[...]
        for a in range(0, nblk, 2):
            sc = lax.dot_general(n2, uv_ref[a * nk:(a + 2) * nk, :], (((1,), (1,)), ((), ())),
                                 preferred_element_type=F32)
            for j in range(2):
                got = jnp.take_along_axis(sc[:, j * nk:(j + 1) * nk], bi, axis=1)
                act = jnp.where(ai == s * nblk + a + j, got, act)
        act_ref[...] = act

    @pl.when(s == ns - 1)
    def _():
        act = act_ref[...]
        gelu = 0.5 * act * (1.0 + lax.erf(act * (2.0 ** -0.5)))
        wv_ref[...] = gate_ref[...] * gelu
        sub = lax.broadcasted_iota(I32, (nk, LANES), 0).astype(F32).astype(BF16)
        zero = jnp.zeros((nk, LANES), BF16)
        one = jnp.ones((nk, LANES), BF16)

        def tok(t, carry):
            e = eidx_ref[pl.ds(t, 1), :]
            w = wv_ref[pl.ds(t, 1), :].astype(BF16)
            ar = jnp.right_shift(e, nk.bit_length() - 1).astype(F32).astype(BF16)
            br = jnp.bitwise_and(e, nk - 1).astype(F32).astype(BF16)
            pt = jnp.where(sub == ar, w, zero)
            qt = jnp.where(sub == br, one, zero)
            wt = lax.dot_general(pt, qt, (((1,), (1,)), ((), ())), preferred_element_type=F32)
            bits = lax.bitcast_convert_type(wt, jnp.uint32)
            word = jnp.right_shift(bits[:half], 16) | (bits[half:] & hi_mask)
            w3_ref[pl.ds(pl.multiple_of(t * W3_PITCH, 8), half), :] = word
            return carry
        lax.fori_loop(0, tm, tok, 0, unroll=32)

    @pl.when(s >= ns)
    def _():
        base = (s - ns) * nblk2
        words = [w3_ref[pl.ds(base + a, tm, stride=W3_PITCH), :] for a in range(nblk2)]
        lo = [lax.bitcast_convert_type(jnp.left_shift(w, 16), F32).astype(BF16) for w in words]
        hi = [lax.bitcast_convert_type(w & hi_mask, F32).astype(BF16) for w in words]
        part = jnp.dot(jnp.concatenate(lo + hi, axis=1), uv_ref[...], preferred_element_type=F32)

        @pl.when(s == ns)
        def _():
            acc_ref[...] = part

        @pl.when(s > ns)
        def _():
            acc_ref[...] += part

    @pl.when(s == 2 * ns - 1)
    def _():
        hh = h_ref[...] + g2_ref[0] * acc_ref[...]
        ms = jnp.mean(hh * hh, axis=-1, keepdims=True)
        out_ref[...] = hh * lax.rsqrt(ms + EPS) * fg_ref[...]


def _peer(h, sh2, sc2, g2, ng2, fg, eidx_t, gate_t, u, v, seq):
    t, d = h.shape
    ne = u.shape[0]
    tm = min(512, seq)
    eb = 4096
    ns = ne // eb
    nj = eidx_t.shape[1]
    v_blocks = v.reshape(2, ns, eb // 2, d).transpose(1, 0, 2, 3).reshape(ne, d)
    uv = jnp.concatenate([u, v_blocks], axis=0).astype(BF16)
    bidx = lambda i, s: ((i * tm) // seq, 0, 0)
    row = lambda i, s: (i, 0)
    const2 = lambda i, s: (0, 0)
    return pl.pallas_call(
        functools.partial(_peer_kernel, ns=ns),
        grid=(t // tm, 2 * ns),
        in_specs=[pl.BlockSpec((tm, d), row),
                  pl.BlockSpec((1, 1, d), bidx), pl.BlockSpec((1, 1, d), bidx),
                  pl.BlockSpec((1, 1, d), bidx),
                  pl.BlockSpec((1, d), const2), pl.BlockSpec((1, d), const2),
                  pl.BlockSpec((tm, nj), row), pl.BlockSpec((tm, nj), row),
                  pl.BlockSpec((eb, d), lambda i, s: (s, 0))],
        out_specs=pl.BlockSpec((tm, d), row),
        out_shape=jax.ShapeDtypeStruct((t, d), F32),
        scratch_shapes=[pltpu.VMEM((tm, d), BF16),
                        pltpu.VMEM((tm, nj), F32),
                        pltpu.VMEM((tm, nj), F32),
                        pltpu.VMEM((tm * W3_PITCH, PEER_NKEYS), jnp.uint32),
                        pltpu.VMEM((tm, d), F32)],
        compiler_params=_cparams(("arbitrary", "arbitrary")),
        name="peer",
    )(h, sh2, sc2, g2, ng2, fg, eidx_t, gate_t, uv)


def _layer(x2, c, bsz, seq, w_ada, b_ada, norm1_g, w_in, conv_w, conv_b, dt_bias_f, dt_bias_b,
           a_log_f, a_log_b, d_skip, ssd_norm_g, w_ssd_br, na_rpb, w_na_br, w_out, norm2_g,
           peer_wq, peer_keys, peer_u, peer_v, out_g):
    t, d = x2.shape
    d_ssd = SSD_HEADS * SSD_HEADDIM
    d_xbc = d_ssd + 2 * SSD_GROUPS * SSD_STATE
    d_na = NA_HEADS * NA_HEADDIM
    assert seq % SSD_CHUNK == 0 and seq % (GRID_W * NA_QROWS) == 0
    assert seq // GRID_W >= NA_KROWS and d == d_ssd == d_na

    mod = _ada(c, w_ada, b_ada)
    sh1, sc1, g1, sh2, sc2, g2 = [m.reshape(bsz, 1, d) for m in jnp.split(mod, 6, axis=-1)]

    o = np.cumsum([0, d_ssd, d_xbc, SSD_HEADS, SSD_HEADS, 3 * d_na, d, d])
    w_main = jnp.concatenate([w_in[:, o[1]:o[2]], w_in[:, o[0]:o[1]], w_in[:, o[4]:o[7]]],
                             axis=1).astype(BF16)
    w_dt = jnp.pad(w_in[:, o[2]:o[4]], ((0, 0), (0, LANES - 2 * SSD_HEADS))).astype(BF16)
    z_col = d_xbc // d_ssd
    q_col = (d_xbc + d_ssd) // LANES
    gs_col = (d_xbc + d_ssd + 3 * d_na) // d
    proj, dt_raw = _inproj(x2, sh1, sc1, norm1_g.reshape(1, d), w_main, w_dt, seq)

    heads = np.arange(d_ssd) // SSD_HEADDIM

    def e2(off):
        m = np.zeros((LANES, d_ssd), np.float32)
        m[off + heads, np.arange(d_ssd)] = 1.0
        m[off + 64 + heads, np.arange(d_ssd)] = 1.0
        return jnp.asarray(m, BF16)

    def lanes16(vec, off):
        return jnp.zeros((1, LANES), F32).at[0, off:off + SSD_HEADS].set(vec.astype(F32))

    params = {
        "cw": jnp.pad(conv_w.astype(F32), ((0, 8 - CONV_W), (0, 0))),
        "cb": conv_b.astype(F32).reshape(1, d_xbc),
        "dtb_f": lanes16(dt_bias_f, 0), "dtb_b": lanes16(dt_bias_b, SSD_HEADS),
        "a_f": lanes16(-jnp.exp(a_log_f.astype(F32)), 0),
        "a_b": lanes16(-jnp.exp(a_log_b.astype(F32)), SSD_HEADS),
        "e2_f": e2(0), "e2_b": e2(SSD_HEADS),
        "dsk": jnp.repeat(d_skip.astype(F32), SSD_HEADDIM).reshape(1, d_ssd),
        "ng": ssd_norm_g.astype(F32).reshape(1, d_ssd),
    }
    fwd = _ssd(proj, dt_raw, z_col, None, params, bsz, seq, rev=False)
    ynorm = _ssd(proj, dt_raw, z_col, fwd, params, bsz, seq, rev=True)

    bias, var_of_blk, ks_of_blk = _na_bias(na_rpb, seq // GRID_W)
    attn = _na(proj, bias, var_of_blk, ks_of_blk, q_col, bsz, seq)

    h, scores = _mix(x2, ynorm, attn, proj, gs_col, g1, sh2, sc2, norm2_g.reshape(1, d),
                     w_ssd_br.astype(BF16), w_na_br.astype(BF16), w_out.astype(BF16),
                     peer_wq.astype(BF16), peer_keys.astype(BF16), seq)

    eidx, gate = _select(scores.reshape(2 * PEER_HEADS, PEER_NKEYS, t // LANES, LANES))
    nj = PEER_HEADS * PEER_TOPK
    eidx_t = eidx.reshape(nj, t).T
    gate_t = gate.reshape(nj, t).T
    return _peer(h, sh2, sc2, g2, norm2_g.reshape(1, d), out_g, eidx_t, gate_t,
                 peer_u.astype(BF16), peer_v.astype(BF16), seq)


def kernel(x, c, w_ada, b_ada, norm1_g, w_in, conv_w, conv_b, dt_bias_f, dt_bias_b, a_log_f, a_log_b,
           d_skip, ssd_norm_g, w_ssd_br, na_rpb, w_na_br, w_out, norm2_g, peer_wq, peer_keys, peer_u,
           peer_v, final_g):
    bsz, seq, d = x.shape
    depth = w_ada.shape[0]
    assert depth == 1, "the final RMSNorm is fused into the last layer's PEER kernel"
    i = 0
    out = _layer(x.reshape(bsz * seq, d), c, bsz, seq, w_ada[i], b_ada[i], norm1_g[i], w_in[i],
                 conv_w[i], conv_b[i], dt_bias_f[i], dt_bias_b[i], a_log_f[i], a_log_b[i], d_skip[i],
                 ssd_norm_g[i], w_ssd_br[i], na_rpb[i], w_na_br[i], w_out[i], norm2_g[i],
                 peer_wq[i], peer_keys[i], peer_u[i], peer_v[i], final_g.reshape(1, d))
    return out.reshape(bsz, seq, d)
```

```python
import functools

import numpy as np
import jax
import jax.numpy as jnp
from jax import lax
from jax.experimental import pallas as pl
from jax.experimental.pallas import tpu as pltpu

F32 = jnp.float32
BF16 = jnp.bfloat16
I32 = jnp.int32

EPS = 1e-6
GRID_W = 64
SSD_HEADS = 16
SSD_HEADDIM = 64
SSD_GROUPS = 4
SSD_STATE = 128
SSD_CHUNK = 128
CONV_W = 5
NA_HEADS = 16
NA_HEADDIM = 64
NA_KH = 8
NA_KW = 16
NA_QROWS = 4
NA_KROWS = 12
NA_QB = 16
NA_KB = NA_QB + NA_KW
PEER_HEADS = 8
PEER_NKEYS = 128
PEER_TOPK = 16
PEER_DHALF = 128
W3_PITCH = PEER_NKEYS // 2 + 8

LANES = 128
HALO = 16
VMEM_LIMIT = 56 * 1024 * 1024


def _cparams(sem):
    return pltpu.CompilerParams(dimension_semantics=sem, vmem_limit_bytes=VMEM_LIMIT)


def _sigmoid(x):
    return 1.0 / (1.0 + jnp.exp(-x))


def _rms_mod(x, g, shift, scale):
    ms = jnp.mean(x * x, axis=-1, keepdims=True)
    y = x * lax.rsqrt(ms + EPS) * g
    return y * (1.0 + scale) + shift


def _ada_kernel(c_ref, w_ref, b_ref, o_ref):
    c = c_ref[...]
    sc = c * _sigmoid(c)
    o_ref[...] = jnp.dot(sc, w_ref[...], preferred_element_type=F32,
                         precision=lax.Precision.HIGHEST) + b_ref[...]


def _ada(c, w, b):
    bsz, d = c.shape
    n = w.shape[1]
    tn = 1024
    return pl.pallas_call(
        _ada_kernel,
        grid=(n // tn,),
        in_specs=[pl.BlockSpec((bsz, d), lambda j: (0, 0)),
                  pl.BlockSpec((d, tn), lambda j: (0, j)),
                  pl.BlockSpec((1, tn), lambda j: (0, j))],
        out_specs=pl.BlockSpec((bsz, tn), lambda j: (0, j)),
        out_shape=jax.ShapeDtypeStruct((bsz, n), F32),
        compiler_params=_cparams(("arbitrary",)),
        name="ada",
    )(c, w, b.reshape(1, n))


def _inproj_kernel(x_ref, sh_ref, sc_ref, g_ref, w_ref, wdt_ref, proj_ref, dt_ref):
    n1 = _rms_mod(x_ref[...], g_ref[...], sh_ref[0], sc_ref[0]).astype(BF16)
    dt_ref[...] = jnp.dot(n1, wdt_ref[...], preferred_element_type=F32)
    tn = 1024
    for j in range(w_ref.shape[1] // tn):
        proj_ref[:, j * tn:(j + 1) * tn] = jnp.dot(
            n1, w_ref[:, j * tn:(j + 1) * tn], preferred_element_type=F32).astype(BF16)


def _inproj(x2, sh, sc, g, w, wdt, seq):
    t, d = x2.shape
    n = w.shape[1]
    tm = min(512, seq)
    bidx = lambda i: ((i * tm) // seq, 0, 0)
    const2 = lambda i: (0, 0)
    return pl.pallas_call(
        _inproj_kernel,
        grid=(t // tm,),
        in_specs=[pl.BlockSpec((tm, d), lambda i: (i, 0)),
                  pl.BlockSpec((1, 1, d), bidx),
                  pl.BlockSpec((1, 1, d), bidx),
                  pl.BlockSpec((1, d), const2),
                  pl.BlockSpec((d, n), const2, pipeline_mode=pl.Buffered(1)),
                  pl.BlockSpec((d, LANES), const2, pipeline_mode=pl.Buffered(1))],
        out_specs=[pl.BlockSpec((tm, n), lambda i: (i, 0)),
                   pl.BlockSpec((tm, LANES), lambda i: (i, 0))],
        out_shape=[jax.ShapeDtypeStruct((t, n), BF16),
                   jax.ShapeDtypeStruct((t, LANES), F32)],
        compiler_params=_cparams(("arbitrary",)),
        name="inproj",
    )(x2, sh, sc, g, w, wdt)


def _expand_heads(v, e2_ref):
    hi = v.astype(BF16).astype(F32)
    comb = (hi + pltpu.roll(v - hi, 64, 1)).astype(BF16)
    return jnp.dot(comb, e2_ref[...], preferred_element_type=F32)


def _ssd_kernel(*refs, rev):
    if rev:
        (xs_ref, bc_ref, dt_ref, z_ref, yf_ref, dtb_ref, a_ref, e2_ref, dsk_ref, ng_ref,
         out_ref, st_ref) = refs
    else:
        (xm_ref, xp_ref, xn_ref, dt_ref, cw_ref, cb_ref, dtb_ref, a_ref, e2_ref,
         out_ref, xs_out_ref, bc_out_ref, st_ref) = refs
    ck = SSD_CHUNK
    d_ssd = SSD_HEADS * SSD_HEADDIM
    d_bc = SSD_GROUPS * SSD_STATE
    gw = d_ssd // SSD_GROUPS
    c = pl.program_id(1)
    nc = pl.num_programs(1)
    ce = nc - 1 - c if rev else c

    @pl.when(c == 0)
    def _():
        st_ref[...] = jnp.zeros_like(st_ref)

    if rev:
        xs = xs_ref[...]
        bc = bc_ref[...]
    else:
        main = xm_ref[...]
        zero_halo = jnp.zeros((HALO, main.shape[1]), main.dtype)
        ext = jnp.concatenate([jnp.where(ce == 0, zero_halo, xp_ref[...]), main,
                               jnp.where(ce == nc - 1, zero_halo, xn_ref[...])], axis=0)
        erow = lax.broadcasted_iota(I32, (ck, ck + 2 * HALO), 0)
        ecol = lax.broadcasted_iota(I32, (ck, ck + 2 * HALO), 1)
        mid = CONV_W // 2
        acc = cb_ref[...] + main.astype(F32) * cw_ref[mid:mid + 1, :]
        for w in range(CONV_W):
            if w != mid:
                shift = jnp.where(ecol == erow + (HALO + w - mid), 1.0, 0.0).astype(BF16)
                acc = acc + jnp.dot(shift, ext, preferred_element_type=F32) * cw_ref[w:w + 1, :]
        xbc = acc * _sigmoid(acc)
        xs = xbc[:, :d_ssd]
        bc = xbc[:, d_ssd:].astype(BF16)
        xs_out_ref[...] = xs
        bc_out_ref[...] = bc

    lane = lax.broadcasted_iota(I32, (ck, LANES), 1)
    row = lax.broadcasted_iota(I32, (ck, LANES), 0)
    off = SSD_HEADS if rev else 0
    valid = (lane >= off) & (lane < off + SSD_HEADS)
    dtr = dt_ref[...] + dtb_ref[...]
    dt = jnp.where(valid, jnp.maximum(dtr, 0.0) + jnp.log1p(jnp.exp(-jnp.abs(dtr))), 0.0)
    da = dt * a_ref[...]
    tri = (lane >= row) if rev else (lane <= row)
    trib = jnp.where(tri, 1.0, 0.0).astype(BF16)
    hi = da.astype(BF16)
    r1 = da - hi.astype(F32)
    mid = r1.astype(BF16)
    lo = (r1 - mid.astype(F32)).astype(BF16)
    cs = (jnp.dot(trib, hi, preferred_element_type=F32)
          + jnp.dot(trib, mid, preferred_element_type=F32)
          + jnp.dot(trib, lo, preferred_element_type=F32))
    cst = cs.T
    tot_row = 0 if rev else ck - 1
    tot = cs[tot_row:tot_row + 1, :]
    dec = jnp.where(valid, jnp.exp(tot - cs), 0.0)
    ecs = jnp.where(valid, jnp.exp(cs), 0.0)
    ecs_e = _expand_heads(ecs, e2_ref)
    xdt = xs * _expand_heads(dt, e2_ref)
    xdec = (xs * _expand_heads(dt * dec, e2_ref)).astype(BF16)

    lo_half = lax.broadcasted_iota(I32, (ck, LANES), 1) < SSD_HEADDIM
    hpg = SSD_HEADS // SSD_GROUPS
    ys = []
    for g in range(SSD_GROUPS):
        bg = bc[:, g * SSD_STATE:(g + 1) * SSD_STATE]
        cg = bc[:, d_bc + g * SSD_STATE:d_bc + (g + 1) * SSD_STATE]
        cbg = lax.dot_general(cg, bg, (((1,), (1,)), ((), ())), preferred_element_type=F32)
        yg = []
        for q in range(hpg // 2):
            ls = []
            for e in range(2):
                col = off + g * hpg + q * 2 + e
                dlt = cs[:, col:col + 1] - cst[col:col + 1, :]
                seg = jnp.exp(jnp.where(tri, dlt, -jnp.inf))
                ls.append((cbg * seg).astype(BF16))
            blk = g * (hpg // 2) + q
            xq = xdt[:, blk * LANES:(blk + 1) * LANES]
            rhs = jnp.concatenate([jnp.where(lo_half, xq, 0.0), jnp.where(lo_half, 0.0, xq)],
                                  axis=0).astype(BF16)
            yg.append(jnp.dot(jnp.concatenate(ls, axis=1), rhs, preferred_element_type=F32))
        y_diag = jnp.concatenate(yg, axis=1)
        st = st_ref[g]
        e_g = ecs_e[:, g * gw:(g + 1) * gw]
        y_off = jnp.dot(cg, st.astype(BF16), preferred_element_type=F32) * e_g
        ys.append(y_diag + y_off)
        upd = jnp.dot(bg.astype(F32).T.astype(BF16), xdec[:, g * gw:(g + 1) * gw],
                      preferred_element_type=F32)
        st_ref[g] = st * e_g[tot_row:tot_row + 1, :] + upd
    y = jnp.concatenate(ys, axis=1)

    if rev:
        y = yf_ref[...] + y + dsk_ref[...] * xs
        z = z_ref[...].astype(F32)
        u = y * (z * _sigmoid(z))
        ms = jnp.mean(u * u, axis=-1, keepdims=True)
        out_ref[...] = (u * lax.rsqrt(ms + EPS) * ng_ref[...]).astype(BF16)
    else:
        out_ref[...] = y


def _ssd(proj, dt_raw, z_col, fwd, params, bsz, seq, rev):
    t = proj.shape[0]
    ck = SSD_CHUNK
    nc = seq // ck
    d_xbc = params["cw"].shape[1]
    d_ssd = SSD_HEADS * SSD_HEADDIM
    d_bc = d_xbc - d_ssd
    per = ck // HALO
    last_halo = t // HALO - 1

    def ce_of(c):
        return nc - 1 - c if rev else c

    main = lambda b, c: (b * nc + ce_of(c), 0)
    prev = lambda b, c: (jnp.maximum((b * nc + ce_of(c)) * per - 1, 0), 0)
    nxt = lambda b, c: (jnp.minimum((b * nc + ce_of(c)) * per + per, last_halo), 0)
    const2 = lambda b, c: (0, 0)
    d = "b" if rev else "f"
    dir_specs = [pl.BlockSpec((1, LANES), const2), pl.BlockSpec((1, LANES), const2),
                 pl.BlockSpec((LANES, d_ssd), const2)]
    dir_args = [params["dtb_" + d], params["a_" + d], params["e2_" + d]]
    if rev:
        yf, xs, bc = fwd
        in_specs = [pl.BlockSpec((ck, d_ssd), main), pl.BlockSpec((ck, d_bc), main),
                    pl.BlockSpec((ck, LANES), main),
                    pl.BlockSpec((ck, d_ssd), lambda b, c: (b * nc + ce_of(c), z_col)),
                    pl.BlockSpec((ck, d_ssd), main)] + dir_specs + [
                        pl.BlockSpec((1, d_ssd), const2), pl.BlockSpec((1, d_ssd), const2)]
        args = [xs, bc, dt_raw, proj, yf] + dir_args + [params["dsk"], params["ng"]]
        out_specs = pl.BlockSpec((ck, d_ssd), main)
        out_shape = jax.ShapeDtypeStruct((t, d_ssd), BF16)
    else:
        in_specs = [pl.BlockSpec((ck, d_xbc), main), pl.BlockSpec((HALO, d_xbc), prev),
                    pl.BlockSpec((HALO, d_xbc), nxt), pl.BlockSpec((ck, LANES), main),
                    pl.BlockSpec((8, d_xbc), const2), pl.BlockSpec((1, d_xbc), const2)] + dir_specs
        args = [proj, proj, proj, dt_raw, params["cw"], params["cb"]] + dir_args
        out_specs = [pl.BlockSpec((ck, d_ssd), main), pl.BlockSpec((ck, d_ssd), main),
                     pl.BlockSpec((ck, d_bc), main)]
        out_shape = [jax.ShapeDtypeStruct((t, d_ssd), F32), jax.ShapeDtypeStruct((t, d_ssd), F32),
                     jax.ShapeDtypeStruct((t, d_bc), BF16)]
    return pl.pallas_call(
        functools.partial(_ssd_kernel, rev=rev),
        grid=(bsz, nc),
        in_specs=in_specs,
        out_specs=out_specs,
        out_shape=out_shape,
        scratch_shapes=[pltpu.VMEM((SSD_GROUPS, SSD_STATE, d_ssd // SSD_GROUPS), F32)],
        compiler_params=_cparams(("arbitrary", "arbitrary")),
        name="ssd_bwd" if rev else "ssd_fwd",
    )(*args)


def _na_plan(rows):
    nblk = rows // NA_QROWS
    variants, var_of_blk, ks_of_blk = [], [], []
    for i in range(nblk):
        ks = int(np.clip(i * NA_QROWS - NA_KH // 2, 0, rows - NA_KROWS))
        drow = np.zeros((NA_QROWS, NA_KROWS), np.int64)
        ok = np.zeros((NA_QROWS, NA_KROWS), bool)
        for dr in range(NA_QROWS):
            r = i * NA_QROWS + dr
            rs = int(np.clip(r - NA_KH // 2, 0, rows - NA_KH))
            for j in range(NA_KH):
                w = rs + j - ks
                drow[dr, w] = rs + j - r + NA_KH - 1
                ok[dr, w] = True
        key = (drow.tobytes(), ok.tobytes())
        for vi, (k2, _, _) in enumerate(variants):
            if k2 == key:
                var_of_blk.append(vi)
                break
        else:
            var_of_blk.append(len(variants))
            variants.append((key, drow, ok))
        ks_of_blk.append(ks)
    return [(d, o) for _, d, o in variants], var_of_blk, ks_of_blk


def _na_blk_start():
    ncb = GRID_W // NA_QB
    return [int(v) for v in np.clip(np.arange(ncb) * NA_QB - NA_KW // 2, 0, GRID_W - NA_KB)]


def _na_bias(rpb, rows):
    variants, var_of_blk, ks_of_blk = _na_plan(rows)
    nh = rpb.shape[0]
    cols = np.arange(GRID_W)
    win_start = np.clip(cols - NA_KW // 2, 0, GRID_W - NA_KW)
    in_win = (cols[None, :] >= win_start[:, None]) & (cols[None, :] < win_start[:, None] + NA_KW)
    dcol = np.clip(cols[None, :] - cols[:, None] + NA_KW - 1, 0, 2 * NA_KW - 2)
    onehot = (dcol[:, :, None] == np.arange(2 * NA_KW - 1)).astype(np.float32)
    band = jnp.einsum("hab,qkb->haqk", rpb.astype(F32), onehot, precision=lax.Precision.HIGHEST)
    band = jnp.where(in_win[None, None], band, -jnp.inf)
    neg = jnp.full((nh, NA_QB, NA_KB), -jnp.inf, F32)
    tabs = []
    for drow, ok in variants:
        per_cb = []
        for m, bs in enumerate(_na_blk_start()):
            qrows = [jnp.concatenate(
                [band[:, drow[dr, w], m * NA_QB:(m + 1) * NA_QB, bs:bs + NA_KB] if ok[dr, w] else neg
                 for w in range(NA_KROWS)], axis=2) for dr in range(NA_QROWS)]
            per_cb.append(jnp.concatenate(qrows, axis=1))
        tabs.append(jnp.stack(per_cb, axis=1))
    tab = jnp.stack(tabs)
    nv, _, ncb, mq, nk = tab.shape
    tab = tab.reshape(nv, nh // 2, 2, ncb, mq, nk).transpose(0, 1, 3, 2, 4, 5)
    return tab.reshape(nv, nh // 2, ncb, 2 * mq, nk), var_of_blk, ks_of_blk


def _na_kernel(var_ref, ks_ref, q_ref, k_ref, v_ref, bias_ref, o_ref, kf_ref, vf_ref):
    mq = NA_QROWS * NA_QB
    nblk = q_ref.shape[0] // (NA_QROWS * GRID_W)
    lo_half = lax.broadcasted_iota(I32, (mq, LANES), 1) < NA_HEADDIM
    kf_ref[...] = k_ref[...].astype(F32)
    vf_ref[...] = v_ref[...].astype(F32)

    def blk(i, carry):
        var = var_ref[i]
        ks = ks_ref[i]
        starts = _na_blk_start()
        q_at = [[pl.multiple_of((i * NA_QROWS + dr) * GRID_W + m * NA_QB, NA_QB)
                 for dr in range(NA_QROWS)] for m in range(len(starts))]
        k_at = [[pl.multiple_of((ks + w) * GRID_W + bs, 8) for w in range(NA_KROWS)]
                for bs in starts]
        scores = []
        for m in range(len(starts)):
            qb = jnp.concatenate([q_ref[pl.ds(o, NA_QB), :] for o in q_at[m]], axis=0)
            qb = qb * (NA_HEADDIM ** -0.5)
            zero = jnp.zeros_like(qb)
            q2 = jnp.concatenate([jnp.where(lo_half, qb, zero), jnp.where(lo_half, zero, qb)], axis=0)
            kb = jnp.concatenate([kf_ref[pl.ds(o, NA_KB), :] for o in k_at[m]], axis=0).astype(BF16)
            s = lax.dot_general(q2, kb, (((1,), (1,)), ((), ())), preferred_element_type=F32)
            scores.append(s + bias_ref[var, 0, m])
        outs = []
        for m, s in enumerate(scores):
            p = jnp.exp(s - jnp.max(s, axis=-1, keepdims=True))
            l = jnp.sum(p, axis=-1, keepdims=True)
            vb = jnp.concatenate([vf_ref[pl.ds(o, NA_KB), :] for o in k_at[m]], axis=0).astype(BF16)
            outs.append(jnp.dot(p.astype(BF16), vb, preferred_element_type=F32) / l)
        for m, o2 in enumerate(outs):
            ob = jnp.where(lo_half, o2[:mq], o2[mq:]).astype(BF16)
            for dr, o in enumerate(q_at[m]):
                o_ref[pl.ds(o, NA_QB), :] = ob[dr * NA_QB:(dr + 1) * NA_QB]
        return carry

    lax.fori_loop(0, nblk, blk, 0, unroll=2)


def _na(proj, bias, var_of_blk, ks_of_blk, q_col, bsz, seq):
    t = proj.shape[0]
    npair = NA_HEADS // 2
    grid_spec = pltpu.PrefetchScalarGridSpec(
        num_scalar_prefetch=2,
        grid=(bsz, npair),
        in_specs=[pl.BlockSpec((seq, LANES), lambda b, p, *_: (b, q_col + p)),
                  pl.BlockSpec((seq, LANES), lambda b, p, *_: (b, q_col + npair + p)),
                  pl.BlockSpec((seq, LANES), lambda b, p, *_: (b, q_col + 2 * npair + p)),
                  pl.BlockSpec((bias.shape[0], 1) + bias.shape[2:], lambda b, p, *_: (0, p, 0, 0, 0))],
        out_specs=pl.BlockSpec((seq, LANES), lambda b, p, *_: (b, p)),
        scratch_shapes=[pltpu.VMEM((seq, LANES), F32), pltpu.VMEM((seq, LANES), F32)],
    )
    return pl.pallas_call(
        _na_kernel,
        grid_spec=grid_spec,
        out_shape=jax.ShapeDtypeStruct((t, NA_HEADS * NA_HEADDIM), BF16),
        compiler_params=_cparams(("arbitrary", "arbitrary")),
        name="na",
    )(jnp.asarray(var_of_blk, I32), jnp.asarray(ks_of_blk, I32), proj, proj, proj, bias)


def _mix_kernel(x_ref, yn_ref, at_ref, gs_ref, gn_ref, g1_ref, sh_ref, sc_ref, ng_ref,
                ws_ref, wn_ref, wo_ref, wq_ref, keys_ref, h_ref, s_ref):
    y1 = jnp.dot(yn_ref[...], ws_ref[...], preferred_element_type=F32)
    y2 = jnp.dot(at_ref[...], wn_ref[...], preferred_element_type=F32)
    mixed = _sigmoid(gs_ref[...].astype(F32)) * y1 + _sigmoid(gn_ref[...].astype(F32)) * y2
    h = x_ref[...] + g1_ref[0] * jnp.dot(mixed.astype(BF16), wo_ref[...], preferred_element_type=F32)
    h_ref[...] = h
    n2 = _rms_mod(h, ng_ref[...], sh_ref[0], sc_ref[0]).astype(BF16)
    q = jnp.dot(n2, wq_ref[...], preferred_element_type=F32).astype(BF16)
    for hz in range(2 * PEER_HEADS):
        qs = q[:, hz * PEER_DHALF:(hz + 1) * PEER_DHALF]
        s_ref[hz] = lax.dot_general(keys_ref[hz % 2], qs, (((1,), (1,)), ((), ())),
                                    preferred_element_type=F32)


def _mix(x2, ynorm, attn, proj, gs_col, g1, sh2, sc2, ng2, ws, wn, wo, wq, keys, seq):
    t, d = x2.shape
    tm = min(512, seq)
    bidx = lambda i: ((i * tm) // seq, 0, 0)
    row = lambda i: (i, 0)
    const2 = lambda i: (0, 0)
    return pl.pallas_call(
        _mix_kernel,
        grid=(t // tm,),
        in_specs=[pl.BlockSpec((tm, d), row), pl.BlockSpec((tm, d), row), pl.BlockSpec((tm, d), row),
                  pl.BlockSpec((tm, d), lambda i: (i, gs_col)),
                  pl.BlockSpec((tm, d), lambda i: (i, gs_col + 1)),
                  pl.BlockSpec((1, 1, d), bidx), pl.BlockSpec((1, 1, d), bidx),
                  pl.BlockSpec((1, 1, d), bidx), pl.BlockSpec((1, d), const2),
                  pl.BlockSpec((d, d), const2, pipeline_mode=pl.Buffered(1)),
                  pl.BlockSpec((d, d), const2, pipeline_mode=pl.Buffered(1)),
                  pl.BlockSpec((d, d), const2, pipeline_mode=pl.Buffered(1)),
                  pl.BlockSpec(wq.shape, const2, pipeline_mode=pl.Buffered(1)),
                  pl.BlockSpec(keys.shape, lambda i: (0, 0, 0), pipeline_mode=pl.Buffered(1))],
        out_specs=[pl.BlockSpec((tm, d), row),
                   pl.BlockSpec((2 * PEER_HEADS, PEER_NKEYS, tm), lambda i: (0, 0, i))],
        out_shape=[jax.ShapeDtypeStruct((t, d), F32),
                   jax.ShapeDtypeStruct((2 * PEER_HEADS, PEER_NKEYS, t), F32)],
        compiler_params=_cparams(("arbitrary",)),
        name="mix",
    )(x2, ynorm, attn, proj, proj, g1, sh2, sc2, ng2, ws, wn, wo, wq, keys)


def _hyperbola():
    return [(i, k) for i in range(PEER_TOPK) for k in range(PEER_TOPK)
            if (i + 1) * (k + 1) <= PEER_TOPK]


def _sort_network(n):
    pairs = []
    p = 1
    while p < n:
        k = p
        while k >= 1:
            for j in range(k % p, n - k, 2 * k):
                for i in range(min(k, n - j - k)):
                    if (i + j) // (2 * p) == (i + j + k) // (2 * p):
                        pairs.append((i + j, i + j + k))
            k //= 2
        p *= 2
    return pairs


def _precedes(va, pa, vb, pb):
    return (va > vb) | ((va == vb) & (pa < pb))


def _compare_exchange(v, p, i, j):
    c = _precedes(v[i], p[i], v[j], p[j])
    v[i], v[j] = jnp.where(c, v[i], v[j]), jnp.where(c, v[j], v[i])
    p[i], p[j] = jnp.where(c, p[i], p[j]), jnp.where(c, p[j], p[i])


def _top16(get, n, sv_ref, sp_ref):
    k = PEER_TOPK
    ng = n // k
    net = _sort_network(k)
    for g in range(ng):
        items = [get(g * k + j) for j in range(k)]
        v = [it[0] for it in items]
        p = [it[1] for it in items]
        for i, j in net:
            _compare_exchange(v, p, i, j)
        if ng == 1:
            return v, p
        for j in range(k):
            sv_ref[g, j] = v[j]
            sp_ref[g, j] = p[j]
    step = 1
    while True:
        for g in range(0, ng, 2 * step):
            v, p = [], []
            for j in range(k):
                xv, xp = sv_ref[g, j], sp_ref[g, j]
                yv, yp = sv_ref[g + step, k - 1 - j], sp_ref[g + step, k - 1 - j]
                c = _precedes(xv, xp, yv, yp)
                v.append(jnp.where(c, xv, yv))
                p.append(jnp.where(c, xp, yp))
            stride = k // 2
            while stride >= 1:
                for i in range(k):
                    if i & stride == 0:
                        _compare_exchange(v, p, i, i + stride)
                stride //= 2
            if 2 * step >= ng:
                return v, p
            for j in range(k):
                sv_ref[g, j] = v[j]
                sp_ref[g, j] = p[j]
        step *= 2


def _select_kernel(s_ref, eidx_ref, gate_ref, sv_ref, sp_ref):
    nk = PEER_NKEYS
    k = PEER_TOPK
    shape = s_ref.shape[2:]
    ebits = (nk * nk - 1).bit_length()

    tops = []
    for z in range(2):
        tops.append(_top16(lambda j, z=z: (s_ref[z, j], jnp.full(shape, j, I32)), nk, sv_ref, sp_ref))
    (tv0, ti0), (tv1, ti1) = tops

    pairs = _hyperbola()
    npad = -len(pairs) % k

    def cand(j):
        if j >= len(pairs):
            return jnp.full(shape, -jnp.inf, F32), jnp.full(shape, (k * k) << ebits, I32)
        i, kk = pairs[j]
        return tv0[i] + tv1[kk], ((i * k + kk) << ebits) + ti0[i] * nk + ti1[kk]

    top, ids = _top16(cand, len(pairs) + npad, sv_ref, sp_ref)
    ex = [jnp.exp(t - top[0]) for t in top]
    zsum = ex[0]
    for r in range(1, k):
        zsum = zsum + ex[r]
    for r in range(k):
        eidx_ref[0, r] = ids[r] & ((1 << ebits) - 1)
        gate_ref[0, r] = ex[r] / zsum


def _select(s4):
    nhz, nk, tb, _ = s4.shape
    r = min(8, tb)
    oshape = (PEER_HEADS, PEER_TOPK, tb, LANES)
    return pl.pallas_call(
        _select_kernel,
        grid=(tb // r, PEER_HEADS),
        in_specs=[pl.BlockSpec((2, nk, r, LANES), lambda i, h: (h, 0, i, 0))],
        out_specs=[pl.BlockSpec((1, PEER_TOPK, r, LANES), lambda i, h: (h, 0, i, 0)),
                   pl.BlockSpec((1, PEER_TOPK, r, LANES), lambda i, h: (h, 0, i, 0))],
        out_shape=[jax.ShapeDtypeStruct(oshape, I32), jax.ShapeDtypeStruct(oshape, F32)],
        scratch_shapes=[pltpu.VMEM((nk // PEER_TOPK, PEER_TOPK, r, LANES), F32),
                        pltpu.VMEM((nk // PEER_TOPK, PEER_TOPK, r, LANES), I32)],
        compiler_params=_cparams(("arbitrary", "arbitrary")),
        name="select",
    )(s4)


def _peer_kernel(h_ref, sh_ref, sc_ref, g2_ref, ng_ref, fg_ref, eidx_ref, gate_ref, uv_ref,
                 out_ref, n2_ref, act_ref, wv_ref, w3_ref, acc_ref, *, ns):
    nk = PEER_NKEYS
    half = nk // 2
    hi_mask = jnp.uint32(0xFFFF0000)
    tm = h_ref.shape[0]
    eb = uv_ref.shape[0]
    nblk = eb // nk
    nblk2 = nblk // 2
    s = pl.program_id(1)

    @pl.when(s == 0)
    def _():
        n2_ref[...] = _rms_mod(h_ref[...], ng_ref[...], sh_ref[0], sc_ref[0]).astype(BF16)
        act_ref[...] = jnp.zeros_like(act_ref)

    @pl.when(s < ns)
    def _():
        e = eidx_ref[...]
        ai = jnp.right_shift(e, nk.bit_length() - 1)
        bi = jnp.bitwise_and(e, nk - 1)
        act = act_ref[...]
        n2 = n2_ref[...]
        for a in range(0, nblk, 2):
            sc = lax.dot_general(n2, uv_ref[a * nk:(a + 2) * nk, :], (((1,), (1,)), ((), ())),
                                 preferred_element_type=F32)
            for j in range(2):
                got = jnp.take_along_axis(sc[:, j * nk:(j + 1) * nk], bi, axis=1)
                act = jnp.where(ai == s * nblk + a + j, got, act)
        act_ref[...] = act

    @pl.when(s == ns - 1)
    def _():
        act = act_ref[...]
        gelu = 0.5 * act * (1.0 + lax.erf(act * (2.0 ** -0.5)))
        wv_ref[...] = gate_ref[...] * gelu
        sub = lax.broadcasted_iota(I32, (nk, LANES), 0).astype(F32).astype(BF16)
        zero = jnp.zeros((nk, LANES), BF16)
        one = jnp.ones((nk, LANES), BF16)

        def tok(t, carry):
            e = eidx_ref[pl.ds(t, 1), :]
            w = wv_ref[pl.ds(t, 1), :].astype(BF16)
            ar = jnp.right_shift(e, nk.bit_length() - 1).astype(F32).astype(BF16)
            br = jnp.bitwise_and(e, nk - 1).astype(F32).astype(BF16)
            pt = jnp.where(sub == ar, w, zero)
            qt = jnp.where(sub == br, one, zero)
            wt = lax.dot_general(pt, qt, (((1,), (1,)), ((), ())), preferred_element_type=F32)
            bits = lax.bitcast_convert_type(wt, jnp.uint32)
            word = jnp.right_shift(bits[:half], 16) | (bits[half:] & hi_mask)
            w3_ref[pl.ds(pl.multiple_of(t * W3_PITCH, 8), half), :] = word
            return carry
        lax.fori_loop(0, tm, tok, 0, unroll=32)

    @pl.when(s >= ns)
    def _():
        base = (s - ns) * nblk2
        words = [w3_ref[pl.ds(base + a, tm, stride=W3_PITCH), :] for a in range(nblk2)]
        lo = [lax.bitcast_convert_type(jnp.left_shift(w, 16), F32).astype(BF16) for w in words]
        hi = [lax.bitcast_convert_type(w & hi_mask, F32).astype(BF16) for w in words]
        part = jnp.dot(jnp.concatenate(lo + hi, axis=1), uv_ref[...], preferred_element_type=F32)

        @pl.when(s == ns)
        def _():
            acc_ref[...] = part

        @pl.when(s > ns)
        def _():
            acc_ref[...] += part

    @pl.when(s == 2 * ns - 1)
    def _():
        hh = h_ref[...] + g2_ref[0] * acc_ref[...]
        ms = jnp.mean(hh * hh, axis=-1, keepdims=True)
        out_ref[...] = hh * lax.rsqrt(ms + EPS) * fg_ref[...]


def _peer(h, sh2, sc2, g2, ng2, fg, eidx_t, gate_t, u, v, seq):
    t, d = h.shape
    ne = u.shape[0]
    tm = min(512, seq)
    eb = 4096
    ns = ne // eb
    nj = eidx_t.shape[1]
    v_blocks = v.reshape(2, ns, eb // 2, d).transpose(1, 0, 2, 3).reshape(ne, d)
    uv = jnp.concatenate([u, v_blocks], axis=0).astype(BF16)
    bidx = lambda i, s: ((i * tm) // seq, 0, 0)
    row = lambda i, s: (i, 0)
    const2 = lambda i, s: (0, 0)
    return pl.pallas_call(
        functools.partial(_peer_kernel, ns=ns),
        grid=(t // tm, 2 * ns),
        in_specs=[pl.BlockSpec((tm, d), row),
                  pl.BlockSpec((1, 1, d), bidx), pl.BlockSpec((1, 1, d), bidx),
                  pl.BlockSpec((1, 1, d), bidx),
                  pl.BlockSpec((1, d), const2), pl.BlockSpec((1, d), const2),
                  pl.BlockSpec((tm, nj), row), pl.BlockSpec((tm, nj), row),
                  pl.BlockSpec((eb, d), lambda i, s: (s, 0))],
        out_specs=pl.BlockSpec((tm, d), row),
        out_shape=jax.ShapeDtypeStruct((t, d), F32),
        scratch_shapes=[pltpu.VMEM((tm, d), BF16),
                        pltpu.VMEM((tm, nj), F32),
                        pltpu.VMEM((tm, nj), F32),
                        pltpu.VMEM((tm * W3_PITCH, PEER_NKEYS), jnp.uint32),
                        pltpu.VMEM((tm, d), F32)],
        compiler_params=_cparams(("arbitrary", "arbitrary")),
        name="peer",
    )(h, sh2, sc2, g2, ng2, fg, eidx_t, gate_t, uv)


def _layer(x2, c, bsz, seq, w_ada, b_ada, norm1_g, w_in, conv_w, conv_b, dt_bias_f, dt_bias_b,
           a_log_f, a_log_b, d_skip, ssd_norm_g, w_ssd_br, na_rpb, w_na_br, w_out, norm2_g,
           peer_wq, peer_keys, peer_u, peer_v, out_g):
    t, d = x2.shape
    d_ssd = SSD_HEADS * SSD_HEADDIM
    d_xbc = d_ssd + 2 * SSD_GROUPS * SSD_STATE
    d_na = NA_HEADS * NA_HEADDIM
    assert seq % SSD_CHUNK == 0 and seq % (GRID_W * NA_QROWS) == 0
    assert seq // GRID_W >= NA_KROWS and d == d_ssd == d_na

    mod = _ada(c, w_ada, b_ada)
    sh1, sc1, g1, sh2, sc2, g2 = [m.reshape(bsz, 1, d) for m in jnp.split(mod, 6, axis=-1)]

    o = np.cumsum([0, d_ssd, d_xbc, SSD_HEADS, SSD_HEADS, 3 * d_na, d, d])
    w_main = jnp.concatenate([w_in[:, o[1]:o[2]], w_in[:, o[0]:o[1]], w_in[:, o[4]:o[7]]],
                             axis=1).astype(BF16)
    w_dt = jnp.pad(w_in[:, o[2]:o[4]], ((0, 0), (0, LANES - 2 * SSD_HEADS))).astype(BF16)
    z_col = d_xbc // d_ssd
    q_col = (d_xbc + d_ssd) // LANES
    gs_col = (d_xbc + d_ssd + 3 * d_na) // d
    proj, dt_raw = _inproj(x2, sh1, sc1, norm1_g.reshape(1, d), w_main, w_dt, seq)

    heads = np.arange(d_ssd) // SSD_HEADDIM

    def e2(off):
        m = np.zeros((LANES, d_ssd), np.float32)
        m[off + heads, np.arange(d_ssd)] = 1.0
        m[off + 64 + heads, np.arange(d_ssd)] = 1.0
        return jnp.asarray(m, BF16)

    def lanes16(vec, off):
        return jnp.zeros((1, LANES), F32).at[0, off:off + SSD_HEADS].set(vec.astype(F32))

    params = {
        "cw": jnp.pad(conv_w.astype(F32), ((0, 8 - CONV_W), (0, 0))),
        "cb": conv_b.astype(F32).reshape(1, d_xbc),
        "dtb_f": lanes16(dt_bias_f, 0), "dtb_b": lanes16(dt_bias_b, SSD_HEADS),
        "a_f": lanes16(-jnp.exp(a_log_f.astype(F32)), 0),
        "a_b": lanes16(-jnp.exp(a_log_b.astype(F32)), SSD_HEADS),
        "e2_f": e2(0), "e2_b": e2(SSD_HEADS),
        "dsk": jnp.repeat(d_skip.astype(F32), SSD_HEADDIM).reshape(1, d_ssd),
        "ng": ssd_norm_g.astype(F32).reshape(1, d_ssd),
    }
    fwd = _ssd(proj, dt_raw, z_col, None, params, bsz, seq, rev=False)
    ynorm = _ssd(proj, dt_raw, z_col, fwd, params, bsz, seq, rev=True)

    bias, var_of_blk, ks_of_blk = _na_bias(na_rpb, seq // GRID_W)
    attn = _na(proj, bias, var_of_blk, ks_of_blk, q_col, bsz, seq)

    h, scores = _mix(x2, ynorm, attn, proj, gs_col, g1, sh2, sc2, norm2_g.reshape(1, d),
                     w_ssd_br.astype(BF16), w_na_br.astype(BF16), w_out.astype(BF16),
                     peer_wq.astype(BF16), peer_keys.astype(BF16), seq)

    eidx, gate = _select(scores.reshape(2 * PEER_HEADS, PEER_NKEYS, t // LANES, LANES))
    nj = PEER_HEADS * PEER_TOPK
    eidx_t = eidx.reshape(nj, t).T
    gate_t = gate.reshape(nj, t).T
    return _peer(h, sh2, sc2, g2, norm2_g.reshape(1, d), out_g, eidx_t, gate_t,
                 peer_u.astype(BF16), peer_v.astype(BF16), seq)


def kernel(x, c, w_ada, b_ada, norm1_g, w_in, conv_w, conv_b, dt_bias_f, dt_bias_b, a_log_f, a_log_b,
           d_skip, ssd_norm_g, w_ssd_br, na_rpb, w_na_br, w_out, norm2_g, peer_wq, peer_keys, peer_u,
           peer_v, final_g):
    bsz, seq, d = x.shape
    depth = w_ada.shape[0]
    assert depth == 1, "the final RMSNorm is fused into the last layer's PEER kernel"
    i = 0
    out = _layer(x.reshape(bsz * seq, d), c, bsz, seq, w_ada[i], b_ada[i], norm1_g[i], w_in[i],
                 conv_w[i], conv_b[i], dt_bias_f[i], dt_bias_b[i], a_log_f[i], a_log_b[i], d_skip[i],
                 ssd_norm_g[i], w_ssd_br[i], na_rpb[i], w_na_br[i], w_out[i], norm2_g[i],
                 peer_wq[i], peer_keys[i], peer_u[i], peer_v[i], final_g.reshape(1, d))
    return out.reshape(bsz, seq, d)
```

```python
import functools

import numpy as np
import jax
import jax.numpy as jnp
from jax import lax
from jax.experimental import pallas as pl
from jax.experimental.pallas import tpu as pltpu

F32 = jnp.float32
BF16 = jnp.bfloat16
I32 = jnp.int32

EPS = 1e-6
GRID_W = 64
SSD_HEADS = 16
SSD_HEADDIM = 64
SSD_GROUPS = 4
SSD_STATE = 128
SSD_CHUNK = 128
CONV_W = 5
NA_HEADS = 16
NA_HEADDIM = 64
NA_KH = 8
NA_KW = 16
NA_QROWS = 4
NA_KROWS = 12
NA_QB = 16
NA_KB = NA_QB + NA_KW
PEER_HEADS = 8
PEER_NKEYS = 128
PEER_TOPK = 16
PEER_DHALF = 128
W3_PITCH = PEER_NKEYS // 2 + 8

LANES = 128
HALO = 16
VMEM_LIMIT = 56 * 1024 * 1024


def _cparams(sem):
    return pltpu.CompilerParams(dimension_semantics=sem, vmem_limit_bytes=VMEM_LIMIT)


def _sigmoid(x):
    return 1.0 / (1.0 + jnp.exp(-x))


def _rms_mod(x, g, shift, scale):
    ms = jnp.mean(x * x, axis=-1, keepdims=True)
    y = x * lax.rsqrt(ms + EPS) * g
    return y * (1.0 + scale) + shift


def _ada_kernel(c_ref, w_ref, b_ref, o_ref):
    c = c_ref[...]
    sc = c * _sigmoid(c)
    o_ref[...] = jnp.dot(sc, w_ref[...], preferred_element_type=F32,
                         precision=lax.Precision.HIGHEST) + b_ref[...]


def _ada(c, w, b):
    bsz, d = c.shape
    n = w.shape[1]
    tn = 1024
    return pl.pallas_call(
        _ada_kernel,
        grid=(n // tn,),
        in_specs=[pl.BlockSpec((bsz, d), lambda j: (0, 0)),
                  pl.BlockSpec((d, tn), lambda j: (0, j)),
                  pl.BlockSpec((1, tn), lambda j: (0, j))],
        out_specs=pl.BlockSpec((bsz, tn), lambda j: (0, j)),
        out_shape=jax.ShapeDtypeStruct((bsz, n), F32),
        compiler_params=_cparams(("arbitrary",)),
        name="ada",
    )(c, w, b.reshape(1, n))


def _inproj_kernel(x_ref, sh_ref, sc_ref, g_ref, w_ref, wdt_ref, proj_ref, dt_ref):
    n1 = _rms_mod(x_ref[...], g_ref[...], sh_ref[0], sc_ref[0]).astype(BF16)
    dt_ref[...] = jnp.dot(n1, wdt_ref[...], preferred_element_type=F32)
    tn = 1024
    for j in range(w_ref.shape[1] // tn):
        proj_ref[:, j * tn:(j + 1) * tn] = jnp.dot(
            n1, w_ref[:, j * tn:(j + 1) * tn], preferred_element_type=F32).astype(BF16)


def _inproj(x2, sh, sc, g, w, wdt, seq):
    t, d = x2.shape
    n = w.shape[1]
    tm = min(512, seq)
    bidx = lambda i: ((i * tm) // seq, 0, 0)
    const2 = lambda i: (0, 0)
    return pl.pallas_call(
        _inproj_kernel,
        grid=(t // tm,),
        in_specs=[pl.BlockSpec((tm, d), lambda i: (i, 0)),
                  pl.BlockSpec((1, 1, d), bidx),
                  pl.BlockSpec((1, 1, d), bidx),
                  pl.BlockSpec((1, d), const2),
                  pl.BlockSpec((d, n), const2, pipeline_mode=pl.Buffered(1)),
                  pl.BlockSpec((d, LANES), const2, pipeline_mode=pl.Buffered(1))],
        out_specs=[pl.BlockSpec((tm, n), lambda i: (i, 0)),
                   pl.BlockSpec((tm, LANES), lambda i: (i, 0))],
        out_shape=[jax.ShapeDtypeStruct((t, n), BF16),
                   jax.ShapeDtypeStruct((t, LANES), F32)],
        compiler_params=_cparams(("arbitrary",)),
        name="inproj",
    )(x2, sh, sc, g, w, wdt)


def _expand_heads(v, e2_ref):
    hi = v.astype(BF16).astype(F32)
    comb = (hi + pltpu.roll(v - hi, 64, 1)).astype(BF16)
    return jnp.dot(comb, e2_ref[...], preferred_element_type=F32)


def _ssd_kernel(*refs, rev):
    if rev:
        (xs_ref, bc_ref, dt_ref, z_ref, yf_ref, dtb_ref, a_ref, e2_ref, dsk_ref, ng_ref,
         out_ref, st_ref) = refs
    else:
        (xm_ref, xp_ref, xn_ref, dt_ref, cw_ref, cb_ref, dtb_ref, a_ref, e2_ref,
         out_ref, xs_out_ref, bc_out_ref, st_ref) = refs
    ck = SSD_CHUNK
    d_ssd = SSD_HEADS * SSD_HEADDIM
    d_bc = SSD_GROUPS * SSD_STATE
    gw = d_ssd // SSD_GROUPS
    c = pl.program_id(1)
    nc = pl.num_programs(1)
    ce = nc - 1 - c if rev else c

    @pl.when(c == 0)
    def _():
        st_ref[...] = jnp.zeros_like(st_ref)

    if rev:
        xs = xs_ref[...]
        bc = bc_ref[...]
    else:
        main = xm_ref[...]
        zero_halo = jnp.zeros((HALO, main.shape[1]), main.dtype)
        ext = jnp.concatenate([jnp.where(ce == 0, zero_halo, xp_ref[...]), main,
                               jnp.where(ce == nc - 1, zero_halo, xn_ref[...])], axis=0)
        erow = lax.broadcasted_iota(I32, (ck, ck + 2 * HALO), 0)
        ecol = lax.broadcasted_iota(I32, (ck, ck + 2 * HALO), 1)
        mid = CONV_W // 2
        acc = cb_ref[...] + main.astype(F32) * cw_ref[mid:mid + 1, :]
        for w in range(CONV_W):
            if w != mid:
                shift = jnp.where(ecol == erow + (HALO + w - mid), 1.0, 0.0).astype(BF16)
                acc = acc + jnp.dot(shift, ext, preferred_element_type=F32) * cw_ref[w:w + 1, :]
        xbc = acc * _sigmoid(acc)
        xs = xbc[:, :d_ssd]
        bc = xbc[:, d_ssd:].astype(BF16)
        xs_out_ref[...] = xs
        bc_out_ref[...] = bc

    lane = lax.broadcasted_iota(I32, (ck, LANES), 1)
    row = lax.broadcasted_iota(I32, (ck, LANES), 0)
    off = SSD_HEADS if rev else 0
    valid = (lane >= off) & (lane < off + SSD_HEADS)
    dtr = dt_ref[...] + dtb_ref[...]
    dt = jnp.where(valid, jnp.maximum(dtr, 0.0) + jnp.log1p(jnp.exp(-jnp.abs(dtr))), 0.0)
    da = dt * a_ref[...]
    tri = (lane >= row) if rev else (lane <= row)
    trib = jnp.where(tri, 1.0, 0.0).astype(BF16)
    hi = da.astype(BF16)
    r1 = da - hi.astype(F32)
    mid = r1.astype(BF16)
    lo = (r1 - mid.astype(F32)).astype(BF16)
    cs = (jnp.dot(trib, hi, preferred_element_type=F32)
          + jnp.dot(trib, mid, preferred_element_type=F32)
          + jnp.dot(trib, lo, preferred_element_type=F32))
    cst = cs.T
    tot_row = 0 if rev else ck - 1
    tot = cs[tot_row:tot_row + 1, :]
    dec = jnp.where(valid, jnp.exp(tot - cs), 0.0)
    ecs = jnp.where(valid, jnp.exp(cs), 0.0)
    ecs_e = _expand_heads(ecs, e2_ref)
    xdt = xs * _expand_heads(dt, e2_ref)
    xdec = (xs * _expand_heads(dt * dec, e2_ref)).astype(BF16)

    lo_half = lax.broadcasted_iota(I32, (ck, LANES), 1) < SSD_HEADDIM
    hpg = SSD_HEADS // SSD_GROUPS
    ys = []
    for g in range(SSD_GROUPS):
        bg = bc[:, g * SSD_STATE:(g + 1) * SSD_STATE]
        cg = bc[:, d_bc + g * SSD_STATE:d_bc + (g + 1) * SSD_STATE]
        cbg = lax.dot_general(cg, bg, (((1,), (1,)), ((), ())), preferred_element_type=F32)
        yg = []
        for q in range(hpg // 2):
            ls = []
            for e in range(2):
                col = off + g * hpg + q * 2 + e
                dlt = cs[:, col:col + 1] - cst[col:col + 1, :]
                seg = jnp.exp(jnp.where(tri, dlt, -jnp.inf))
                ls.append((cbg * seg).astype(BF16))
            blk = g * (hpg // 2) + q
            xq = xdt[:, blk * LANES:(blk + 1) * LANES]
            rhs = jnp.concatenate([jnp.where(lo_half, xq, 0.0), jnp.where(lo_half, 0.0, xq)],
                                  axis=0).astype(BF16)
            yg.append(jnp.dot(jnp.concatenate(ls, axis=1), rhs, preferred_element_type=F32))
        y_diag = jnp.concatenate(yg, axis=1)
        st = st_ref[g]
        e_g = ecs_e[:, g * gw:(g + 1) * gw]
        y_off = jnp.dot(cg, st.astype(BF16), preferred_element_type=F32) * e_g
        ys.append(y_diag + y_off)
        upd = jnp.dot(bg.astype(F32).T.astype(BF16), xdec[:, g * gw:(g + 1) * gw],
                      preferred_element_type=F32)
        st_ref[g] = st * e_g[tot_row:tot_row + 1, :] + upd
    y = jnp.concatenate(ys, axis=1)

    if rev:
        y = yf_ref[...] + y + dsk_ref[...] * xs
        z = z_ref[...].astype(F32)
        u = y * (z * _sigmoid(z))
        ms = jnp.mean(u * u, axis=-1, keepdims=True)
        out_ref[...] = (u * lax.rsqrt(ms + EPS) * ng_ref[...]).astype(BF16)
    else:
        out_ref[...] = y


def _ssd(proj, dt_raw, z_col, fwd, params, bsz, seq, rev):
    t = proj.shape[0]
    ck = SSD_CHUNK
    nc = seq // ck
    d_xbc = params["cw"].shape[1]
    d_ssd = SSD_HEADS * SSD_HEADDIM
    d_bc = d_xbc - d_ssd
    per = ck // HALO
    last_halo = t // HALO - 1

    def ce_of(c):
        return nc - 1 - c if rev else c

    main = lambda b, c: (b * nc + ce_of(c), 0)
    prev = lambda b, c: (jnp.maximum((b * nc + ce_of(c)) * per - 1, 0), 0)
    nxt = lambda b, c: (jnp.minimum((b * nc + ce_of(c)) * per + per, last_halo), 0)
    const2 = lambda b, c: (0, 0)
    d = "b" if rev else "f"
    dir_specs = [pl.BlockSpec((1, LANES), const2), pl.BlockSpec((1, LANES), const2),
                 pl.BlockSpec((LANES, d_ssd), const2)]
    dir_args = [params["dtb_" + d], params["a_" + d], params["e2_" + d]]
    if rev:
        yf, xs, bc = fwd
        in_specs = [pl.BlockSpec((ck, d_ssd), main), pl.BlockSpec((ck, d_bc), main),
                    pl.BlockSpec((ck, LANES), main),
                    pl.BlockSpec((ck, d_ssd), lambda b, c: (b * nc + ce_of(c), z_col)),
                    pl.BlockSpec((ck, d_ssd), main)] + dir_specs + [
                        pl.BlockSpec((1, d_ssd), const2), pl.BlockSpec((1, d_ssd), const2)]
        args = [xs, bc, dt_raw, proj, yf] + dir_args + [params["dsk"], params["ng"]]
        out_specs = pl.BlockSpec((ck, d_ssd), main)
        out_shape = jax.ShapeDtypeStruct((t, d_ssd), BF16)
    else:
        in_specs = [pl.BlockSpec((ck, d_xbc), main), pl.BlockSpec((HALO, d_xbc), prev),
                    pl.BlockSpec((HALO, d_xbc), nxt), pl.BlockSpec((ck, LANES), main),
                    pl.BlockSpec((8, d_xbc), const2), pl.BlockSpec((1, d_xbc), const2)] + dir_specs
        args = [proj, proj, proj, dt_raw, params["cw"], params["cb"]] + dir_args
        out_specs = [pl.BlockSpec((ck, d_ssd), main), pl.BlockSpec((ck, d_ssd), main),
                     pl.BlockSpec((ck, d_bc), main)]
        out_shape = [jax.ShapeDtypeStruct((t, d_ssd), F32), jax.ShapeDtypeStruct((t, d_ssd), F32),
                     jax.ShapeDtypeStruct((t, d_bc), BF16)]
    return pl.pallas_call(
        functools.partial(_ssd_kernel, rev=rev),
        grid=(bsz, nc),
        in_specs=in_specs,
        out_specs=out_specs,
        out_shape=out_shape,
        scratch_shapes=[pltpu.VMEM((SSD_GROUPS, SSD_STATE, d_ssd // SSD_GROUPS), F32)],
        compiler_params=_cparams(("arbitrary", "arbitrary")),
        name="ssd_bwd" if rev else "ssd_fwd",
    )(*args)


def _na_plan(rows):
    nblk = rows // NA_QROWS
    variants, var_of_blk, ks_of_blk = [], [], []
    for i in range(nblk):
        ks = int(np.clip(i * NA_QROWS - NA_KH // 2, 0, rows - NA_KROWS))
        drow = np.zeros((NA_QROWS, NA_KROWS), np.int64)
        ok = np.zeros((NA_QROWS, NA_KROWS), bool)
        for dr in range(NA_QROWS):
            r = i * NA_QROWS + dr
            rs = int(np.clip(r - NA_KH // 2, 0, rows - NA_KH))
            for j in range(NA_KH):
                w = rs + j - ks
                drow[dr, w] = rs + j - r + NA_KH - 1
                ok[dr, w] = True
        key = (drow.tobytes(), ok.tobytes())
        for vi, (k2, _, _) in enumerate(variants):
            if k2 == key:
                var_of_blk.append(vi)
                break
        else:
            var_of_blk.append(len(variants))
            variants.append((key, drow, ok))
        ks_of_blk.append(ks)
    return [(d, o) for _, d, o in variants], var_of_blk, ks_of_blk


def _na_blk_start():
    ncb = GRID_W // NA_QB
    return [int(v) for v in np.clip(np.arange(ncb) * NA_QB - NA_KW // 2, 0, GRID_W - NA_KB)]


def _na_bias(rpb, rows):
    variants, var_of_blk, ks_of_blk = _na_plan(rows)
    nh = rpb.shape[0]
    cols = np.arange(GRID_W)
    win_start = np.clip(cols - NA_KW // 2, 0, GRID_W - NA_KW)
    in_win = (cols[None, :] >= win_start[:, None]) & (cols[None, :] < win_start[:, None] + NA_KW)
    dcol = np.clip(cols[None, :] - cols[:, None] + NA_KW - 1, 0, 2 * NA_KW - 2)
    onehot = (dcol[:, :, None] == np.arange(2 * NA_KW - 1)).astype(np.float32)
    band = jnp.einsum("hab,qkb->haqk", rpb.astype(F32), onehot, precision=lax.Precision.HIGHEST)
    band = jnp.where(in_win[None, None], band, -jnp.inf)
    neg = jnp.full((nh, NA_QB, NA_KB), -jnp.inf, F32)
    tabs = []
    for drow, ok in variants:
        per_cb = []
        for m, bs in enumerate(_na_blk_start()):
            qrows = [jnp.concatenate(
                [band[:, drow[dr, w], m * NA_QB:(m + 1) * NA_QB, bs:bs + NA_KB] if ok[dr, w] else neg
                 for w in range(NA_KROWS)], axis=2) for dr in range(NA_QROWS)]
            per_cb.append(jnp.concatenate(qrows, axis=1))
        tabs.append(jnp.stack(per_cb, axis=1))
    tab = jnp.stack(tabs)
    nv, _, ncb, mq, nk = tab.shape
    tab = tab.reshape(nv, nh // 2, 2, ncb, mq, nk).transpose(0, 1, 3, 2, 4, 5)
    return tab.reshape(nv, nh // 2, ncb, 2 * mq, nk), var_of_blk, ks_of_blk


def _na_kernel(var_ref, ks_ref, q_ref, k_ref, v_ref, bias_ref, o_ref, kf_ref, vf_ref):
    mq = NA_QROWS * NA_QB
    nblk = q_ref.shape[0] // (NA_QROWS * GRID_W)
    lo_half = lax.broadcasted_iota(I32, (mq, LANES), 1) < NA_HEADDIM
    kf_ref[...] = k_ref[...].astype(F32)
    vf_ref[...] = v_ref[...].astype(F32)

    def blk(i, carry):
        var = var_ref[i]
        ks = ks_ref[i]
        starts = _na_blk_start()
        q_at = [[pl.multiple_of((i * NA_QROWS + dr) * GRID_W + m * NA_QB, NA_QB)
                 for dr in range(NA_QROWS)] for m in range(len(starts))]
        k_at = [[pl.multiple_of((ks + w) * GRID_W + bs, 8) for w in range(NA_KROWS)]
                for bs in starts]
        scores = []
        for m in range(len(starts)):
            qb = jnp.concatenate([q_ref[pl.ds(o, NA_QB), :] for o in q_at[m]], axis=0)
            qb = qb * (NA_HEADDIM ** -0.5)
            zero = jnp.zeros_like(qb)
            q2 = jnp.concatenate([jnp.where(lo_half, qb, zero), jnp.where(lo_half, zero, qb)], axis=0)
            kb = jnp.concatenate([kf_ref[pl.ds(o, NA_KB), :] for o in k_at[m]], axis=0).astype(BF16)
            s = lax.dot_general(q2, kb, (((1,), (1,)), ((), ())), preferred_element_type=F32)
            scores.append(s + bias_ref[var, 0, m])
        outs = []
        for m, s in enumerate(scores):
            p = jnp.exp(s - jnp.max(s, axis=-1, keepdims=True))
            l = jnp.sum(p, axis=-1, keepdims=True)
            vb = jnp.concatenate([vf_ref[pl.ds(o, NA_KB), :] for o in k_at[m]], axis=0).astype(BF16)
            outs.append(jnp.dot(p.astype(BF16), vb, preferred_element_type=F32) / l)
        for m, o2 in enumerate(outs):
            ob = jnp.where(lo_half, o2[:mq], o2[mq:]).astype(BF16)
            for dr, o in enumerate(q_at[m]):
                o_ref[pl.ds(o, NA_QB), :] = ob[dr * NA_QB:(dr + 1) * NA_QB]
        return carry

    lax.fori_loop(0, nblk, blk, 0, unroll=2)


def _na(proj, bias, var_of_blk, ks_of_blk, q_col, bsz, seq):
    t = proj.shape[0]
    npair = NA_HEADS // 2
    grid_spec = pltpu.PrefetchScalarGridSpec(
        num_scalar_prefetch=2,
        grid=(bsz, npair),
        in_specs=[pl.BlockSpec((seq, LANES), lambda b, p, *_: (b, q_col + p)),
                  pl.BlockSpec((seq, LANES), lambda b, p, *_: (b, q_col + npair + p)),
                  pl.BlockSpec((seq, LANES), lambda b, p, *_: (b, q_col + 2 * npair + p)),
                  pl.BlockSpec((bias.shape[0], 1) + bias.shape[2:], lambda b, p, *_: (0, p, 0, 0, 0))],
        out_specs=pl.BlockSpec((seq, LANES), lambda b, p, *_: (b, p)),
        scratch_shapes=[pltpu.VMEM((seq, LANES), F32), pltpu.VMEM((seq, LANES), F32)],
    )
    return pl.pallas_call(
        _na_kernel,
        grid_spec=grid_spec,
        out_shape=jax.ShapeDtypeStruct((t, NA_HEADS * NA_HEADDIM), BF16),
        compiler_params=_cparams(("arbitrary", "arbitrary")),
        name="na",
    )(jnp.asarray(var_of_blk, I32), jnp.asarray(ks_of_blk, I32), proj, proj, proj, bias)


def _mix_kernel(x_ref, yn_ref, at_ref, gs_ref, gn_ref, g1_ref, sh_ref, sc_ref, ng_ref,
                ws_ref, wn_ref, wo_ref, wq_ref, keys_ref, h_ref, s_ref):
    y1 = jnp.dot(yn_ref[...], ws_ref[...], preferred_element_type=F32)
    y2 = jnp.dot(at_ref[...], wn_ref[...], preferred_element_type=F32)
    mixed = _sigmoid(gs_ref[...].astype(F32)) * y1 + _sigmoid(gn_ref[...].astype(F32)) * y2
    h = x_ref[...] + g1_ref[0] * jnp.dot(mixed.astype(BF16), wo_ref[...], preferred_element_type=F32)
    h_ref[...] = h
    n2 = _rms_mod(h, ng_ref[...], sh_ref[0], sc_ref[0]).astype(BF16)
    q = jnp.dot(n2, wq_ref[...], preferred_element_type=F32).astype(BF16)
    sub = 8
    for hz in range(2 * PEER_HEADS):
        qs = q[:, hz * PEER_DHALF:(hz + 1) * PEER_DHALF]
        sc = lax.dot_general(keys_ref[hz % 2], qs, (((1,), (1,)), ((), ())),
                             preferred_element_type=F32)
        for kq in range(PEER_NKEYS // sub):
            for tb in range(sc.shape[1] // LANES):
                s_ref[hz, kq, tb * sub:(tb + 1) * sub, :] = sc[kq * sub:(kq + 1) * sub,
                                                               tb * LANES:(tb + 1) * LANES]


def _mix(x2, ynorm, attn, proj, gs_col, g1, sh2, sc2, ng2, ws, wn, wo, wq, keys, seq):
    t, d = x2.shape
    tm = min(512, seq)
    bidx = lambda i: ((i * tm) // seq, 0, 0)
    row = lambda i: (i, 0)
    const2 = lambda i: (0, 0)
    return pl.pallas_call(
        _mix_kernel,
        grid=(t // tm,),
        in_specs=[pl.BlockSpec((tm, d), row), pl.BlockSpec((tm, d), row), pl.BlockSpec((tm, d), row),
                  pl.BlockSpec((tm, d), lambda i: (i, gs_col)),
                  pl.BlockSpec((tm, d), lambda i: (i, gs_col + 1)),
                  pl.BlockSpec((1, 1, d), bidx), pl.BlockSpec((1, 1, d), bidx),
                  pl.BlockSpec((1, 1, d), bidx), pl.BlockSpec((1, d), const2),
                  pl.BlockSpec((d, d), const2, pipeline_mode=pl.Buffered(1)),
                  pl.BlockSpec((d, d), const2, pipeline_mode=pl.Buffered(1)),
                  pl.BlockSpec((d, d), const2, pipeline_mode=pl.Buffered(1)),
                  pl.BlockSpec(wq.shape, const2, pipeline_mode=pl.Buffered(1)),
                  pl.BlockSpec(keys.shape, lambda i: (0, 0, 0), pipeline_mode=pl.Buffered(1))],
        out_specs=[pl.BlockSpec((tm, d), row),
                   pl.BlockSpec((2 * PEER_HEADS, PEER_NKEYS // 8, tm // LANES * 8, LANES),
                                lambda i: (0, 0, i, 0))],
        out_shape=[jax.ShapeDtypeStruct((t, d), F32),
                   jax.ShapeDtypeStruct((2 * PEER_HEADS, PEER_NKEYS // 8, t // LANES * 8, LANES), F32)],
        compiler_params=_cparams(("arbitrary",)),
        name="mix",
    )(x2, ynorm, attn, proj, proj, g1, sh2, sc2, ng2, ws, wn, wo, wq, keys)


def _hyperbola():
    return [(i, k) for i in range(PEER_TOPK) for k in range(PEER_TOPK)
            if (i + 1) * (k + 1) <= PEER_TOPK]


def _sort_network(n):
    pairs = []
    p = 1
    while p < n:
        k = p
        while k >= 1:
            for j in range(k % p, n - k, 2 * k):
                for i in range(min(k, n - j - k)):
                    if (i + j) // (2 * p) == (i + j + k) // (2 * p):
                        pairs.append((i + j, i + j + k))
            k //= 2
        p *= 2
    return pairs


def _precedes(va, pa, vb, pb):
    return (va > vb) | ((va == vb) & (pa < pb))


def _compare_exchange(v, p, i, j):
    c = _precedes(v[i], p[i], v[j], p[j])
    v[i], v[j] = jnp.where(c, v[i], v[j]), jnp.where(c, v[j], v[i])
    p[i], p[j] = jnp.where(c, p[i], p[j]), jnp.where(c, p[j], p[i])


def _top16(get, n, sv_ref, sp_ref):
    k = PEER_TOPK
    ng = n // k
    net = _sort_network(k)
    for g in range(ng):
        items = [get(g * k + j) for j in range(k)]
        v = [it[0] for it in items]
        p = [it[1] for it in items]
        for i, j in net:
            _compare_exchange(v, p, i, j)
        if ng == 1:
            return v, p
        for j in range(k):
            sv_ref[g, j] = v[j]
            sp_ref[g, j] = p[j]
    step = 1
    while True:
        for g in range(0, ng, 2 * step):
            v, p = [], []
            for j in range(k):
                xv, xp = sv_ref[g, j], sp_ref[g, j]
                yv, yp = sv_ref[g + step, k - 1 - j], sp_ref[g + step, k - 1 - j]
                c = _precedes(xv, xp, yv, yp)
                v.append(jnp.where(c, xv, yv))
                p.append(jnp.where(c, xp, yp))
            stride = k // 2
            while stride >= 1:
                for i in range(k):
                    if i & stride == 0:
                        _compare_exchange(v, p, i, i + stride)
                stride //= 2
            if 2 * step >= ng:
                return v, p
            for j in range(k):
                sv_ref[g, j] = v[j]
                sp_ref[g, j] = p[j]
        step *= 2


def _select_kernel(s_ref, eidx_ref, gate_ref, sv_ref, sp_ref):
    nk = PEER_NKEYS
    k = PEER_TOPK
    sub = 8
    shape = (sub, LANES)
    ebits = (nk * nk - 1).bit_length()

    def key(z, j):
        return s_ref[z, j // sub, pl.ds(j % sub, sub, stride=sub), :], jnp.full(shape, j, I32)

    tops = []
    for z in range(2):
        tops.append(_top16(functools.partial(key, z), nk, sv_ref, sp_ref))
    (tv0, ti0), (tv1, ti1) = tops

    pairs = _hyperbola()
    npad = -len(pairs) % k

    def cand(j):
        if j >= len(pairs):
            return jnp.full(shape, -jnp.inf, F32), jnp.full(shape, (k * k) << ebits, I32)
        i, kk = pairs[j]
        return tv0[i] + tv1[kk], ((i * k + kk) << ebits) + ti0[i] * nk + ti1[kk]

    top, ids = _top16(cand, len(pairs) + npad, sv_ref, sp_ref)
    ex = [jnp.exp(t - top[0]) for t in top]
    zsum = ex[0]
    for r in range(1, k):
        zsum = zsum + ex[r]
    for r in range(k):
        eidx_ref[0, r] = ids[r] & ((1 << ebits) - 1)
        gate_ref[0, r] = ex[r] / zsum


def _select(scores):
    r = 8
    nk = PEER_NKEYS
    tb = scores.shape[2] // r
    assert tb % r == 0
    oshape = (PEER_HEADS, PEER_TOPK, tb, LANES)
    return pl.pallas_call(
        _select_kernel,
        grid=(tb // r, PEER_HEADS),
        in_specs=[pl.BlockSpec((2, nk // r, r * r, LANES), lambda i, h: (h, 0, i, 0))],
        out_specs=[pl.BlockSpec((1, PEER_TOPK, r, LANES), lambda i, h: (h, 0, i, 0)),
                   pl.BlockSpec((1, PEER_TOPK, r, LANES), lambda i, h: (h, 0, i, 0))],
        out_shape=[jax.ShapeDtypeStruct(oshape, I32), jax.ShapeDtypeStruct(oshape, F32)],
        scratch_shapes=[pltpu.VMEM((nk // PEER_TOPK, PEER_TOPK, r, LANES), F32),
                        pltpu.VMEM((nk // PEER_TOPK, PEER_TOPK, r, LANES), I32)],
        compiler_params=_cparams(("arbitrary", "arbitrary")),
        name="select",
    )(scores)


def _peer_kernel(h_ref, sh_ref, sc_ref, g2_ref, ng_ref, fg_ref, eidx_ref, gate_ref, uv_ref,
                 out_ref, n2_ref, act_ref, wv_ref, w3_ref, acc_ref, *, ns):
    nk = PEER_NKEYS
    half = nk // 2
    hi_mask = jnp.uint32(0xFFFF0000)
    tm = h_ref.shape[0]
    eb = uv_ref.shape[0]
    nblk = eb // nk
    nblk2 = nblk // 2
    s = pl.program_id(1)

    @pl.when(s == 0)
    def _():
        n2_ref[...] = _rms_mod(h_ref[...], ng_ref[...], sh_ref[0], sc_ref[0]).astype(BF16)
        act_ref[...] = jnp.zeros_like(act_ref)

    @pl.when(s < ns)
    def _():
        e = eidx_ref[...]
        ai = jnp.right_shift(e, nk.bit_length() - 1)
        bi = jnp.bitwise_and(e, nk - 1)
        act = act_ref[...]
        n2 = n2_ref[...]
        for a in range(0, nblk, 2):
            sc = lax.dot_general(n2, uv_ref[a * nk:(a + 2) * nk, :], (((1,), (1,)), ((), ())),
                                 preferred_element_type=F32)
            for j in range(2):
                got = jnp.take_along_axis(sc[:, j * nk:(j + 1) * nk], bi, axis=1)
                act = jnp.where(ai == s * nblk + a + j, got, act)
        act_ref[...] = act

    @pl.when(s == ns - 1)
    def _():
        act = act_ref[...]
        gelu = 0.5 * act * (1.0 + lax.erf(act * (2.0 ** -0.5)))
        wv_ref[...] = gate_ref[...] * gelu
        sub = lax.broadcasted_iota(I32, (nk, LANES), 0).astype(F32).astype(BF16)
        zero = jnp.zeros((nk, LANES), BF16)
        one = jnp.ones((nk, LANES), BF16)

        def tok(t, carry):
            e = eidx_ref[pl.ds(t, 1), :]
            w = wv_ref[pl.ds(t, 1), :].astype(BF16)
            ar = jnp.right_shift(e, nk.bit_length() - 1).astype(F32).astype(BF16)
            br = jnp.bitwise_and(e, nk - 1).astype(F32).astype(BF16)
            pt = jnp.where(sub == ar, w, zero)
            qt = jnp.where(sub == br, one, zero)
            wt = lax.dot_general(pt, qt, (((1,), (1,)), ((), ())), preferred_element_type=F32)
            bits = lax.bitcast_convert_type(wt, jnp.uint32)
            word = jnp.right_shift(bits[:half], 16) | (bits[half:] & hi_mask)
            w3_ref[pl.ds(pl.multiple_of(t * W3_PITCH, 8), half), :] = word
            return carry
        lax.fori_loop(0, tm, tok, 0, unroll=32)

    @pl.when(s >= ns)
    def _():
        base = (s - ns) * nblk2
        words = [w3_ref[pl.ds(base + a, tm, stride=W3_PITCH), :] for a in range(nblk2)]
        lo = [lax.bitcast_convert_type(jnp.left_shift(w, 16), F32).astype(BF16) for w in words]
        hi = [lax.bitcast_convert_type(w & hi_mask, F32).astype(BF16) for w in words]
        part = jnp.dot(jnp.concatenate(lo + hi, axis=1), uv_ref[...], preferred_element_type=F32)

        @pl.when(s == ns)
        def _():
            acc_ref[...] = part

        @pl.when(s > ns)
        def _():
            acc_ref[...] += part

    @pl.when(s == 2 * ns - 1)
    def _():
        hh = h_ref[...] + g2_ref[0] * acc_ref[...]
        ms = jnp.mean(hh * hh, axis=-1, keepdims=True)
        out_ref[...] = hh * lax.rsqrt(ms + EPS) * fg_ref[...]


def _peer(h, sh2, sc2, g2, ng2, fg, eidx_t, gate_t, u, v, seq):
    t, d = h.shape
    ne = u.shape[0]
    tm = min(512, seq)
    eb = 4096
    ns = ne // eb
    nj = eidx_t.shape[1]
    v_blocks = v.reshape(2, ns, eb // 2, d).transpose(1, 0, 2, 3).reshape(ne, d)
    uv = jnp.concatenate([u, v_blocks], axis=0).astype(BF16)
    bidx = lambda i, s: ((i * tm) // seq, 0, 0)
    row = lambda i, s: (i, 0)
    const2 = lambda i, s: (0, 0)
    return pl.pallas_call(
        functools.partial(_peer_kernel, ns=ns),
        grid=(t // tm, 2 * ns),
        in_specs=[pl.BlockSpec((tm, d), row),
                  pl.BlockSpec((1, 1, d), bidx), pl.BlockSpec((1, 1, d), bidx),
                  pl.BlockSpec((1, 1, d), bidx),
                  pl.BlockSpec((1, d), const2), pl.BlockSpec((1, d), const2),
                  pl.BlockSpec((tm, nj), row), pl.BlockSpec((tm, nj), row),
                  pl.BlockSpec((eb, d), lambda i, s: (s, 0))],
        out_specs=pl.BlockSpec((tm, d), row),
        out_shape=jax.ShapeDtypeStruct((t, d), F32),
        scratch_shapes=[pltpu.VMEM((tm, d), BF16),
                        pltpu.VMEM((tm, nj), F32),
                        pltpu.VMEM((tm, nj), F32),
                        pltpu.VMEM((tm * W3_PITCH, PEER_NKEYS), jnp.uint32),
                        pltpu.VMEM((tm, d), F32)],
        compiler_params=_cparams(("arbitrary", "arbitrary")),
        name="peer",
    )(h, sh2, sc2, g2, ng2, fg, eidx_t, gate_t, uv)


def _layer(x2, c, bsz, seq, w_ada, b_ada, norm1_g, w_in, conv_w, conv_b, dt_bias_f, dt_bias_b,
           a_log_f, a_log_b, d_skip, ssd_norm_g, w_ssd_br, na_rpb, w_na_br, w_out, norm2_g,
           peer_wq, peer_keys, peer_u, peer_v, out_g):
    t, d = x2.shape
    d_ssd = SSD_HEADS * SSD_HEADDIM
    d_xbc = d_ssd + 2 * SSD_GROUPS * SSD_STATE
    d_na = NA_HEADS * NA_HEADDIM
    assert seq % SSD_CHUNK == 0 and seq % (GRID_W * NA_QROWS) == 0
    assert seq // GRID_W >= NA_KROWS and d == d_ssd == d_na

    mod = _ada(c, w_ada, b_ada)
    sh1, sc1, g1, sh2, sc2, g2 = [m.reshape(bsz, 1, d) for m in jnp.split(mod, 6, axis=-1)]

    o = np.cumsum([0, d_ssd, d_xbc, SSD_HEADS, SSD_HEADS, 3 * d_na, d, d])
    w_main = jnp.concatenate([w_in[:, o[1]:o[2]], w_in[:, o[0]:o[1]], w_in[:, o[4]:o[7]]],
                             axis=1).astype(BF16)
    w_dt = jnp.pad(w_in[:, o[2]:o[4]], ((0, 0), (0, LANES - 2 * SSD_HEADS))).astype(BF16)
    z_col = d_xbc // d_ssd
    q_col = (d_xbc + d_ssd) // LANES
    gs_col = (d_xbc + d_ssd + 3 * d_na) // d
    proj, dt_raw = _inproj(x2, sh1, sc1, norm1_g.reshape(1, d), w_main, w_dt, seq)

    heads = np.arange(d_ssd) // SSD_HEADDIM

    def e2(off):
        m = np.zeros((LANES, d_ssd), np.float32)
        m[off + heads, np.arange(d_ssd)] = 1.0
        m[off + 64 + heads, np.arange(d_ssd)] = 1.0
        return jnp.asarray(m, BF16)

    def lanes16(vec, off):
        return jnp.zeros((1, LANES), F32).at[0, off:off + SSD_HEADS].set(vec.astype(F32))

    params = {
        "cw": jnp.pad(conv_w.astype(F32), ((0, 8 - CONV_W), (0, 0))),
        "cb": conv_b.astype(F32).reshape(1, d_xbc),
        "dtb_f": lanes16(dt_bias_f, 0), "dtb_b": lanes16(dt_bias_b, SSD_HEADS),
        "a_f": lanes16(-jnp.exp(a_log_f.astype(F32)), 0),
        "a_b": lanes16(-jnp.exp(a_log_b.astype(F32)), SSD_HEADS),
        "e2_f": e2(0), "e2_b": e2(SSD_HEADS),
        "dsk": jnp.repeat(d_skip.astype(F32), SSD_HEADDIM).reshape(1, d_ssd),
        "ng": ssd_norm_g.astype(F32).reshape(1, d_ssd),
    }
    fwd = _ssd(proj, dt_raw, z_col, None, params, bsz, seq, rev=False)
    ynorm = _ssd(proj, dt_raw, z_col, fwd, params, bsz, seq, rev=True)

    bias, var_of_blk, ks_of_blk = _na_bias(na_rpb, seq // GRID_W)
    attn = _na(proj, bias, var_of_blk, ks_of_blk, q_col, bsz, seq)

    h, scores = _mix(x2, ynorm, attn, proj, gs_col, g1, sh2, sc2, norm2_g.reshape(1, d),
                     w_ssd_br.astype(BF16), w_na_br.astype(BF16), w_out.astype(BF16),
                     peer_wq.astype(BF16), peer_keys.astype(BF16), seq)

    eidx, gate = _select(scores)
    nj = PEER_HEADS * PEER_TOPK
    eidx_t = eidx.reshape(nj, t).T
    gate_t = gate.reshape(nj, t).T
    return _peer(h, sh2, sc2, g2, norm2_g.reshape(1, d), out_g, eidx_t, gate_t,
                 peer_u.astype(BF16), peer_v.astype(BF16), seq)


def kernel(x, c, w_ada, b_ada, norm1_g, w_in, conv_w, conv_b, dt_bias_f, dt_bias_b, a_log_f, a_log_b,
           d_skip, ssd_norm_g, w_ssd_br, na_rpb, w_na_br, w_out, norm2_g, peer_wq, peer_keys, peer_u,
           peer_v, final_g):
    bsz, seq, d = x.shape
    depth = w_ada.shape[0]
    assert depth == 1, "the final RMSNorm is fused into the last layer's PEER kernel"
    i = 0
    out = _layer(x.reshape(bsz * seq, d), c, bsz, seq, w_ada[i], b_ada[i], norm1_g[i], w_in[i],
                 conv_w[i], conv_b[i], dt_bias_f[i], dt_bias_b[i], a_log_f[i], a_log_b[i], d_skip[i],
                 ssd_norm_g[i], w_ssd_br[i], na_rpb[i], w_na_br[i], w_out[i], norm2_g[i],
                 peer_wq[i], peer_keys[i], peer_u[i], peer_v[i], final_g.reshape(1, d))
    return out.reshape(bsz, seq, d)
```

```python
import functools

import numpy as np
import jax
import jax.numpy as jnp
from jax import lax
from jax.experimental import pallas as pl
from jax.experimental.pallas import tpu as pltpu

F32 = jnp.float32
BF16 = jnp.bfloat16
I32 = jnp.int32

EPS = 1e-6
GRID_W = 64
SSD_HEADS = 16
SSD_HEADDIM = 64
SSD_GROUPS = 4
SSD_STATE = 128
SSD_CHUNK = 128
SSD_SUB = 2
CONV_W = 5
NA_HEADS = 16
NA_HEADDIM = 64
NA_KH = 8
NA_KW = 16
NA_QROWS = 4
NA_KROWS = 12
NA_QB = 16
NA_KB = NA_QB + NA_KW
PEER_HEADS = 8
PEER_NKEYS = 128
PEER_TOPK = 16
PEER_DHALF = 128
W3_PITCH = PEER_NKEYS // 2 + 8

LANES = 128
HALO = 16
VMEM_LIMIT = 56 * 1024 * 1024


def _cparams(sem):
    return pltpu.CompilerParams(dimension_semantics=sem, vmem_limit_bytes=VMEM_LIMIT)


def _sigmoid(x):
    return 0.5 * jnp.tanh(0.5 * x) + 0.5


def _rms_mod(x, g, shift, scale):
    ms = jnp.mean(x * x, axis=-1, keepdims=True)
    y = x * lax.rsqrt(ms + EPS) * g
    return y * (1.0 + scale) + shift


def _ada_kernel(c_ref, w_ref, b_ref, o_ref):
    c = c_ref[...]
    sc = c * _sigmoid(c)
    o_ref[...] = jnp.dot(sc, w_ref[...], preferred_element_type=F32,
                         precision=lax.Precision.HIGHEST) + b_ref[...]


def _ada(c, w, b):
    bsz, d = c.shape
    n = w.shape[1]
    tn = 1024
    return pl.pallas_call(
        _ada_kernel,
        grid=(n // tn,),
        in_specs=[pl.BlockSpec((bsz, d), lambda j: (0, 0)),
                  pl.BlockSpec((d, tn), lambda j: (0, j)),
                  pl.BlockSpec((1, tn), lambda j: (0, j))],
        out_specs=pl.BlockSpec((bsz, tn), lambda j: (0, j)),
        out_shape=jax.ShapeDtypeStruct((bsz, n), F32),
        compiler_params=_cparams(("arbitrary",)),
        name="ada",
    )(c, w, b.reshape(1, n))


def _inproj_kernel(x_ref, sh_ref, sc_ref, g_ref, w_ref, wdt_ref, proj_ref, dt_ref):
    n1 = _rms_mod(x_ref[...], g_ref[...], sh_ref[0], sc_ref[0]).astype(BF16)
    dt_ref[...] = jnp.dot(n1, wdt_ref[...], preferred_element_type=F32)
    tn = 1024
    for j in range(w_ref.shape[1] // tn):
        proj_ref[:, j * tn:(j + 1) * tn] = jnp.dot(
            n1, w_ref[:, j * tn:(j + 1) * tn], preferred_element_type=F32).astype(BF16)


def _inproj(x2, sh, sc, g, w, wdt, seq):
    t, d = x2.shape
    n = w.shape[1]
    tm = min(512, seq)
    bidx = lambda i: ((i * tm) // seq, 0, 0)
    const2 = lambda i: (0, 0)
    return pl.pallas_call(
        _inproj_kernel,
        grid=(t // tm,),
        in_specs=[pl.BlockSpec((tm, d), lambda i: (i, 0)),
                  pl.BlockSpec((1, 1, d), bidx),
                  pl.BlockSpec((1, 1, d), bidx),
                  pl.BlockSpec((1, d), const2),
                  pl.BlockSpec((d, n), const2, pipeline_mode=pl.Buffered(1)),
                  pl.BlockSpec((d, LANES), const2, pipeline_mode=pl.Buffered(1))],
        out_specs=[pl.BlockSpec((tm, n), lambda i: (i, 0)),
                   pl.BlockSpec((tm, LANES), lambda i: (i, 0))],
        out_shape=[jax.ShapeDtypeStruct((t, n), BF16),
                   jax.ShapeDtypeStruct((t, LANES), F32)],
        compiler_params=_cparams(("arbitrary",)),
        name="inproj",
    )(x2, sh, sc, g, w, wdt)


def _expand_heads(v, e2_ref):
    hi = v.astype(BF16).astype(F32)
    comb = (hi + pltpu.roll(v - hi, 64, 1)).astype(BF16)
    return jnp.dot(comb, e2_ref[...], preferred_element_type=F32)


def _ssd_conv(main, ext, cw_ref, cb_ref):
    ck = main.shape[0]
    d_ssd = SSD_HEADS * SSD_HEADDIM
    erow = lax.broadcasted_iota(I32, (ck, ck + 2 * HALO), 0)
    ecol = lax.broadcasted_iota(I32, (ck, ck + 2 * HALO), 1)
    mid = CONV_W // 2
    acc = cb_ref[...] + main.astype(F32) * cw_ref[mid:mid + 1, :]
    for w in range(CONV_W):
        if w != mid:
            shift = jnp.where(ecol == erow + (HALO + w - mid), 1.0, 0.0).astype(BF16)
            acc = acc + jnp.dot(shift, ext, preferred_element_type=F32) * cw_ref[w:w + 1, :]
    xbc = acc * _sigmoid(acc)
    return xbc[:, :d_ssd], xbc[:, d_ssd:].astype(BF16)


def _ssd_scan(xs, bc, dt_raw, dtb_ref, a_ref, e2_ref, st_ref, rev):
    ck = SSD_CHUNK
    d_ssd = SSD_HEADS * SSD_HEADDIM
    d_bc = SSD_GROUPS * SSD_STATE
    gw = d_ssd // SSD_GROUPS
    lane = lax.broadcasted_iota(I32, (ck, LANES), 1)
    row = lax.broadcasted_iota(I32, (ck, LANES), 0)
    off = SSD_HEADS if rev else 0
    valid = (lane >= off) & (lane < off + SSD_HEADS)
    dtr = dt_raw + dtb_ref[...]
    dt = jnp.where(valid, jnp.maximum(dtr, 0.0) + jnp.log1p(jnp.exp(-jnp.abs(dtr))), 0.0)
    da = dt * a_ref[...]
    tri = (lane >= row) if rev else (lane <= row)
    trib = jnp.where(tri, 1.0, 0.0).astype(BF16)
    hi = da.astype(BF16)
    r1 = da - hi.astype(F32)
    mid = r1.astype(BF16)
    lo = (r1 - mid.astype(F32)).astype(BF16)
    cs = (jnp.dot(trib, hi, preferred_element_type=F32)
          + jnp.dot(trib, mid, preferred_element_type=F32)
          + jnp.dot(trib, lo, preferred_element_type=F32))
    cst = cs.T
    tot_row = 0 if rev else ck - 1
    tot = cs[tot_row:tot_row + 1, :]
    dec = jnp.where(valid, jnp.exp(tot - cs), 0.0)
    ecs = jnp.where(valid, jnp.exp(cs), 0.0)
    ecs_e = _expand_heads(ecs, e2_ref)
    xdt = xs * _expand_heads(dt, e2_ref)
    xdec = (xs * _expand_heads(dt * dec, e2_ref)).astype(BF16)

    lo_half = lax.broadcasted_iota(I32, (ck, LANES), 1) < SSD_HEADDIM
    hpg = SSD_HEADS // SSD_GROUPS
    ys = []
    for g in range(SSD_GROUPS):
        bg = bc[:, g * SSD_STATE:(g + 1) * SSD_STATE]
        cg = bc[:, d_bc + g * SSD_STATE:d_bc + (g + 1) * SSD_STATE]
        cbg = lax.dot_general(cg, bg, (((1,), (1,)), ((), ())), preferred_element_type=F32)
        yg = []
        for q in range(hpg // 2):
            ls = []
            for e in range(2):
                col = off + g * hpg + q * 2 + e
                dlt = cs[:, col:col + 1] - cst[col:col + 1, :]
                seg = jnp.exp(jnp.where(tri, dlt, -jnp.inf))
                ls.append((cbg * seg).astype(BF16))
            blk = g * (hpg // 2) + q
            xq = xdt[:, blk * LANES:(blk + 1) * LANES]
            rhs = jnp.concatenate([jnp.where(lo_half, xq, 0.0), jnp.where(lo_half, 0.0, xq)],
                                  axis=0).astype(BF16)
            yg.append(jnp.dot(jnp.concatenate(ls, axis=1), rhs, preferred_element_type=F32))
        y_diag = jnp.concatenate(yg, axis=1)
        st = st_ref[g]
        e_g = ecs_e[:, g * gw:(g + 1) * gw]
        y_off = jnp.dot(cg, st.astype(BF16), preferred_element_type=F32) * e_g
        ys.append(y_diag + y_off)
        upd = jnp.dot(bg.astype(F32).T.astype(BF16), xdec[:, g * gw:(g + 1) * gw],
                      preferred_element_type=F32)
        st_ref[g] = st * e_g[tot_row:tot_row + 1, :] + upd
    return jnp.concatenate(ys, axis=1)


def _ssd_kernel(*refs, rev):
    if rev:
        (xs_ref, bc_ref, dt_ref, z_ref, yf_ref, dtb_ref, a_ref, e2_ref, dsk_ref, ng_ref,
         out_ref, st_ref) = refs
    else:
        (xm_ref, xp_ref, xn_ref, dt_ref, cw_ref, cb_ref, dtb_ref, a_ref, e2_ref,
         out_ref, xs_out_ref, bc_out_ref, st_ref) = refs
    ck = SSD_CHUNK
    c = pl.program_id(1)
    nc = pl.num_programs(1)
    ce = nc - 1 - c if rev else c

    @pl.when(c == 0)
    def _():
        st_ref[...] = jnp.zeros_like(st_ref)

    rows = [slice(j * ck, (j + 1) * ck) for j in range(SSD_SUB)]
    if rev:
        conv = [(xs_ref[r, :], bc_ref[r, :]) for r in rows]
    else:
        main = xm_ref[...]
        zero_halo = jnp.zeros((HALO, main.shape[1]), main.dtype)
        ext = jnp.concatenate([jnp.where(ce == 0, zero_halo, xp_ref[...]), main,
                               jnp.where(ce == nc - 1, zero_halo, xn_ref[...])], axis=0)
        conv = [_ssd_conv(main[r], ext[r.start:r.stop + 2 * HALO], cw_ref, cb_ref) for r in rows]
        for r, (xs, bc) in zip(rows, conv):
            xs_out_ref[r, :] = xs
            bc_out_ref[r, :] = bc

    for j in (reversed(range(SSD_SUB)) if rev else range(SSD_SUB)):
        r = rows[j]
        xs, bc = conv[j]
        y = _ssd_scan(xs, bc, dt_ref[r, :], dtb_ref, a_ref, e2_ref, st_ref, rev)
        if rev:
            y = yf_ref[r, :] + y + dsk_ref[...] * xs
            z = z_ref[r, :].astype(F32)
            u = y * (z * _sigmoid(z))
            ms = jnp.mean(u * u, axis=-1, keepdims=True)
            out_ref[r, :] = (u * lax.rsqrt(ms + EPS) * ng_ref[...]).astype(BF16)
        else:
            out_ref[r, :] = y


def _ssd(proj, dt_raw, z_col, fwd, params, bsz, seq, rev):
    t = proj.shape[0]
    ck = SSD_CHUNK * SSD_SUB
    nc = seq // ck
    d_xbc = params["cw"].shape[1]
    d_ssd = SSD_HEADS * SSD_HEADDIM
    d_bc = d_xbc - d_ssd
    per = ck // HALO
    last_halo = t // HALO - 1

    def ce_of(c):
        return nc - 1 - c if rev else c

    main = lambda b, c: (b * nc + ce_of(c), 0)
    prev = lambda b, c: (jnp.maximum((b * nc + ce_of(c)) * per - 1, 0), 0)
    nxt = lambda b, c: (jnp.minimum((b * nc + ce_of(c)) * per + per, last_halo), 0)
    const2 = lambda b, c: (0, 0)
    d = "b" if rev else "f"
    dir_specs = [pl.BlockSpec((1, LANES), const2), pl.BlockSpec((1, LANES), const2),
                 pl.BlockSpec((LANES, d_ssd), const2)]
    dir_args = [params["dtb_" + d], params["a_" + d], params["e2_" + d]]
    if rev:
        yf, xs, bc = fwd
        in_specs = [pl.BlockSpec((ck, d_ssd), main), pl.BlockSpec((ck, d_bc), main),
                    pl.BlockSpec((ck, LANES), main),
                    pl.BlockSpec((ck, d_ssd), lambda b, c: (b * nc + ce_of(c), z_col)),
                    pl.BlockSpec((ck, d_ssd), main)] + dir_specs + [
                        pl.BlockSpec((1, d_ssd), const2), pl.BlockSpec((1, d_ssd), const2)]
        args = [xs, bc, dt_raw, proj, yf] + dir_args + [params["dsk"], params["ng"]]
        out_specs = pl.BlockSpec((ck, d_ssd), main)
        out_shape = jax.ShapeDtypeStruct((t, d_ssd), BF16)
    else:
        in_specs = [pl.BlockSpec((ck, d_xbc), main), pl.BlockSpec((HALO, d_xbc), prev),
                    pl.BlockSpec((HALO, d_xbc), nxt), pl.BlockSpec((ck, LANES), main),
                    pl.BlockSpec((8, d_xbc), const2), pl.BlockSpec((1, d_xbc), const2)] + dir_specs
        args = [proj, proj, proj, dt_raw, params["cw"], params["cb"]] + dir_args
        out_specs = [pl.BlockSpec((ck, d_ssd), main), pl.BlockSpec((ck, d_ssd), main),
                     pl.BlockSpec((ck, d_bc), main)]
        out_shape = [jax.ShapeDtypeStruct((t, d_ssd), F32), jax.ShapeDtypeStruct((t, d_ssd), F32),
                     jax.ShapeDtypeStruct((t, d_bc), BF16)]
    return pl.pallas_call(
        functools.partial(_ssd_kernel, rev=rev),
        grid=(bsz, nc),
        in_specs=in_specs,
        out_specs=out_specs,
        out_shape=out_shape,
        scratch_shapes=[pltpu.VMEM((SSD_GROUPS, SSD_STATE, d_ssd // SSD_GROUPS), F32)],
        compiler_params=_cparams(("arbitrary", "arbitrary")),
        name="ssd_bwd" if rev else "ssd_fwd",
    )(*args)


def _na_plan(rows):
    nblk = rows // NA_QROWS
    variants, var_of_blk, ks_of_blk = [], [], []
    for i in range(nblk):
        ks = int(np.clip(i * NA_QROWS - NA_KH // 2, 0, rows - NA_KROWS))
        drow = np.zeros((NA_QROWS, NA_KROWS), np.int64)
        ok = np.zeros((NA_QROWS, NA_KROWS), bool)
        for dr in range(NA_QROWS):
            r = i * NA_QROWS + dr
            rs = int(np.clip(r - NA_KH // 2, 0, rows - NA_KH))
            for j in range(NA_KH):
                w = rs + j - ks
                drow[dr, w] = rs + j - r + NA_KH - 1
                ok[dr, w] = True
        key = (drow.tobytes(), ok.tobytes())
        for vi, (k2, _, _) in enumerate(variants):
            if k2 == key:
                var_of_blk.append(vi)
                break
        else:
            var_of_blk.append(len(variants))
            variants.append((key, drow, ok))
        ks_of_blk.append(ks)
    return [(d, o) for _, d, o in variants], var_of_blk, ks_of_blk


def _na_blk_start():
    ncb = GRID_W // NA_QB
    return [int(v) for v in np.clip(np.arange(ncb) * NA_QB - NA_KW // 2, 0, GRID_W - NA_KB)]


def _na_bias(rpb, rows):
    variants, var_of_blk, ks_of_blk = _na_plan(rows)
    nh = rpb.shape[0]
    cols = np.arange(GRID_W)
    win_start = np.clip(cols - NA_KW // 2, 0, GRID_W - NA_KW)
    in_win = (cols[None, :] >= win_start[:, None]) & (cols[None, :] < win_start[:, None] + NA_KW)
    dcol = np.clip(cols[None, :] - cols[:, None] + NA_KW - 1, 0, 2 * NA_KW - 2)
    onehot = (dcol[:, :, None] == np.arange(2 * NA_KW - 1)).astype(np.float32)
    band = jnp.einsum("hab,qkb->haqk", rpb.astype(F32), onehot, precision=lax.Precision.HIGHEST)
    band = jnp.where(in_win[None, None], band, -jnp.inf)
    neg = jnp.full((nh, NA_QB, NA_KB), -jnp.inf, F32)
    tabs = []
    for drow, ok in variants:
        per_cb = []
        for m, bs in enumerate(_na_blk_start()):
            qrows = [jnp.concatenate(
                [band[:, drow[dr, w], m * NA_QB:(m + 1) * NA_QB, bs:bs + NA_KB] if ok[dr, w] else neg
                 for w in range(NA_KROWS)], axis=2) for dr in range(NA_QROWS)]
            per_cb.append(jnp.concatenate(qrows, axis=1))
        tabs.append(jnp.stack(per_cb, axis=1))
    tab = jnp.stack(tabs)
    nv, _, ncb, mq, nk = tab.shape
    tab = tab.reshape(nv, nh // 2, 2, ncb, mq, nk).transpose(0, 1, 3, 2, 4, 5)
    return tab.reshape(nv, nh // 2, ncb, 2 * mq, nk), var_of_blk, ks_of_blk


def _na_kernel(var_ref, ks_ref, q_ref, k_ref, v_ref, bias_ref, o_ref, kf_ref, vf_ref):
    mq = NA_QROWS * NA_QB
    nblk = q_ref.shape[0] // (NA_QROWS * GRID_W)
    lo_half = lax.broadcasted_iota(I32, (mq, LANES), 1) < NA_HEADDIM
    kf_ref[...] = k_ref[...].astype(F32)
    vf_ref[...] = v_ref[...].astype(F32)

    def blk(i, carry):
        var = var_ref[i]
        ks = ks_ref[i]
        starts = _na_blk_start()
        q_at = [[pl.multiple_of((i * NA_QROWS + dr) * GRID_W + m * NA_QB, NA_QB)
                 for dr in range(NA_QROWS)] for m in range(len(starts))]
        k_at = [[pl.multiple_of((ks + w) * GRID_W + bs, 8) for w in range(NA_KROWS)]
                for bs in starts]
        scores = []
        for m in range(len(starts)):
            qb = jnp.concatenate([q_ref[pl.ds(o, NA_QB), :] for o in q_at[m]], axis=0)
            qb = qb * (NA_HEADDIM ** -0.5)
            zero = jnp.zeros_like(qb)
            q2 = jnp.concatenate([jnp.where(lo_half, qb, zero), jnp.where(lo_half, zero, qb)], axis=0)
            kb = jnp.concatenate([kf_ref[pl.ds(o, NA_KB), :] for o in k_at[m]], axis=0).astype(BF16)
            s = lax.dot_general(q2, kb, (((1,), (1,)), ((), ())), preferred_element_type=F32)
            scores.append(s + bias_ref[var, 0, m])
        outs = []
        for m, s in enumerate(scores):
            p = jnp.exp(s - jnp.max(s, axis=-1, keepdims=True))
            l = jnp.sum(p, axis=-1, keepdims=True)
            vb = jnp.concatenate([vf_ref[pl.ds(o, NA_KB), :] for o in k_at[m]], axis=0).astype(BF16)
            outs.append(jnp.dot(p.astype(BF16), vb, preferred_element_type=F32) / l)
        for m, o2 in enumerate(outs):
            ob = jnp.where(lo_half, o2[:mq], o2[mq:]).astype(BF16)
            for dr, o in enumerate(q_at[m]):
                o_ref[pl.ds(o, NA_QB), :] = ob[dr * NA_QB:(dr + 1) * NA_QB]
        return carry

    lax.fori_loop(0, nblk, blk, 0, unroll=2)


def _na(proj, bias, var_of_blk, ks_of_blk, q_col, bsz, seq):
    t = proj.shape[0]
    npair = NA_HEADS // 2
    grid_spec = pltpu.PrefetchScalarGridSpec(
        num_scalar_prefetch=2,
        grid=(bsz, npair),
        in_specs=[pl.BlockSpec((seq, LANES), lambda b, p, *_: (b, q_col + p)),
                  pl.BlockSpec((seq, LANES), lambda b, p, *_: (b, q_col + npair + p)),
                  pl.BlockSpec((seq, LANES), lambda b, p, *_: (b, q_col + 2 * npair + p)),
                  pl.BlockSpec((bias.shape[0], 1) + bias.shape[2:], lambda b, p, *_: (0, p, 0, 0, 0))],
        out_specs=pl.BlockSpec((seq, LANES), lambda b, p, *_: (b, p)),
        scratch_shapes=[pltpu.VMEM((seq, LANES), F32), pltpu.VMEM((seq, LANES), F32)],
    )
    return pl.pallas_call(
        _na_kernel,
        grid_spec=grid_spec,
        out_shape=jax.ShapeDtypeStruct((t, NA_HEADS * NA_HEADDIM), BF16),
        compiler_params=_cparams(("arbitrary", "arbitrary")),
        name="na",
    )(jnp.asarray(var_of_blk, I32), jnp.asarray(ks_of_blk, I32), proj, proj, proj, bias)


def _mix_kernel(x_ref, yn_ref, at_ref, gs_ref, gn_ref, g1_ref, sh_ref, sc_ref, ng_ref,
                ws_ref, wn_ref, wo_ref, wq_ref, keys_ref, h_ref, s_ref):
    y1 = jnp.dot(yn_ref[...], ws_ref[...], preferred_element_type=F32)
    y2 = jnp.dot(at_ref[...], wn_ref[...], preferred_element_type=F32)
    mixed = _sigmoid(gs_ref[...].astype(F32)) * y1 + _sigmoid(gn_ref[...].astype(F32)) * y2
    h = x_ref[...] + g1_ref[0] * jnp.dot(mixed.astype(BF16), wo_ref[...], preferred_element_type=F32)
    h_ref[...] = h
    n2 = _rms_mod(h, ng_ref[...], sh_ref[0], sc_ref[0]).astype(BF16)
    q = jnp.dot(n2, wq_ref[...], preferred_element_type=F32).astype(BF16)
    sub = 8
    for hz in range(2 * PEER_HEADS):
        qs = q[:, hz * PEER_DHALF:(hz + 1) * PEER_DHALF]
        sc = lax.dot_general(keys_ref[hz % 2], qs, (((1,), (1,)), ((), ())),
                             preferred_element_type=F32)
        for kq in range(PEER_NKEYS // sub):
            for tb in range(sc.shape[1] // LANES):
                s_ref[hz, kq, tb * sub:(tb + 1) * sub, :] = sc[kq * sub:(kq + 1) * sub,
                                                               tb * LANES:(tb + 1) * LANES]


def _mix(x2, ynorm, attn, proj, gs_col, g1, sh2, sc2, ng2, ws, wn, wo, wq, keys, seq):
    t, d = x2.shape
    tm = min(512, seq)
    bidx = lambda i: ((i * tm) // seq, 0, 0)
    row = lambda i: (i, 0)
    const2 = lambda i: (0, 0)
    return pl.pallas_call(
        _mix_kernel,
        grid=(t // tm,),
        in_specs=[pl.BlockSpec((tm, d), row), pl.BlockSpec((tm, d), row), pl.BlockSpec((tm, d), row),
                  pl.BlockSpec((tm, d), lambda i: (i, gs_col)),
                  pl.BlockSpec((tm, d), lambda i: (i, gs_col + 1)),
                  pl.BlockSpec((1, 1, d), bidx), pl.BlockSpec((1, 1, d), bidx),
                  pl.BlockSpec((1, 1, d), bidx), pl.BlockSpec((1, d), const2),
                  pl.BlockSpec((d, d), const2, pipeline_mode=pl.Buffered(1)),
                  pl.BlockSpec((d, d), const2, pipeline_mode=pl.Buffered(1)),
                  pl.BlockSpec((d, d), const2, pipeline_mode=pl.Buffered(1)),
                  pl.BlockSpec(wq.shape, const2, pipeline_mode=pl.Buffered(1)),
                  pl.BlockSpec(keys.shape, lambda i: (0, 0, 0), pipeline_mode=pl.Buffered(1))],
        out_specs=[pl.BlockSpec((tm, d), row),
                   pl.BlockSpec((2 * PEER_HEADS, PEER_NKEYS // 8, tm // LANES * 8, LANES),
                                lambda i: (0, 0, i, 0))],
        out_shape=[jax.ShapeDtypeStruct((t, d), F32),
                   jax.ShapeDtypeStruct((2 * PEER_HEADS, PEER_NKEYS // 8, t // LANES * 8, LANES), F32)],
        compiler_params=_cparams(("arbitrary",)),
        name="mix",
    )(x2, ynorm, attn, proj, proj, g1, sh2, sc2, ng2, ws, wn, wo, wq, keys)


def _hyperbola():
    return [(i, k) for i in range(PEER_TOPK) for k in range(PEER_TOPK)
            if (i + 1) * (k + 1) <= PEER_TOPK]


def _sort_network(n):
    pairs = []
    p = 1
    while p < n:
        k = p
        while k >= 1:
            for j in range(k % p, n - k, 2 * k):
                for i in range(min(k, n - j - k)):
                    if (i + j) // (2 * p) == (i + j + k) // (2 * p):
                        pairs.append((i + j, i + j + k))
            k //= 2
        p *= 2
    return pairs


def _precedes(va, pa, vb, pb):
    return (va > vb) | ((va == vb) & (pa < pb))


def _compare_exchange(v, p, i, j):
    c = _precedes(v[i], p[i], v[j], p[j])
    v[i], v[j] = jnp.where(c, v[i], v[j]), jnp.where(c, v[j], v[i])
    p[i], p[j] = jnp.where(c, p[i], p[j]), jnp.where(c, p[j], p[i])


def _top16(get, n, sv_ref, sp_ref):
    k = PEER_TOPK
    ng = n // k
    net = _sort_network(k)
    for g in range(ng):
        items = [get(g * k + j) for j in range(k)]
        v = [it[0] for it in items]
        p = [it[1] for it in items]
        for i, j in net:
            _compare_exchange(v, p, i, j)
        if ng == 1:
            return v, p
        for j in range(k):
            sv_ref[g, j] = v[j]
            sp_ref[g, j] = p[j]
    step = 1
    while True:
        for g in range(0, ng, 2 * step):
            v, p = [], []
            for j in range(k):
                xv, xp = sv_ref[g, j], sp_ref[g, j]
                yv, yp = sv_ref[g + step, k - 1 - j], sp_ref[g + step, k - 1 - j]
                c = _precedes(xv, xp, yv, yp)
                v.append(jnp.where(c, xv, yv))
                p.append(jnp.where(c, xp, yp))
            stride = k // 2
            while stride >= 1:
                for i in range(k):
                    if i & stride == 0:
                        _compare_exchange(v, p, i, i + stride)
                stride //= 2
            if 2 * step >= ng:
                return v, p
            for j in range(k):
                sv_ref[g, j] = v[j]
                sp_ref[g, j] = p[j]
        step *= 2


def _select_kernel(s_ref, eidx_ref, gate_ref, sv_ref, sp_ref):
    nk = PEER_NKEYS
    k = PEER_TOPK
    sub = 8
    shape = (sub, LANES)
    ebits = (nk * nk - 1).bit_length()

    def key(z, j):
        return s_ref[z, j // sub, pl.ds(j % sub, sub, stride=sub), :], jnp.full(shape, j, I32)

    tops = []
    for z in range(2):
        tops.append(_top16(functools.partial(key, z), nk, sv_ref, sp_ref))
    (tv0, ti0), (tv1, ti1) = tops

    pairs = _hyperbola()
    npad = -len(pairs) % k

    def cand(j):
        if j >= len(pairs):
            return jnp.full(shape, -jnp.inf, F32), jnp.full(shape, (k * k) << ebits, I32)
        i, kk = pairs[j]
        return tv0[i] + tv1[kk], ((i * k + kk) << ebits) + ti0[i] * nk + ti1[kk]

    top, ids = _top16(cand, len(pairs) + npad, sv_ref, sp_ref)
    ex = [jnp.exp(t - top[0]) for t in top]
    zsum = ex[0]
    for r in range(1, k):
        zsum = zsum + ex[r]
    for r in range(k):
        eidx_ref[0, r] = ids[r] & ((1 << ebits) - 1)
        gate_ref[0, r] = ex[r] / zsum


def _select(scores):
    r = 8
    nk = PEER_NKEYS
    tb = scores.shape[2] // r
    assert tb % r == 0
    oshape = (PEER_HEADS, PEER_TOPK, tb, LANES)
    return pl.pallas_call(
        _select_kernel,
        grid=(tb // r, PEER_HEADS),
        in_specs=[pl.BlockSpec((2, nk // r, r * r, LANES), lambda i, h: (h, 0, i, 0))],
        out_specs=[pl.BlockSpec((1, PEER_TOPK, r, LANES), lambda i, h: (h, 0, i, 0)),
                   pl.BlockSpec((1, PEER_TOPK, r, LANES), lambda i, h: (h, 0, i, 0))],
        out_shape=[jax.ShapeDtypeStruct(oshape, I32), jax.ShapeDtypeStruct(oshape, F32)],
        scratch_shapes=[pltpu.VMEM((nk // PEER_TOPK, PEER_TOPK, r, LANES), F32),
                        pltpu.VMEM((nk // PEER_TOPK, PEER_TOPK, r, LANES), I32)],
        compiler_params=_cparams(("arbitrary", "arbitrary")),
        name="select",
    )(scores)


def _peer_kernel(h_ref, sh_ref, sc_ref, g2_ref, ng_ref, fg_ref, eidx_ref, gate_ref, uv_ref,
                 out_ref, n2_ref, act_ref, wv_ref, w3_ref, acc_ref, *, ns):
    nk = PEER_NKEYS
    half = nk // 2
    hi_mask = jnp.uint32(0xFFFF0000)
    tm = h_ref.shape[0]
    eb = uv_ref.shape[0]
    nblk = eb // nk
    nblk2 = nblk // 2
    s = pl.program_id(1)

    @pl.when(s == 0)
    def _():
        n2_ref[...] = _rms_mod(h_ref[...], ng_ref[...], sh_ref[0], sc_ref[0]).astype(BF16)
        act_ref[...] = jnp.zeros_like(act_ref)

    @pl.when(s < ns)
    def _():
        e = eidx_ref[...]
        ai = jnp.right_shift(e, nk.bit_length() - 1)
        bi = jnp.bitwise_and(e, nk - 1)
        act = act_ref[...]
        n2 = n2_ref[...]
        for a in range(0, nblk, 2):
            sc = lax.dot_general(n2, uv_ref[a * nk:(a + 2) * nk, :], (((1,), (1,)), ((), ())),
                                 preferred_element_type=F32)
            for j in range(2):
                got = jnp.take_along_axis(sc[:, j * nk:(j + 1) * nk], bi, axis=1)
                act = jnp.where(ai == s * nblk + a + j, got, act)
        act_ref[...] = act

    @pl.when(s == ns - 1)
    def _():
        act = act_ref[...]
        gelu = 0.5 * act * (1.0 + lax.erf(act * (2.0 ** -0.5)))
        wv_ref[...] = gate_ref[...] * gelu
        sub = lax.broadcasted_iota(I32, (nk, LANES), 0).astype(F32).astype(BF16)
        zero = jnp.zeros((nk, LANES), BF16)
        one = jnp.ones((nk, LANES), BF16)

        def tok(t, carry):
            e = eidx_ref[pl.ds(t, 1), :]
            w = wv_ref[pl.ds(t, 1), :].astype(BF16)
            ar = jnp.right_shift(e, nk.bit_length() - 1).astype(F32).astype(BF16)
            br = jnp.bitwise_and(e, nk - 1).astype(F32).astype(BF16)
            pt = jnp.where(sub == ar, w, zero)
            qt = jnp.where(sub == br, one, zero)
            wt = lax.dot_general(pt, qt, (((1,), (1,)), ((), ())), preferred_element_type=F32)
            bits = lax.bitcast_convert_type(wt, jnp.uint32)
            word = jnp.right_shift(bits[:half], 16) | (bits[half:] & hi_mask)
            w3_ref[pl.ds(pl.multiple_of(t * W3_PITCH, 8), half), :] = word
            return carry
        lax.fori_loop(0, tm, tok, 0, unroll=32)

    @pl.when(s >= ns)
    def _():
        base = (s - ns) * nblk2
        words = [w3_ref[pl.ds(base + a, tm, stride=W3_PITCH), :] for a in range(nblk2)]
        lo = [lax.bitcast_convert_type(jnp.left_shift(w, 16), F32).astype(BF16) for w in words]
        hi = [lax.bitcast_convert_type(w & hi_mask, F32).astype(BF16) for w in words]
        part = jnp.dot(jnp.concatenate(lo + hi, axis=1), uv_ref[...], preferred_element_type=F32)

        @pl.when(s == ns)
        def _():
            acc_ref[...] = part

        @pl.when(s > ns)
        def _():
            acc_ref[...] += part

    @pl.when(s == 2 * ns - 1)
    def _():
        hh = h_ref[...] + g2_ref[0] * acc_ref[...]
        ms = jnp.mean(hh * hh, axis=-1, keepdims=True)
        out_ref[...] = hh * lax.rsqrt(ms + EPS) * fg_ref[...]


def _peer(h, sh2, sc2, g2, ng2, fg, eidx_t, gate_t, u, v, seq):
    t, d = h.shape
    ne = u.shape[0]
    tm = min(512, seq)
    eb = 4096
    ns = ne // eb
    nj = eidx_t.shape[1]
    v_blocks = v.reshape(2, ns, eb // 2, d).transpose(1, 0, 2, 3).reshape(ne, d)
    uv = jnp.concatenate([u, v_blocks], axis=0).astype(BF16)
    bidx = lambda i, s: ((i * tm) // seq, 0, 0)
    row = lambda i, s: (i, 0)
    const2 = lambda i, s: (0, 0)
    return pl.pallas_call(
        functools.partial(_peer_kernel, ns=ns),
        grid=(t // tm, 2 * ns),
        in_specs=[pl.BlockSpec((tm, d), row),
                  pl.BlockSpec((1, 1, d), bidx), pl.BlockSpec((1, 1, d), bidx),
                  pl.BlockSpec((1, 1, d), bidx),
                  pl.BlockSpec((1, d), const2), pl.BlockSpec((1, d), const2),
                  pl.BlockSpec((tm, nj), row), pl.BlockSpec((tm, nj), row),
                  pl.BlockSpec((eb, d), lambda i, s: (s, 0))],
        out_specs=pl.BlockSpec((tm, d), row),
        out_shape=jax.ShapeDtypeStruct((t, d), F32),
        scratch_shapes=[pltpu.VMEM((tm, d), BF16),
                        pltpu.VMEM((tm, nj), F32),
                        pltpu.VMEM((tm, nj), F32),
                        pltpu.VMEM((tm * W3_PITCH, PEER_NKEYS), jnp.uint32),
                        pltpu.VMEM((tm, d), F32)],
        compiler_params=_cparams(("arbitrary", "arbitrary")),
        name="peer",
    )(h, sh2, sc2, g2, ng2, fg, eidx_t, gate_t, uv)


def _layer(x2, c, bsz, seq, w_ada, b_ada, norm1_g, w_in, conv_w, conv_b, dt_bias_f, dt_bias_b,
           a_log_f, a_log_b, d_skip, ssd_norm_g, w_ssd_br, na_rpb, w_na_br, w_out, norm2_g,
           peer_wq, peer_keys, peer_u, peer_v, out_g):
    t, d = x2.shape
    d_ssd = SSD_HEADS * SSD_HEADDIM
    d_xbc = d_ssd + 2 * SSD_GROUPS * SSD_STATE
    d_na = NA_HEADS * NA_HEADDIM
    assert seq % (SSD_CHUNK * SSD_SUB) == 0 and seq % (GRID_W * NA_QROWS) == 0
    assert seq // GRID_W >= NA_KROWS and d == d_ssd == d_na

    mod = _ada(c, w_ada, b_ada)
    sh1, sc1, g1, sh2, sc2, g2 = [m.reshape(bsz, 1, d) for m in jnp.split(mod, 6, axis=-1)]

    o = np.cumsum([0, d_ssd, d_xbc, SSD_HEADS, SSD_HEADS, 3 * d_na, d, d])
    w_main = jnp.concatenate([w_in[:, o[1]:o[2]], w_in[:, o[0]:o[1]], w_in[:, o[4]:o[7]]],
                             axis=1).astype(BF16)
    w_dt = jnp.pad(w_in[:, o[2]:o[4]], ((0, 0), (0, LANES - 2 * SSD_HEADS))).astype(BF16)
    z_col = d_xbc // d_ssd
    q_col = (d_xbc + d_ssd) // LANES
    gs_col = (d_xbc + d_ssd + 3 * d_na) // d
    proj, dt_raw = _inproj(x2, sh1, sc1, norm1_g.reshape(1, d), w_main, w_dt, seq)

    heads = np.arange(d_ssd) // SSD_HEADDIM

    def e2(off):
        m = np.zeros((LANES, d_ssd), np.float32)
        m[off + heads, np.arange(d_ssd)] = 1.0
        m[off + 64 + heads, np.arange(d_ssd)] = 1.0
        return jnp.asarray(m, BF16)

    def lanes16(vec, off):
        return jnp.zeros((1, LANES), F32).at[0, off:off + SSD_HEADS].set(vec.astype(F32))

    params = {
        "cw": jnp.pad(conv_w.astype(F32), ((0, 8 - CONV_W), (0, 0))),
        "cb": conv_b.astype(F32).reshape(1, d_xbc),
        "dtb_f": lanes16(dt_bias_f, 0), "dtb_b": lanes16(dt_bias_b, SSD_HEADS),
        "a_f": lanes16(-jnp.exp(a_log_f.astype(F32)), 0),
        "a_b": lanes16(-jnp.exp(a_log_b.astype(F32)), SSD_HEADS),
        "e2_f": e2(0), "e2_b": e2(SSD_HEADS),
        "dsk": jnp.repeat(d_skip.astype(F32), SSD_HEADDIM).reshape(1, d_ssd),
        "ng": ssd_norm_g.astype(F32).reshape(1, d_ssd),
    }
    fwd = _ssd(proj, dt_raw, z_col, None, params, bsz, seq, rev=False)
    ynorm = _ssd(proj, dt_raw, z_col, fwd, params, bsz, seq, rev=True)

    bias, var_of_blk, ks_of_blk = _na_bias(na_rpb, seq // GRID_W)
    attn = _na(proj, bias, var_of_blk, ks_of_blk, q_col, bsz, seq)

    h, scores = _mix(x2, ynorm, attn, proj, gs_col, g1, sh2, sc2, norm2_g.reshape(1, d),
                     w_ssd_br.astype(BF16), w_na_br.astype(BF16), w_out.astype(BF16),
                     peer_wq.astype(BF16), peer_keys.astype(BF16), seq)

    eidx, gate = _select(scores)
    nj = PEER_HEADS * PEER_TOPK
    eidx_t = eidx.reshape(nj, t).T
    gate_t = gate.reshape(nj, t).T
    return _peer(h, sh2, sc2, g2, norm2_g.reshape(1, d), out_g, eidx_t, gate_t,
                 peer_u.astype(BF16), peer_v.astype(BF16), seq)


def kernel(x, c, w_ada, b_ada, norm1_g, w_in, conv_w, conv_b, dt_bias_f, dt_bias_b, a_log_f, a_log_b,
           d_skip, ssd_norm_g, w_ssd_br, na_rpb, w_na_br, w_out, norm2_g, peer_wq, peer_keys, peer_u,
           peer_v, final_g):
    bsz, seq, d = x.shape
    depth = w_ada.shape[0]
    assert depth == 1, "the final RMSNorm is fused into the last layer's PEER kernel"
    i = 0
    out = _layer(x.reshape(bsz * seq, d), c, bsz, seq, w_ada[i], b_ada[i], norm1_g[i], w_in[i],
                 conv_w[i], conv_b[i], dt_bias_f[i], dt_bias_b[i], a_log_f[i], a_log_b[i], d_skip[i],
                 ssd_norm_g[i], w_ssd_br[i], na_rpb[i], w_na_br[i], w_out[i], norm2_g[i],
                 peer_wq[i], peer_keys[i], peer_u[i], peer_v[i], final_g.reshape(1, d))
    return out.reshape(bsz, seq, d)
```

```python
import functools

import numpy as np
import jax
import jax.numpy as jnp
from jax import lax
from jax.experimental import pallas as pl
from jax.experimental.pallas import tpu as pltpu

F32 = jnp.float32
BF16 = jnp.bfloat16
I32 = jnp.int32

EPS = 1e-6
GRID_W = 64
SSD_HEADS = 16
SSD_HEADDIM = 64
SSD_GROUPS = 4
SSD_STATE = 128
SSD_CHUNK = 128
SSD_SUB = 2
CONV_W = 5
NA_HEADS = 16
NA_HEADDIM = 64
NA_KH = 8
NA_KW = 16
NA_QROWS = 4
NA_KROWS = 12
NA_QB = 16
NA_KB = NA_QB + NA_KW
PEER_HEADS = 8
PEER_NKEYS = 128
PEER_TOPK = 16
PEER_DHALF = 128
W3_PITCH = PEER_NKEYS // 2 + 8

LANES = 128
HALO = 16
VMEM_LIMIT = 56 * 1024 * 1024


def _cparams(sem):
    return pltpu.CompilerParams(dimension_semantics=sem, vmem_limit_bytes=VMEM_LIMIT)


def _sigmoid(x):
    return 0.5 * jnp.tanh(0.5 * x) + 0.5


def _rms_mod(x, g, shift, scale):
    ms = jnp.mean(x * x, axis=-1, keepdims=True)
    y = x * lax.rsqrt(ms + EPS) * g
    return y * (1.0 + scale) + shift


def _ada_kernel(c_ref, w_ref, b_ref, o_ref):
    c = c_ref[...]
    sc = c * _sigmoid(c)
    o_ref[...] = jnp.dot(sc, w_ref[...], preferred_element_type=F32,
                         precision=lax.Precision.HIGHEST) + b_ref[...]


def _ada(c, w, b):
    bsz, d = c.shape
    n = w.shape[1]
    tn = 1024
    return pl.pallas_call(
        _ada_kernel,
        grid=(n // tn,),
        in_specs=[pl.BlockSpec((bsz, d), lambda j: (0, 0)),
                  pl.BlockSpec((d, tn), lambda j: (0, j)),
                  pl.BlockSpec((1, tn), lambda j: (0, j))],
        out_specs=pl.BlockSpec((bsz, tn), lambda j: (0, j)),
        out_shape=jax.ShapeDtypeStruct((bsz, n), F32),
        compiler_params=_cparams(("arbitrary",)),
        name="ada",
    )(c, w, b.reshape(1, n))


def _inproj_kernel(x_ref, sh_ref, sc_ref, g_ref, w_ref, wdt_ref, proj_ref, dt_ref):
    n1 = _rms_mod(x_ref[...], g_ref[...], sh_ref[0], sc_ref[0]).astype(BF16)
    dt_ref[...] = jnp.dot(n1, wdt_ref[...], preferred_element_type=F32)
    tn = 1024
    for j in range(w_ref.shape[1] // tn):
        proj_ref[:, j * tn:(j + 1) * tn] = jnp.dot(
            n1, w_ref[:, j * tn:(j + 1) * tn], preferred_element_type=F32).astype(BF16)


def _inproj(x2, sh, sc, g, w, wdt, seq):
    t, d = x2.shape
    n = w.shape[1]
    tm = min(512, seq)
    bidx = lambda i: ((i * tm) // seq, 0, 0)
    const2 = lambda i: (0, 0)
    return pl.pallas_call(
        _inproj_kernel,
        grid=(t // tm,),
        in_specs=[pl.BlockSpec((tm, d), lambda i: (i, 0)),
                  pl.BlockSpec((1, 1, d), bidx),
                  pl.BlockSpec((1, 1, d), bidx),
                  pl.BlockSpec((1, d), const2),
                  pl.BlockSpec((d, n), const2, pipeline_mode=pl.Buffered(1)),
                  pl.BlockSpec((d, LANES), const2, pipeline_mode=pl.Buffered(1))],
        out_specs=[pl.BlockSpec((tm, n), lambda i: (i, 0)),
                   pl.BlockSpec((tm, LANES), lambda i: (i, 0))],
        out_shape=[jax.ShapeDtypeStruct((t, n), BF16),
                   jax.ShapeDtypeStruct((t, LANES), F32)],
        compiler_params=_cparams(("arbitrary",)),
        name="inproj",
    )(x2, sh, sc, g, w, wdt)


def _expand_heads(v, e2_ref):
    hi = v.astype(BF16).astype(F32)
    comb = (hi + pltpu.roll(v - hi, 64, 1)).astype(BF16)
    return jnp.dot(comb, e2_ref[...], preferred_element_type=F32)


def _ssd_conv(main, ext, cw_ref, cb_ref):
    ck = main.shape[0]
    d_ssd = SSD_HEADS * SSD_HEADDIM
    erow = lax.broadcasted_iota(I32, (ck, ck + 2 * HALO), 0)
    ecol = lax.broadcasted_iota(I32, (ck, ck + 2 * HALO), 1)
    mid = CONV_W // 2
    acc = cb_ref[...] + main.astype(F32) * cw_ref[mid:mid + 1, :]
    for w in range(CONV_W):
        if w != mid:
            shift = jnp.where(ecol == erow + (HALO + w - mid), 1.0, 0.0).astype(BF16)
            acc = acc + jnp.dot(shift, ext, preferred_element_type=F32) * cw_ref[w:w + 1, :]
    xbc = acc * _sigmoid(acc)
    return xbc[:, :d_ssd], xbc[:, d_ssd:].astype(BF16)


def _ssd_scan(xs, bc, dt_raw, dtb_ref, a_ref, e2_ref, st_ref, rev):
    ck = SSD_CHUNK
    d_ssd = SSD_HEADS * SSD_HEADDIM
    d_bc = SSD_GROUPS * SSD_STATE
    gw = d_ssd // SSD_GROUPS
    lane = lax.broadcasted_iota(I32, (ck, LANES), 1)
    row = lax.broadcasted_iota(I32, (ck, LANES), 0)
    off = SSD_HEADS if rev else 0
    valid = (lane >= off) & (lane < off + SSD_HEADS)
    dtr = dt_raw + dtb_ref[...]
    dt = jnp.where(valid, jnp.maximum(dtr, 0.0) + jnp.log1p(jnp.exp(-jnp.abs(dtr))), 0.0)
    da = dt * a_ref[...]
    tri = (lane >= row) if rev else (lane <= row)
    trib = jnp.where(tri, 1.0, 0.0).astype(BF16)
    hi = da.astype(BF16)
    r1 = da - hi.astype(F32)
    mid = r1.astype(BF16)
    lo = (r1 - mid.astype(F32)).astype(BF16)
    cs = (jnp.dot(trib, hi, preferred_element_type=F32)
          + jnp.dot(trib, mid, preferred_element_type=F32)
          + jnp.dot(trib, lo, preferred_element_type=F32))
    cst = cs.T
    tot_row = 0 if rev else ck - 1
    tot = cs[tot_row:tot_row + 1, :]
    dec = jnp.where(valid, jnp.exp(tot - cs), 0.0)
    ecs = jnp.where(valid, jnp.exp(cs), 0.0)
    ecs_e = _expand_heads(ecs, e2_ref)
    xdt = xs * _expand_heads(dt, e2_ref)
    xdec = (xs * _expand_heads(dt * dec, e2_ref)).astype(BF16)

    lo_half = lax.broadcasted_iota(I32, (ck, LANES), 1) < SSD_HEADDIM
    hpg = SSD_HEADS // SSD_GROUPS
    ys = []
    for g in range(SSD_GROUPS):
        bg = bc[:, g * SSD_STATE:(g + 1) * SSD_STATE]
        cg = bc[:, d_bc + g * SSD_STATE:d_bc + (g + 1) * SSD_STATE]
        cbg = lax.dot_general(cg, bg, (((1,), (1,)), ((), ())), preferred_element_type=F32)
        yg = []
        for q in range(hpg // 2):
            ls = []
            for e in range(2):
                col = off + g * hpg + q * 2 + e
                dlt = cs[:, col:col + 1] - cst[col:col + 1, :]
                seg = jnp.exp(jnp.where(tri, dlt, -jnp.inf))
                ls.append((cbg * seg).astype(BF16))
            blk = g * (hpg // 2) + q
            xq = xdt[:, blk * LANES:(blk + 1) * LANES]
            rhs = jnp.concatenate([jnp.where(lo_half, xq, 0.0), jnp.where(lo_half, 0.0, xq)],
                                  axis=0).astype(BF16)
            yg.append(jnp.dot(jnp.concatenate(ls, axis=1), rhs, preferred_element_type=F32))
        y_diag = jnp.concatenate(yg, axis=1)
        st = st_ref[g]
        e_g = ecs_e[:, g * gw:(g + 1) * gw]
        y_off = jnp.dot(cg, st.astype(BF16), preferred_element_type=F32) * e_g
        ys.append(y_diag + y_off)
        upd = jnp.dot(bg.astype(F32).T.astype(BF16), xdec[:, g * gw:(g + 1) * gw],
                      preferred_element_type=F32)
        st_ref[g] = st * e_g[tot_row:tot_row + 1, :] + upd
    return jnp.concatenate(ys, axis=1)


def _ssd_kernel(*refs, rev):
    if rev:
        (xs_ref, bc_ref, dt_ref, z_ref, yf_ref, dtb_ref, a_ref, e2_ref, dsk_ref, ng_ref,
         out_ref, st_ref) = refs
    else:
        (xm_ref, xp_ref, xn_ref, dt_ref, cw_ref, cb_ref, dtb_ref, a_ref, e2_ref,
         out_ref, xs_out_ref, bc_out_ref, st_ref) = refs
    ck = SSD_CHUNK
    c = pl.program_id(1)
    nc = pl.num_programs(1)
    ce = nc - 1 - c if rev else c

    @pl.when(c == 0)
    def _():
        st_ref[...] = jnp.zeros_like(st_ref)

    rows = [slice(j * ck, (j + 1) * ck) for j in range(SSD_SUB)]
    if rev:
        conv = [(xs_ref[r, :], bc_ref[r, :]) for r in rows]
    else:
        main = xm_ref[...]
        zero_halo = jnp.zeros((HALO, main.shape[1]), main.dtype)
        ext = jnp.concatenate([jnp.where(ce == 0, zero_halo, xp_ref[...]), main,
                               jnp.where(ce == nc - 1, zero_halo, xn_ref[...])], axis=0)
        conv = [_ssd_conv(main[r], ext[r.start:r.stop + 2 * HALO], cw_ref, cb_ref) for r in rows]
        for r, (xs, bc) in zip(rows, conv):
            xs_out_ref[r, :] = xs
            bc_out_ref[r, :] = bc

    for j in (reversed(range(SSD_SUB)) if rev else range(SSD_SUB)):
        r = rows[j]
        xs, bc = conv[j]
        y = _ssd_scan(xs, bc, dt_ref[r, :], dtb_ref, a_ref, e2_ref, st_ref, rev)
        if rev:
            y = yf_ref[r, :] + y + dsk_ref[...] * xs
            z = z_ref[r, :].astype(F32)
            u = y * (z * _sigmoid(z))
            ms = jnp.mean(u * u, axis=-1, keepdims=True)
            out_ref[r, :] = (u * lax.rsqrt(ms + EPS) * ng_ref[...]).astype(BF16)
        else:
            out_ref[r, :] = y


def _ssd(proj, dt_raw, z_col, fwd, params, bsz, seq, rev):
    t = proj.shape[0]
    ck = SSD_CHUNK * SSD_SUB
    nc = seq // ck
    d_xbc = params["cw"].shape[1]
    d_ssd = SSD_HEADS * SSD_HEADDIM
    d_bc = d_xbc - d_ssd
    per = ck // HALO
    last_halo = t // HALO - 1

    def ce_of(c):
        return nc - 1 - c if rev else c

    main = lambda b, c: (b * nc + ce_of(c), 0)
    prev = lambda b, c: (jnp.maximum((b * nc + ce_of(c)) * per - 1, 0), 0)
    nxt = lambda b, c: (jnp.minimum((b * nc + ce_of(c)) * per + per, last_halo), 0)
    const2 = lambda b, c: (0, 0)
    d = "b" if rev else "f"
    dir_specs = [pl.BlockSpec((1, LANES), const2), pl.BlockSpec((1, LANES), const2),
                 pl.BlockSpec((LANES, d_ssd), const2)]
    dir_args = [params["dtb_" + d], params["a_" + d], params["e2_" + d]]
    if rev:
        yf, xs, bc = fwd
        in_specs = [pl.BlockSpec((ck, d_ssd), main), pl.BlockSpec((ck, d_bc), main),
                    pl.BlockSpec((ck, LANES), main),
                    pl.BlockSpec((ck, d_ssd), lambda b, c: (b * nc + ce_of(c), z_col)),
                    pl.BlockSpec((ck, d_ssd), main)] + dir_specs + [
                        pl.BlockSpec((1, d_ssd), const2), pl.BlockSpec((1, d_ssd), const2)]
        args = [xs, bc, dt_raw, proj, yf] + dir_args + [params["dsk"], params["ng"]]
        out_specs = pl.BlockSpec((ck, d_ssd), main)
        out_shape = jax.ShapeDtypeStruct((t, d_ssd), BF16)
    else:
        in_specs = [pl.BlockSpec((ck, d_xbc), main), pl.BlockSpec((HALO, d_xbc), prev),
                    pl.BlockSpec((HALO, d_xbc), nxt), pl.BlockSpec((ck, LANES), main),
                    pl.BlockSpec((8, d_xbc), const2), pl.BlockSpec((1, d_xbc), const2)] + dir_specs
        args = [proj, proj, proj, dt_raw, params["cw"], params["cb"]] + dir_args
        out_specs = [pl.BlockSpec((ck, d_ssd), main), pl.BlockSpec((ck, d_ssd), main),
                     pl.BlockSpec((ck, d_bc), main)]
        out_shape = [jax.ShapeDtypeStruct((t, d_ssd), F32), jax.ShapeDtypeStruct((t, d_ssd), F32),
                     jax.ShapeDtypeStruct((t, d_bc), BF16)]
    return pl.pallas_call(
        functools.partial(_ssd_kernel, rev=rev),
        grid=(bsz, nc),
        in_specs=in_specs,
        out_specs=out_specs,
        out_shape=out_shape,
        scratch_shapes=[pltpu.VMEM((SSD_GROUPS, SSD_STATE, d_ssd // SSD_GROUPS), F32)],
        compiler_params=_cparams(("arbitrary", "arbitrary")),
        name="ssd_bwd" if rev else "ssd_fwd",
    )(*args)


def _na_plan(rows):
    nblk = rows // NA_QROWS
    variants, var_of_blk, ks_of_blk = [], [], []
    for i in range(nblk):
        ks = int(np.clip(i * NA_QROWS - NA_KH // 2, 0, rows - NA_KROWS))
        drow = np.zeros((NA_QROWS, NA_KROWS), np.int64)
        ok = np.zeros((NA_QROWS, NA_KROWS), bool)
        for dr in range(NA_QROWS):
            r = i * NA_QROWS + dr
            rs = int(np.clip(r - NA_KH // 2, 0, rows - NA_KH))
            for j in range(NA_KH):
                w = rs + j - ks
                drow[dr, w] = rs + j - r + NA_KH - 1
                ok[dr, w] = True
        key = (drow.tobytes(), ok.tobytes())
        for vi, (k2, _, _) in enumerate(variants):
            if k2 == key:
                var_of_blk.append(vi)
                break
        else:
            var_of_blk.append(len(variants))
            variants.append((key, drow, ok))
        ks_of_blk.append(ks)
    return [(d, o) for _, d, o in variants], var_of_blk, ks_of_blk


def _na_blk_start():
    ncb = GRID_W // NA_QB
    return [int(v) for v in np.clip(np.arange(ncb) * NA_QB - NA_KW // 2, 0, GRID_W - NA_KB)]


def _na_bias(rpb, rows):
    variants, var_of_blk, ks_of_blk = _na_plan(rows)
    nh = rpb.shape[0]
    cols = np.arange(GRID_W)
    win_start = np.clip(cols - NA_KW // 2, 0, GRID_W - NA_KW)
    in_win = (cols[None, :] >= win_start[:, None]) & (cols[None, :] < win_start[:, None] + NA_KW)
    dcol = np.clip(cols[None, :] - cols[:, None] + NA_KW - 1, 0, 2 * NA_KW - 2)
    onehot = (dcol[:, :, None] == np.arange(2 * NA_KW - 1)).astype(np.float32)
    band = jnp.einsum("hab,qkb->haqk", rpb.astype(F32), onehot, precision=lax.Precision.HIGHEST)
    band = jnp.where(in_win[None, None], band, -jnp.inf)
    neg = jnp.full((nh, NA_QB, NA_KB), -jnp.inf, F32)
    tabs = []
    for drow, ok in variants:
        per_cb = []
        for m, bs in enumerate(_na_blk_start()):
            qrows = [jnp.concatenate(
                [band[:, drow[dr, w], m * NA_QB:(m + 1) * NA_QB, bs:bs + NA_KB] if ok[dr, w] else neg
                 for w in range(NA_KROWS)], axis=2) for dr in range(NA_QROWS)]
            per_cb.append(jnp.concatenate(qrows, axis=1))
        tabs.append(jnp.stack(per_cb, axis=1))
    tab = jnp.stack(tabs)
    nv, _, ncb, mq, nk = tab.shape
    tab = tab.reshape(nv, nh // 2, 2, ncb, mq, nk).transpose(0, 1, 3, 2, 4, 5)
    return tab.reshape(nv, nh // 2, ncb, 2 * mq, nk), var_of_blk, ks_of_blk


def _na_kernel(var_ref, ks_ref, q_ref, k_ref, v_ref, bias_ref, o_ref, kf_ref, vf_ref):
    mq = NA_QROWS * NA_QB
    nblk = q_ref.shape[0] // (NA_QROWS * GRID_W)
    lo_half = lax.broadcasted_iota(I32, (mq, LANES), 1) < NA_HEADDIM
    kf_ref[...] = k_ref[...].astype(F32)
    vf_ref[...] = v_ref[...].astype(F32)

    def blk(i, carry):
        var = var_ref[i]
        ks = ks_ref[i]
        starts = _na_blk_start()
        q_at = [[pl.multiple_of((i * NA_QROWS + dr) * GRID_W + m * NA_QB, NA_QB)
                 for dr in range(NA_QROWS)] for m in range(len(starts))]
        k_at = [[pl.multiple_of((ks + w) * GRID_W + bs, 8) for w in range(NA_KROWS)]
                for bs in starts]
        scores = []
        for m in range(len(starts)):
            qb = jnp.concatenate([q_ref[pl.ds(o, NA_QB), :] for o in q_at[m]], axis=0)
            qb = qb * (NA_HEADDIM ** -0.5)
            zero = jnp.zeros_like(qb)
            q2 = jnp.concatenate([jnp.where(lo_half, qb, zero), jnp.where(lo_half, zero, qb)], axis=0)
            kb = jnp.concatenate([kf_ref[pl.ds(o, NA_KB), :] for o in k_at[m]], axis=0).astype(BF16)
            s = lax.dot_general(q2, kb, (((1,), (1,)), ((), ())), preferred_element_type=F32)
            scores.append(s + bias_ref[var, 0, m])
        outs = []
        for m, s in enumerate(scores):
            p = jnp.exp(s - jnp.max(s, axis=-1, keepdims=True))
            l = jnp.sum(p, axis=-1, keepdims=True)
            vb = jnp.concatenate([vf_ref[pl.ds(o, NA_KB), :] for o in k_at[m]], axis=0).astype(BF16)
            outs.append(jnp.dot(p.astype(BF16), vb, preferred_element_type=F32) / l)
        for m, o2 in enumerate(outs):
            ob = jnp.where(lo_half, o2[:mq], o2[mq:]).astype(BF16)
            for dr, o in enumerate(q_at[m]):
                o_ref[pl.ds(o, NA_QB), :] = ob[dr * NA_QB:(dr + 1) * NA_QB]
        return carry

    lax.fori_loop(0, nblk, blk, 0, unroll=4)


def _na(proj, bias, var_of_blk, ks_of_blk, q_col, bsz, seq):
    t = proj.shape[0]
    npair = NA_HEADS // 2
    grid_spec = pltpu.PrefetchScalarGridSpec(
        num_scalar_prefetch=2,
        grid=(bsz, npair),
        in_specs=[pl.BlockSpec((seq, LANES), lambda b, p, *_: (b, q_col + p)),
                  pl.BlockSpec((seq, LANES), lambda b, p, *_: (b, q_col + npair + p)),
                  pl.BlockSpec((seq, LANES), lambda b, p, *_: (b, q_col + 2 * npair + p)),
                  pl.BlockSpec((bias.shape[0], 1) + bias.shape[2:], lambda b, p, *_: (0, p, 0, 0, 0))],
        out_specs=pl.BlockSpec((seq, LANES), lambda b, p, *_: (b, p)),
        scratch_shapes=[pltpu.VMEM((seq, LANES), F32), pltpu.VMEM((seq, LANES), F32)],
    )
    return pl.pallas_call(
        _na_kernel,
        grid_spec=grid_spec,
        out_shape=jax.ShapeDtypeStruct((t, NA_HEADS * NA_HEADDIM), BF16),
        compiler_params=_cparams(("arbitrary", "arbitrary")),
        name="na",
    )(jnp.asarray(var_of_blk, I32), jnp.asarray(ks_of_blk, I32), proj, proj, proj, bias)


def _mix_kernel(x_ref, yn_ref, at_ref, gs_ref, gn_ref, g1_ref, sh_ref, sc_ref, ng_ref,
                ws_ref, wn_ref, wo_ref, wq_ref, keys_ref, h_ref, s_ref):
    y1 = jnp.dot(yn_ref[...], ws_ref[...], preferred_element_type=F32)
    y2 = jnp.dot(at_ref[...], wn_ref[...], preferred_element_type=F32)
    mixed = _sigmoid(gs_ref[...].astype(F32)) * y1 + _sigmoid(gn_ref[...].astype(F32)) * y2
    h = x_ref[...] + g1_ref[0] * jnp.dot(mixed.astype(BF16), wo_ref[...], preferred_element_type=F32)
    h_ref[...] = h
    n2 = _rms_mod(h, ng_ref[...], sh_ref[0], sc_ref[0]).astype(BF16)
    q = jnp.dot(n2, wq_ref[...], preferred_element_type=F32).astype(BF16)
    sub = 8
    for hz in range(2 * PEER_HEADS):
        qs = q[:, hz * PEER_DHALF:(hz + 1) * PEER_DHALF]
        sc = lax.dot_general(keys_ref[hz % 2], qs, (((1,), (1,)), ((), ())),
                             preferred_element_type=F32)
        for kq in range(PEER_NKEYS // sub):
            for tb in range(sc.shape[1] // LANES):
                s_ref[hz, kq, tb * sub:(tb + 1) * sub, :] = sc[kq * sub:(kq + 1) * sub,
                                                               tb * LANES:(tb + 1) * LANES]


def _mix(x2, ynorm, attn, proj, gs_col, g1, sh2, sc2, ng2, ws, wn, wo, wq, keys, seq):
    t, d = x2.shape
    tm = min(512, seq)
    bidx = lambda i: ((i * tm) // seq, 0, 0)
    row = lambda i: (i, 0)
    const2 = lambda i: (0, 0)
    return pl.pallas_call(
        _mix_kernel,
        grid=(t // tm,),
        in_specs=[pl.BlockSpec((tm, d), row), pl.BlockSpec((tm, d), row), pl.BlockSpec((tm, d), row),
                  pl.BlockSpec((tm, d), lambda i: (i, gs_col)),
                  pl.BlockSpec((tm, d), lambda i: (i, gs_col + 1)),
                  pl.BlockSpec((1, 1, d), bidx), pl.BlockSpec((1, 1, d), bidx),
                  pl.BlockSpec((1, 1, d), bidx), pl.BlockSpec((1, d), const2),
                  pl.BlockSpec((d, d), const2, pipeline_mode=pl.Buffered(1)),
                  pl.BlockSpec((d, d), const2, pipeline_mode=pl.Buffered(1)),
                  pl.BlockSpec((d, d), const2, pipeline_mode=pl.Buffered(1)),
                  pl.BlockSpec(wq.shape, const2, pipeline_mode=pl.Buffered(1)),
                  pl.BlockSpec(keys.shape, lambda i: (0, 0, 0), pipeline_mode=pl.Buffered(1))],
        out_specs=[pl.BlockSpec((tm, d), row),
                   pl.BlockSpec((2 * PEER_HEADS, PEER_NKEYS // 8, tm // LANES * 8, LANES),
                                lambda i: (0, 0, i, 0))],
        out_shape=[jax.ShapeDtypeStruct((t, d), F32),
                   jax.ShapeDtypeStruct((2 * PEER_HEADS, PEER_NKEYS // 8, t // LANES * 8, LANES), F32)],
        compiler_params=_cparams(("arbitrary",)),
        name="mix",
    )(x2, ynorm, attn, proj, proj, g1, sh2, sc2, ng2, ws, wn, wo, wq, keys)


def _hyperbola():
    return [(i, k) for i in range(PEER_TOPK) for k in range(PEER_TOPK)
            if (i + 1) * (k + 1) <= PEER_TOPK]


def _sort_network(n):
    pairs = []
    p = 1
    while p < n:
        k = p
        while k >= 1:
            for j in range(k % p, n - k, 2 * k):
                for i in range(min(k, n - j - k)):
                    if (i + j) // (2 * p) == (i + j + k) // (2 * p):
                        pairs.append((i + j, i + j + k))
            k //= 2
        p *= 2
    return pairs


def _precedes(va, pa, vb, pb):
    return (va > vb) | ((va == vb) & (pa < pb))


def _compare_exchange(v, p, i, j):
    c = _precedes(v[i], p[i], v[j], p[j])
    v[i], v[j] = jnp.where(c, v[i], v[j]), jnp.where(c, v[j], v[i])
    p[i], p[j] = jnp.where(c, p[i], p[j]), jnp.where(c, p[j], p[i])


def _top16(get, n, sv_ref, sp_ref):
    k = PEER_TOPK
    ng = n // k
    net = _sort_network(k)
    for g in range(ng):
        items = [get(g * k + j) for j in range(k)]
        v = [it[0] for it in items]
        p = [it[1] for it in items]
        for i, j in net:
            _compare_exchange(v, p, i, j)
        if ng == 1:
            return v, p
        for j in range(k):
            sv_ref[g, j] = v[j]
            sp_ref[g, j] = p[j]
    step = 1
    while True:
        for g in range(0, ng, 2 * step):
            v, p = [], []
            for j in range(k):
                xv, xp = sv_ref[g, j], sp_ref[g, j]
                yv, yp = sv_ref[g + step, k - 1 - j], sp_ref[g + step, k - 1 - j]
                c = _precedes(xv, xp, yv, yp)
                v.append(jnp.where(c, xv, yv))
                p.append(jnp.where(c, xp, yp))
            stride = k // 2
            while stride >= 1:
                for i in range(k):
                    if i & stride == 0:
                        _compare_exchange(v, p, i, i + stride)
                stride //= 2
            if 2 * step >= ng:
                return v, p
            for j in range(k):
                sv_ref[g, j] = v[j]
                sp_ref[g, j] = p[j]
        step *= 2


def _select_kernel(s_ref, eidx_ref, gate_ref, sv_ref, sp_ref):
    nk = PEER_NKEYS
    k = PEER_TOPK
    sub = 8
    shape = (sub, LANES)
    ebits = (nk * nk - 1).bit_length()

    def key(z, j):
        return s_ref[z, j // sub, pl.ds(j % sub, sub, stride=sub), :], jnp.full(shape, j, I32)

    tops = []
    for z in range(2):
        tops.append(_top16(functools.partial(key, z), nk, sv_ref, sp_ref))
    (tv0, ti0), (tv1, ti1) = tops

    pairs = _hyperbola()
    npad = -len(pairs) % k

    def cand(j):
        if j >= len(pairs):
            return jnp.full(shape, -jnp.inf, F32), jnp.full(shape, (k * k) << ebits, I32)
        i, kk = pairs[j]
        return tv0[i] + tv1[kk], ((i * k + kk) << ebits) + ti0[i] * nk + ti1[kk]

    top, ids = _top16(cand, len(pairs) + npad, sv_ref, sp_ref)
    ex = [jnp.exp(t - top[0]) for t in top]
    zsum = ex[0]
    for r in range(1, k):
        zsum = zsum + ex[r]
    for r in range(k):
        eidx_ref[0, r] = ids[r] & ((1 << ebits) - 1)
        gate_ref[0, r] = ex[r] / zsum


def _select(scores):
    r = 8
    nk = PEER_NKEYS
    tb = scores.shape[2] // r
    assert tb % r == 0
    oshape = (PEER_HEADS, PEER_TOPK, tb, LANES)
    return pl.pallas_call(
        _select_kernel,
        grid=(tb // r, PEER_HEADS),
        in_specs=[pl.BlockSpec((2, nk // r, r * r, LANES), lambda i, h: (h, 0, i, 0))],
        out_specs=[pl.BlockSpec((1, PEER_TOPK, r, LANES), lambda i, h: (h, 0, i, 0)),
                   pl.BlockSpec((1, PEER_TOPK, r, LANES), lambda i, h: (h, 0, i, 0))],
        out_shape=[jax.ShapeDtypeStruct(oshape, I32), jax.ShapeDtypeStruct(oshape, F32)],
        scratch_shapes=[pltpu.VMEM((nk // PEER_TOPK, PEER_TOPK, r, LANES), F32),
                        pltpu.VMEM((nk // PEER_TOPK, PEER_TOPK, r, LANES), I32)],
        compiler_params=_cparams(("arbitrary", "arbitrary")),
        name="select",
    )(scores)


def _peer_kernel(h_ref, sh_ref, sc_ref, g2_ref, ng_ref, fg_ref, eidx_ref, gate_ref, uv_ref,
                 out_ref, n2_ref, act_ref, wv_ref, w3_ref, acc_ref, *, ns):
    nk = PEER_NKEYS
    half = nk // 2
    hi_mask = jnp.uint32(0xFFFF0000)
    tm = h_ref.shape[0]
    eb = uv_ref.shape[0]
    nblk = eb // nk
    nblk2 = nblk // 2
    s = pl.program_id(1)

    @pl.when(s == 0)
    def _():
        n2_ref[...] = _rms_mod(h_ref[...], ng_ref[...], sh_ref[0], sc_ref[0]).astype(BF16)
        act_ref[...] = jnp.zeros_like(act_ref)

    @pl.when(s < ns)
    def _():
        e = eidx_ref[...]
        ai = jnp.right_shift(e, nk.bit_length() - 1)
        bi = jnp.bitwise_and(e, nk - 1)
        n2 = n2_ref[...]
        for a in range(0, nblk, 2):
            sc = lax.dot_general(n2, uv_ref[a * nk:(a + 2) * nk, :], (((1,), (1,)), ((), ())),
                                 preferred_element_type=F32)
            act = act_ref[...]
            for j in range(2):
                got = jnp.take_along_axis(sc[:, j * nk:(j + 1) * nk], bi, axis=1)
                act = jnp.where(ai == s * nblk + a + j, got, act)
            act_ref[...] = act

    @pl.when(s == ns - 1)
    def _():
        act = act_ref[...]
        gelu = 0.5 * act * (1.0 + lax.erf(act * (2.0 ** -0.5)))
        wv_ref[...] = gate_ref[...] * gelu
        sub = lax.broadcasted_iota(I32, (nk, LANES), 0).astype(F32).astype(BF16)
        zero = jnp.zeros((nk, LANES), BF16)
        one = jnp.ones((nk, LANES), BF16)

        def tok(t, carry):
            e = eidx_ref[pl.ds(t, 1), :]
            w = wv_ref[pl.ds(t, 1), :].astype(BF16)
            ar = jnp.right_shift(e, nk.bit_length() - 1).astype(F32).astype(BF16)
            br = jnp.bitwise_and(e, nk - 1).astype(F32).astype(BF16)
            pt = jnp.where(sub == ar, w, zero)
            qt = jnp.where(sub == br, one, zero)
            wt = lax.dot_general(pt, qt, (((1,), (1,)), ((), ())), preferred_element_type=F32)
            bits = lax.bitcast_convert_type(wt, jnp.uint32)
            word = jnp.right_shift(bits[:half], 16) | (bits[half:] & hi_mask)
            w3_ref[pl.ds(pl.multiple_of(t * W3_PITCH, 8), half), :] = word
            return carry
        lax.fori_loop(0, tm, tok, 0, unroll=32)

    @pl.when(s >= ns)
    def _():
        base = (s - ns) * nblk2
        words = [w3_ref[pl.ds(base + a, tm, stride=W3_PITCH), :] for a in range(nblk2)]
        lo = [lax.bitcast_convert_type(jnp.left_shift(w, 16), F32).astype(BF16) for w in words]
        hi = [lax.bitcast_convert_type(w & hi_mask, F32).astype(BF16) for w in words]
        part = jnp.dot(jnp.concatenate(lo + hi, axis=1), uv_ref[...], preferred_element_type=F32)

        @pl.when(s == ns)
        def _():
            acc_ref[...] = part

        @pl.when(s > ns)
        def _():
            acc_ref[...] += part

    @pl.when(s == 2 * ns - 1)
    def _():
        hh = h_ref[...] + g2_ref[0] * acc_ref[...]
        ms = jnp.mean(hh * hh, axis=-1, keepdims=True)
        out_ref[...] = hh * lax.rsqrt(ms + EPS) * fg_ref[...]


def _peer(h, sh2, sc2, g2, ng2, fg, eidx_t, gate_t, u, v, seq):
    t, d = h.shape
    ne = u.shape[0]
    tm = min(512, seq)
    eb = 4096
    ns = ne // eb
    nj = eidx_t.shape[1]
    v_blocks = v.reshape(2, ns, eb // 2, d).transpose(1, 0, 2, 3).reshape(ne, d)
    uv = jnp.concatenate([u, v_blocks], axis=0).astype(BF16)
    bidx = lambda i, s: ((i * tm) // seq, 0, 0)
    row = lambda i, s: (i, 0)
    const2 = lambda i, s: (0, 0)
    return pl.pallas_call(
        functools.partial(_peer_kernel, ns=ns),
        grid=(t // tm, 2 * ns),
        in_specs=[pl.BlockSpec((tm, d), row),
                  pl.BlockSpec((1, 1, d), bidx), pl.BlockSpec((1, 1, d), bidx),
                  pl.BlockSpec((1, 1, d), bidx),
                  pl.BlockSpec((1, d), const2), pl.BlockSpec((1, d), const2),
                  pl.BlockSpec((tm, nj), row), pl.BlockSpec((tm, nj), row),
                  pl.BlockSpec((eb, d), lambda i, s: (s, 0))],
        out_specs=pl.BlockSpec((tm, d), row),
        out_shape=jax.ShapeDtypeStruct((t, d), F32),
        scratch_shapes=[pltpu.VMEM((tm, d), BF16),
                        pltpu.VMEM((tm, nj), F32),
                        pltpu.VMEM((tm, nj), F32),
                        pltpu.VMEM((tm * W3_PITCH, PEER_NKEYS), jnp.uint32),
                        pltpu.VMEM((tm, d), F32)],
        compiler_params=_cparams(("arbitrary", "arbitrary")),
        name="peer",
    )(h, sh2, sc2, g2, ng2, fg, eidx_t, gate_t, uv)


def _layer(x2, c, bsz, seq, w_ada, b_ada, norm1_g, w_in, conv_w, conv_b, dt_bias_f, dt_bias_b,
           a_log_f, a_log_b, d_skip, ssd_norm_g, w_ssd_br, na_rpb, w_na_br, w_out, norm2_g,
           peer_wq, peer_keys, peer_u, peer_v, out_g):
    t, d = x2.shape
    d_ssd = SSD_HEADS * SSD_HEADDIM
    d_xbc = d_ssd + 2 * SSD_GROUPS * SSD_STATE
    d_na = NA_HEADS * NA_HEADDIM
    assert seq % (SSD_CHUNK * SSD_SUB) == 0 and seq % (GRID_W * NA_QROWS) == 0
    assert seq // GRID_W >= NA_KROWS and d == d_ssd == d_na

    mod = _ada(c, w_ada, b_ada)
    sh1, sc1, g1, sh2, sc2, g2 = [m.reshape(bsz, 1, d) for m in jnp.split(mod, 6, axis=-1)]

    o = np.cumsum([0, d_ssd, d_xbc, SSD_HEADS, SSD_HEADS, 3 * d_na, d, d])
    w_main = jnp.concatenate([w_in[:, o[1]:o[2]], w_in[:, o[0]:o[1]], w_in[:, o[4]:o[7]]],
                             axis=1).astype(BF16)
    w_dt = jnp.pad(w_in[:, o[2]:o[4]], ((0, 0), (0, LANES - 2 * SSD_HEADS))).astype(BF16)
    z_col = d_xbc // d_ssd
    q_col = (d_xbc + d_ssd) // LANES
    gs_col = (d_xbc + d_ssd + 3 * d_na) // d
    proj, dt_raw = _inproj(x2, sh1, sc1, norm1_g.reshape(1, d), w_main, w_dt, seq)

    heads = np.arange(d_ssd) // SSD_HEADDIM

    def e2(off):
        m = np.zeros((LANES, d_ssd), np.float32)
        m[off + heads, np.arange(d_ssd)] = 1.0
        m[off + 64 + heads, np.arange(d_ssd)] = 1.0
        return jnp.asarray(m, BF16)

    def lanes16(vec, off):
        return jnp.zeros((1, LANES), F32).at[0, off:off + SSD_HEADS].set(vec.astype(F32))

    params = {
        "cw": jnp.pad(conv_w.astype(F32), ((0, 8 - CONV_W), (0, 0))),
        "cb": conv_b.astype(F32).reshape(1, d_xbc),
        "dtb_f": lanes16(dt_bias_f, 0), "dtb_b": lanes16(dt_bias_b, SSD_HEADS),
        "a_f": lanes16(-jnp.exp(a_log_f.astype(F32)), 0),
        "a_b": lanes16(-jnp.exp(a_log_b.astype(F32)), SSD_HEADS),
        "e2_f": e2(0), "e2_b": e2(SSD_HEADS),
        "dsk": jnp.repeat(d_skip.astype(F32), SSD_HEADDIM).reshape(1, d_ssd),
        "ng": ssd_norm_g.astype(F32).reshape(1, d_ssd),
    }
    fwd = _ssd(proj, dt_raw, z_col, None, params, bsz, seq, rev=False)
    ynorm = _ssd(proj, dt_raw, z_col, fwd, params, bsz, seq, rev=True)

    bias, var_of_blk, ks_of_blk = _na_bias(na_rpb, seq // GRID_W)
    attn = _na(proj, bias, var_of_blk, ks_of_blk, q_col, bsz, seq)

    h, scores = _mix(x2, ynorm, attn, proj, gs_col, g1, sh2, sc2, norm2_g.reshape(1, d),
                     w_ssd_br.astype(BF16), w_na_br.astype(BF16), w_out.astype(BF16),
                     peer_wq.astype(BF16), peer_keys.astype(BF16), seq)

    eidx, gate = _select(scores)
    nj = PEER_HEADS * PEER_TOPK
    eidx_t = eidx.reshape(nj, t).T
    gate_t = gate.reshape(nj, t).T
    return _peer(h, sh2, sc2, g2, norm2_g.reshape(1, d), out_g, eidx_t, gate_t,
                 peer_u.astype(BF16), peer_v.astype(BF16), seq)


def kernel(x, c, w_ada, b_ada, norm1_g, w_in, conv_w, conv_b, dt_bias_f, dt_bias_b, a_log_f, a_log_b,
           d_skip, ssd_norm_g, w_ssd_br, na_rpb, w_na_br, w_out, norm2_g, peer_wq, peer_keys, peer_u,
           peer_v, final_g):
    bsz, seq, d = x.shape
    depth = w_ada.shape[0]
    assert depth == 1, "the final RMSNorm is fused into the last layer's PEER kernel"
    i = 0
    out = _layer(x.reshape(bsz * seq, d), c, bsz, seq, w_ada[i], b_ada[i], norm1_g[i], w_in[i],
                 conv_w[i], conv_b[i], dt_bias_f[i], dt_bias_b[i], a_log_f[i], a_log_b[i], d_skip[i],
                 ssd_norm_g[i], w_ssd_br[i], na_rpb[i], w_na_br[i], w_out[i], norm2_g[i],
                 peer_wq[i], peer_keys[i], peer_u[i], peer_v[i], final_g.reshape(1, d))
    return out.reshape(bsz, seq, d)
```

```python
import functools

import numpy as np
import jax
import jax.numpy as jnp
from jax import lax
from jax.experimental import pallas as pl
from jax.experimental.pallas import tpu as pltpu

F32 = jnp.float32
BF16 = jnp.bfloat16
I32 = jnp.int32

EPS = 1e-6
GRID_W = 64
SSD_HEADS = 16
SSD_HEADDIM = 64
SSD_GROUPS = 4
SSD_STATE = 128
SSD_CHUNK = 128
SSD_SUB = 2
CONV_W = 5
NA_HEADS = 16
NA_HEADDIM = 64
NA_KH = 8
NA_KW = 16
NA_QROWS = 4
NA_KROWS = 12
NA_QB = 16
NA_KB = NA_QB + NA_KW
PEER_HEADS = 8
PEER_NKEYS = 128
PEER_TOPK = 16
PEER_DHALF = 128
W3_PITCH = PEER_NKEYS // 2 + 8

LANES = 128
HALO = 16
VMEM_LIMIT = 56 * 1024 * 1024


def _cparams(sem):
    return pltpu.CompilerParams(dimension_semantics=sem, vmem_limit_bytes=VMEM_LIMIT)


def _sigmoid(x):
    return 0.5 * jnp.tanh(0.5 * x) + 0.5


def _silu(x):
    h = 0.5 * x
    return h * (jnp.tanh(h) + 1.0)


def _rms_mod(x, g, shift, scale):
    ms = jnp.mean(x * x, axis=-1, keepdims=True)
    y = x * lax.rsqrt(ms + EPS) * g
    return y * (1.0 + scale) + shift


def _ada_kernel(c_ref, w_ref, b_ref, o_ref):
    c = c_ref[...]
    sc = _silu(c)
    o_ref[...] = jnp.dot(sc, w_ref[...], preferred_element_type=F32,
                         precision=lax.Precision.HIGHEST) + b_ref[...]


def _ada(c, w, b):
    bsz, d = c.shape
    n = w.shape[1]
    tn = 1024
    return pl.pallas_call(
        _ada_kernel,
        grid=(n // tn,),
        in_specs=[pl.BlockSpec((bsz, d), lambda j: (0, 0)),
                  pl.BlockSpec((d, tn), lambda j: (0, j)),
                  pl.BlockSpec((1, tn), lambda j: (0, j))],
        out_specs=pl.BlockSpec((bsz, tn), lambda j: (0, j)),
        out_shape=jax.ShapeDtypeStruct((bsz, n), F32),
        compiler_params=_cparams(("arbitrary",)),
        name="ada",
    )(c, w, b.reshape(1, n))


def _inproj_kernel(x_ref, sh_ref, sc_ref, g_ref, w_ref, wdt_ref, proj_ref, dt_ref):
    n1 = _rms_mod(x_ref[...], g_ref[...], sh_ref[0], sc_ref[0]).astype(BF16)
    dt_ref[...] = jnp.dot(n1, wdt_ref[...], preferred_element_type=F32)
    tn = 1024
    for j in range(w_ref.shape[1] // tn):
        proj_ref[:, j * tn:(j + 1) * tn] = jnp.dot(
            n1, w_ref[:, j * tn:(j + 1) * tn], preferred_element_type=F32).astype(BF16)


def _inproj(x2, sh, sc, g, w, wdt, seq):
    t, d = x2.shape
    n = w.shape[1]
    tm = min(512, seq)
    bidx = lambda i: ((i * tm) // seq, 0, 0)
    const2 = lambda i: (0, 0)
    return pl.pallas_call(
        _inproj_kernel,
        grid=(t // tm,),
        in_specs=[pl.BlockSpec((tm, d), lambda i: (i, 0)),
                  pl.BlockSpec((1, 1, d), bidx),
                  pl.BlockSpec((1, 1, d), bidx),
                  pl.BlockSpec((1, d), const2),
                  pl.BlockSpec((d, n), const2, pipeline_mode=pl.Buffered(1)),
                  pl.BlockSpec((d, LANES), const2, pipeline_mode=pl.Buffered(1))],
        out_specs=[pl.BlockSpec((tm, n), lambda i: (i, 0)),
                   pl.BlockSpec((tm, LANES), lambda i: (i, 0))],
        out_shape=[jax.ShapeDtypeStruct((t, n), BF16),
                   jax.ShapeDtypeStruct((t, LANES), F32)],
        compiler_params=_cparams(("arbitrary",)),
        name="inproj",
    )(x2, sh, sc, g, w, wdt)


def _expand_heads(v, e2_ref):
    hi = v.astype(BF16).astype(F32)
    comb = (hi + pltpu.roll(v - hi, 64, 1)).astype(BF16)
    return jnp.dot(comb, e2_ref[...], preferred_element_type=F32)


def _ssd_conv(main, ext, cw_ref, cb_ref):
    ck = main.shape[0]
    d_ssd = SSD_HEADS * SSD_HEADDIM
    erow = lax.broadcasted_iota(I32, (ck, ck + 2 * HALO), 0)
    ecol = lax.broadcasted_iota(I32, (ck, ck + 2 * HALO), 1)
    mid = CONV_W // 2
    acc = cb_ref[...] + main.astype(F32) * cw_ref[mid:mid + 1, :]
    for w in range(CONV_W):
        if w != mid:
            shift = jnp.where(ecol == erow + (HALO + w - mid), 1.0, 0.0).astype(BF16)
            acc = acc + jnp.dot(shift, ext, preferred_element_type=F32) * cw_ref[w:w + 1, :]
    xbc = _silu(acc)
    return xbc[:, :d_ssd], xbc[:, d_ssd:].astype(BF16)


def _ssd_scan(xs, bc, dt_raw, dtb_ref, a_ref, e2_ref, st_ref, rev):
    ck = SSD_CHUNK
    d_ssd = SSD_HEADS * SSD_HEADDIM
    d_bc = SSD_GROUPS * SSD_STATE
    gw = d_ssd // SSD_GROUPS
    lane = lax.broadcasted_iota(I32, (ck, LANES), 1)
    row = lax.broadcasted_iota(I32, (ck, LANES), 0)
    off = SSD_HEADS if rev else 0
    valid = (lane >= off) & (lane < off + SSD_HEADS)
    dtr = dt_raw + dtb_ref[...]
    dt = jnp.where(valid, jnp.maximum(dtr, 0.0) + jnp.log1p(jnp.exp(-jnp.abs(dtr))), 0.0)
    da = dt * a_ref[...]
    tri = (lane >= row) if rev else (lane <= row)
    trib = jnp.where(tri, 1.0, 0.0).astype(BF16)
    hi = da.astype(BF16)
    r1 = da - hi.astype(F32)
    mid = r1.astype(BF16)
    lo = (r1 - mid.astype(F32)).astype(BF16)
    cs = (jnp.dot(trib, hi, preferred_element_type=F32)
          + jnp.dot(trib, mid, preferred_element_type=F32)
          + jnp.dot(trib, lo, preferred_element_type=F32))
    cst = cs.T
    tot_row = 0 if rev else ck - 1
    tot = cs[tot_row:tot_row + 1, :]
    dec = jnp.where(valid, jnp.exp(tot - cs), 0.0)
    ecs = jnp.where(valid, jnp.exp(cs), 0.0)
    ecs_e = _expand_heads(ecs, e2_ref)
    xdt = xs * _expand_heads(dt, e2_ref)
    xdec = (xs * _expand_heads(dt * dec, e2_ref)).astype(BF16)

    lo_half = lax.broadcasted_iota(I32, (ck, LANES), 1) < SSD_HEADDIM
    hpg = SSD_HEADS // SSD_GROUPS
    ys = []
    for g in range(SSD_GROUPS):
        bg = bc[:, g * SSD_STATE:(g + 1) * SSD_STATE]
        cg = bc[:, d_bc + g * SSD_STATE:d_bc + (g + 1) * SSD_STATE]
        cbg = lax.dot_general(cg, bg, (((1,), (1,)), ((), ())), preferred_element_type=F32)
        yg = []
        for q in range(hpg // 2):
            ls = []
            for e in range(2):
                col = off + g * hpg + q * 2 + e
                dlt = cs[:, col:col + 1] - cst[col:col + 1, :]
                seg = jnp.exp(jnp.where(tri, dlt, -jnp.inf))
                ls.append((cbg * seg).astype(BF16))
            blk = g * (hpg // 2) + q
            xq = xdt[:, blk * LANES:(blk + 1) * LANES]
            rhs = jnp.concatenate([jnp.where(lo_half, xq, 0.0), jnp.where(lo_half, 0.0, xq)],
                                  axis=0).astype(BF16)
            yg.append(jnp.dot(jnp.concatenate(ls, axis=1), rhs, preferred_element_type=F32))
        y_diag = jnp.concatenate(yg, axis=1)
        st = st_ref[g]
        e_g = ecs_e[:, g * gw:(g + 1) * gw]
        y_off = jnp.dot(cg, st.astype(BF16), preferred_element_type=F32) * e_g
        ys.append(y_diag + y_off)
        upd = jnp.dot(bg.astype(F32).T.astype(BF16), xdec[:, g * gw:(g + 1) * gw],
                      preferred_element_type=F32)
        st_ref[g] = st * e_g[tot_row:tot_row + 1, :] + upd
    return jnp.concatenate(ys, axis=1)


def _ssd_kernel(*refs, rev):
    if rev:
        (xs_ref, bc_ref, dt_ref, z_ref, yf_ref, dtb_ref, a_ref, e2_ref, dsk_ref, ng_ref,
         out_ref, st_ref) = refs
    else:
        (xm_ref, xp_ref, xn_ref, dt_ref, cw_ref, cb_ref, dtb_ref, a_ref, e2_ref,
         out_ref, xs_out_ref, bc_out_ref, st_ref) = refs
    ck = SSD_CHUNK
    c = pl.program_id(1)
    nc = pl.num_programs(1)
    ce = nc - 1 - c if rev else c

    @pl.when(c == 0)
    def _():
        st_ref[...] = jnp.zeros_like(st_ref)

    rows = [slice(j * ck, (j + 1) * ck) for j in range(SSD_SUB)]
    if rev:
        conv = [(xs_ref[r, :], bc_ref[r, :]) for r in rows]
    else:
        main = xm_ref[...]
        zero_halo = jnp.zeros((HALO, main.shape[1]), main.dtype)
        ext = jnp.concatenate([jnp.where(ce == 0, zero_halo, xp_ref[...]), main,
                               jnp.where(ce == nc - 1, zero_halo, xn_ref[...])], axis=0)
        conv = [_ssd_conv(main[r], ext[r.start:r.stop + 2 * HALO], cw_ref, cb_ref) for r in rows]
        for r, (xs, bc) in zip(rows, conv):
            xs_out_ref[r, :] = xs
            bc_out_ref[r, :] = bc

    for j in (reversed(range(SSD_SUB)) if rev else range(SSD_SUB)):
        r = rows[j]
        xs, bc = conv[j]
        y = _ssd_scan(xs, bc, dt_ref[r, :], dtb_ref, a_ref, e2_ref, st_ref, rev)
        if rev:
            y = yf_ref[r, :] + y + dsk_ref[...] * xs
            z = z_ref[r, :].astype(F32)
            u = y * _silu(z)
            ms = jnp.mean(u * u, axis=-1, keepdims=True)
            out_ref[r, :] = (u * lax.rsqrt(ms + EPS) * ng_ref[...]).astype(BF16)
        else:
            out_ref[r, :] = y


def _ssd(proj, dt_raw, z_col, fwd, params, bsz, seq, rev):
    t = proj.shape[0]
    ck = SSD_CHUNK * SSD_SUB
    nc = seq // ck
    d_xbc = params["cw"].shape[1]
    d_ssd = SSD_HEADS * SSD_HEADDIM
    d_bc = d_xbc - d_ssd
    per = ck // HALO
    last_halo = t // HALO - 1

    def ce_of(c):
        return nc - 1 - c if rev else c

    main = lambda b, c: (b * nc + ce_of(c), 0)
    prev = lambda b, c: (jnp.maximum((b * nc + ce_of(c)) * per - 1, 0), 0)
    nxt = lambda b, c: (jnp.minimum((b * nc + ce_of(c)) * per + per, last_halo), 0)
    const2 = lambda b, c: (0, 0)
    d = "b" if rev else "f"
    dir_specs = [pl.BlockSpec((1, LANES), const2), pl.BlockSpec((1, LANES), const2),
                 pl.BlockSpec((LANES, d_ssd), const2)]
    dir_args = [params["dtb_" + d], params["a_" + d], params["e2_" + d]]
    if rev:
        yf, xs, bc = fwd
        in_specs = [pl.BlockSpec((ck, d_ssd), main), pl.BlockSpec((ck, d_bc), main),
                    pl.BlockSpec((ck, LANES), main),
                    pl.BlockSpec((ck, d_ssd), lambda b, c: (b * nc + ce_of(c), z_col)),
                    pl.BlockSpec((ck, d_ssd), main)] + dir_specs + [
                        pl.BlockSpec((1, d_ssd), const2), pl.BlockSpec((1, d_ssd), const2)]
        args = [xs, bc, dt_raw, proj, yf] + dir_args + [params["dsk"], params["ng"]]
        out_specs = pl.BlockSpec((ck, d_ssd), main)
        out_shape = jax.ShapeDtypeStruct((t, d_ssd), BF16)
    else:
        in_specs = [pl.BlockSpec((ck, d_xbc), main), pl.BlockSpec((HALO, d_xbc), prev),
                    pl.BlockSpec((HALO, d_xbc), nxt), pl.BlockSpec((ck, LANES), main),
                    pl.BlockSpec((8, d_xbc), const2), pl.BlockSpec((1, d_xbc), const2)] + dir_specs
        args = [proj, proj, proj, dt_raw, params["cw"], params["cb"]] + dir_args
        out_specs = [pl.BlockSpec((ck, d_ssd), main), pl.BlockSpec((ck, d_ssd), main),
                     pl.BlockSpec((ck, d_bc), main)]
        out_shape = [jax.ShapeDtypeStruct((t, d_ssd), F32), jax.ShapeDtypeStruct((t, d_ssd), F32),
                     jax.ShapeDtypeStruct((t, d_bc), BF16)]
    return pl.pallas_call(
        functools.partial(_ssd_kernel, rev=rev),
        grid=(bsz, nc),
        in_specs=in_specs,
        out_specs=out_specs,
        out_shape=out_shape,
        scratch_shapes=[pltpu.VMEM((SSD_GROUPS, SSD_STATE, d_ssd // SSD_GROUPS), F32)],
        compiler_params=_cparams(("arbitrary", "arbitrary")),
        name="ssd_bwd" if rev else "ssd_fwd",
    )(*args)


def _na_plan(rows):
    nblk = rows // NA_QROWS
    variants, var_of_blk, ks_of_blk = [], [], []
    for i in range(nblk):
        ks = int(np.clip(i * NA_QROWS - NA_KH // 2, 0, rows - NA_KROWS))
        drow = np.zeros((NA_QROWS, NA_KROWS), np.int64)
        ok = np.zeros((NA_QROWS, NA_KROWS), bool)
        for dr in range(NA_QROWS):
            r = i * NA_QROWS + dr
            rs = int(np.clip(r - NA_KH // 2, 0, rows - NA_KH))
            for j in range(NA_KH):
                w = rs + j - ks
                drow[dr, w] = rs + j - r + NA_KH - 1
                ok[dr, w] = True
        key = (drow.tobytes(), ok.tobytes())
        for vi, (k2, _, _) in enumerate(variants):
            if k2 == key:
                var_of_blk.append(vi)
                break
        else:
            var_of_blk.append(len(variants))
            variants.append((key, drow, ok))
        ks_of_blk.append(ks)
    return [(d, o) for _, d, o in variants], var_of_blk, ks_of_blk


def _na_blk_start():
    ncb = GRID_W // NA_QB
    return [int(v) for v in np.clip(np.arange(ncb) * NA_QB - NA_KW // 2, 0, GRID_W - NA_KB)]


def _na_bias(rpb, rows):
    variants, var_of_blk, ks_of_blk = _na_plan(rows)
    nh = rpb.shape[0]
    cols = np.arange(GRID_W)
    win_start = np.clip(cols - NA_KW // 2, 0, GRID_W - NA_KW)
    in_win = (cols[None, :] >= win_start[:, None]) & (cols[None, :] < win_start[:, None] + NA_KW)
    dcol = np.clip(cols[None, :] - cols[:, None] + NA_KW - 1, 0, 2 * NA_KW - 2)
    onehot = (dcol[:, :, None] == np.arange(2 * NA_KW - 1)).astype(np.float32)
    band = jnp.einsum("hab,qkb->haqk", rpb.astype(F32), onehot, precision=lax.Precision.HIGHEST)
    band = jnp.where(in_win[None, None], band, -jnp.inf)
    neg = jnp.full((nh, NA_QB, NA_KB), -jnp.inf, F32)
    tabs = []
    for drow, ok in variants:
        per_cb = []
        for m, bs in enumerate(_na_blk_start()):
            qrows = [jnp.concatenate(
                [band[:, drow[dr, w], m * NA_QB:(m + 1) * NA_QB, bs:bs + NA_KB] if ok[dr, w] else neg
                 for w in range(NA_KROWS)], axis=2) for dr in range(NA_QROWS)]
            per_cb.append(jnp.concatenate(qrows, axis=1))
        tabs.append(jnp.stack(per_cb, axis=1))
    tab = jnp.stack(tabs)
    nv, _, ncb, mq, nk = tab.shape
    tab = tab.reshape(nv, nh // 2, 2, ncb, mq, nk).transpose(0, 1, 3, 2, 4, 5)
    return tab.reshape(nv, nh // 2, ncb, 2 * mq, nk), var_of_blk, ks_of_blk


def _na_kernel(var_ref, ks_ref, q_ref, k_ref, v_ref, bias_ref, o_ref, kf_ref, vf_ref):
    mq = NA_QROWS * NA_QB
    nblk = q_ref.shape[0] // (NA_QROWS * GRID_W)
    lo_half = lax.broadcasted_iota(I32, (mq, LANES), 1) < NA_HEADDIM
    kf_ref[...] = k_ref[...].astype(F32)
    vf_ref[...] = v_ref[...].astype(F32)

    def blk(i, carry):
        var = var_ref[i]
        ks = ks_ref[i]
        starts = _na_blk_start()
        q_at = [[pl.multiple_of((i * NA_QROWS + dr) * GRID_W + m * NA_QB, NA_QB)
                 for dr in range(NA_QROWS)] for m in range(len(starts))]
        k_at = [[pl.multiple_of((ks + w) * GRID_W + bs, 8) for w in range(NA_KROWS)]
                for bs in starts]
        scores = []
        for m in range(len(starts)):
            qb = jnp.concatenate([q_ref[pl.ds(o, NA_QB), :] for o in q_at[m]], axis=0)
            qb = qb * (NA_HEADDIM ** -0.5)
            zero = jnp.zeros_like(qb)
            q2 = jnp.concatenate([jnp.where(lo_half, qb, zero), jnp.where(lo_half, zero, qb)], axis=0)
            kb = jnp.concatenate([kf_ref[pl.ds(o, NA_KB), :] for o in k_at[m]], axis=0).astype(BF16)
            s = lax.dot_general(q2, kb, (((1,), (1,)), ((), ())), preferred_element_type=F32)
            scores.append(s + bias_ref[var, 0, m])
        outs = []
        for m, s in enumerate(scores):
            p = jnp.exp(s - jnp.max(s, axis=-1, keepdims=True))
            l = jnp.sum(p, axis=-1, keepdims=True)
            vb = jnp.concatenate([vf_ref[pl.ds(o, NA_KB), :] for o in k_at[m]], axis=0).astype(BF16)
            outs.append(jnp.dot(p.astype(BF16), vb, preferred_element_type=F32) / l)
        for m, o2 in enumerate(outs):
            ob = jnp.where(lo_half, o2[:mq], o2[mq:]).astype(BF16)
            for dr, o in enumerate(q_at[m]):
                o_ref[pl.ds(o, NA_QB), :] = ob[dr * NA_QB:(dr + 1) * NA_QB]
        return carry

    lax.fori_loop(0, nblk, blk, 0, unroll=4)


def _na(proj, bias, var_of_blk, ks_of_blk, q_col, bsz, seq):
    t = proj.shape[0]
    npair = NA_HEADS // 2
    grid_spec = pltpu.PrefetchScalarGridSpec(
        num_scalar_prefetch=2,
        grid=(bsz, npair),
        in_specs=[pl.BlockSpec((seq, LANES), lambda b, p, *_: (b, q_col + p)),
                  pl.BlockSpec((seq, LANES), lambda b, p, *_: (b, q_col + npair + p)),
                  pl.BlockSpec((seq, LANES), lambda b, p, *_: (b, q_col + 2 * npair + p)),
                  pl.BlockSpec((bias.shape[0], 1) + bias.shape[2:], lambda b, p, *_: (0, p, 0, 0, 0))],
        out_specs=pl.BlockSpec((seq, LANES), lambda b, p, *_: (b, p)),
        scratch_shapes=[pltpu.VMEM((seq, LANES), F32), pltpu.VMEM((seq, LANES), F32)],
    )
    return pl.pallas_call(
        _na_kernel,
        grid_spec=grid_spec,
        out_shape=jax.ShapeDtypeStruct((t, NA_HEADS * NA_HEADDIM), BF16),
        compiler_params=_cparams(("arbitrary", "arbitrary")),
        name="na",
    )(jnp.asarray(var_of_blk, I32), jnp.asarray(ks_of_blk, I32), proj, proj, proj, bias)


def _mix_kernel(x_ref, yn_ref, at_ref, gs_ref, gn_ref, g1_ref, sh_ref, sc_ref, ng_ref,
                ws_ref, wn_ref, wo_ref, wq_ref, keys_ref, h_ref, s_ref):
    y1 = jnp.dot(yn_ref[...], ws_ref[...], preferred_element_type=F32)
    y2 = jnp.dot(at_ref[...], wn_ref[...], preferred_element_type=F32)
    mixed = _sigmoid(gs_ref[...].astype(F32)) * y1 + _sigmoid(gn_ref[...].astype(F32)) * y2
    h = x_ref[...] + g1_ref[0] * jnp.dot(mixed.astype(BF16), wo_ref[...], preferred_element_type=F32)
    h_ref[...] = h
    n2 = _rms_mod(h, ng_ref[...], sh_ref[0], sc_ref[0]).astype(BF16)
    q = jnp.dot(n2, wq_ref[...], preferred_element_type=F32).astype(BF16)
    sub = 8
    for hz in range(2 * PEER_HEADS):
        qs = q[:, hz * PEER_DHALF:(hz + 1) * PEER_DHALF]
        sc = lax.dot_general(keys_ref[hz % 2], qs, (((1,), (1,)), ((), ())),
                             preferred_element_type=F32)
        for kq in range(PEER_NKEYS // sub):
            for tb in range(sc.shape[1] // LANES):
                s_ref[hz, kq, tb * sub:(tb + 1) * sub, :] = sc[kq * sub:(kq + 1) * sub,
                                                               tb * LANES:(tb + 1) * LANES]


def _mix(x2, ynorm, attn, proj, gs_col, g1, sh2, sc2, ng2, ws, wn, wo, wq, keys, seq):
    t, d = x2.shape
    tm = min(512, seq)
    bidx = lambda i: ((i * tm) // seq, 0, 0)
    row = lambda i: (i, 0)
    const2 = lambda i: (0, 0)
    return pl.pallas_call(
        _mix_kernel,
        grid=(t // tm,),
        in_specs=[pl.BlockSpec((tm, d), row), pl.BlockSpec((tm, d), row), pl.BlockSpec((tm, d), row),
                  pl.BlockSpec((tm, d), lambda i: (i, gs_col)),
                  pl.BlockSpec((tm, d), lambda i: (i, gs_col + 1)),
                  pl.BlockSpec((1, 1, d), bidx), pl.BlockSpec((1, 1, d), bidx),
                  pl.BlockSpec((1, 1, d), bidx), pl.BlockSpec((1, d), const2),
                  pl.BlockSpec((d, d), const2, pipeline_mode=pl.Buffered(1)),
                  pl.BlockSpec((d, d), const2, pipeline_mode=pl.Buffered(1)),
                  pl.BlockSpec((d, d), const2, pipeline_mode=pl.Buffered(1)),
                  pl.BlockSpec(wq.shape, const2, pipeline_mode=pl.Buffered(1)),
                  pl.BlockSpec(keys.shape, lambda i: (0, 0, 0), pipeline_mode=pl.Buffered(1))],
        out_specs=[pl.BlockSpec((tm, d), row),
                   pl.BlockSpec((2 * PEER_HEADS, PEER_NKEYS // 8, tm // LANES * 8, LANES),
                                lambda i: (0, 0, i, 0))],
        out_shape=[jax.ShapeDtypeStruct((t, d), F32),
                   jax.ShapeDtypeStruct((2 * PEER_HEADS, PEER_NKEYS // 8, t // LANES * 8, LANES), F32)],
        compiler_params=_cparams(("arbitrary",)),
        name="mix",
    )(x2, ynorm, attn, proj, proj, g1, sh2, sc2, ng2, ws, wn, wo, wq, keys)


def _hyperbola():
    return [(i, k) for i in range(PEER_TOPK) for k in range(PEER_TOPK)
            if (i + 1) * (k + 1) <= PEER_TOPK]


def _sort_network(n):
    pairs = []
    p = 1
    while p < n:
        k = p
        while k >= 1:
            for j in range(k % p, n - k, 2 * k):
                for i in range(min(k, n - j - k)):
                    if (i + j) // (2 * p) == (i + j + k) // (2 * p):
                        pairs.append((i + j, i + j + k))
            k //= 2
        p *= 2
    return pairs


def _precedes(va, pa, vb, pb):
    return (va > vb) | ((va == vb) & (pa < pb))


def _compare_exchange(v, p, i, j):
    c = _precedes(v[i], p[i], v[j], p[j])
    v[i], v[j] = jnp.where(c, v[i], v[j]), jnp.where(c, v[j], v[i])
    p[i], p[j] = jnp.where(c, p[i], p[j]), jnp.where(c, p[j], p[i])


def _top16(get, n, sv_ref, sp_ref):
    k = PEER_TOPK
    ng = n // k
    net = _sort_network(k)
    for g in range(ng):
        items = [get(g * k + j) for j in range(k)]
        v = [it[0] for it in items]
        p = [it[1] for it in items]
        for i, j in net:
            _compare_exchange(v, p, i, j)
        if ng == 1:
            return v, p
        for j in range(k):
            sv_ref[g, j] = v[j]
            sp_ref[g, j] = p[j]
    step = 1
    while True:
        for g in range(0, ng, 2 * step):
            v, p = [], []
            for j in range(k):
                xv, xp = sv_ref[g, j], sp_ref[g, j]
                yv, yp = sv_ref[g + step, k - 1 - j], sp_ref[g + step, k - 1 - j]
                c = _precedes(xv, xp, yv, yp)
                v.append(jnp.where(c, xv, yv))
                p.append(jnp.where(c, xp, yp))
            stride = k // 2
            while stride >= 1:
                for i in range(k):
                    if i & stride == 0:
                        _compare_exchange(v, p, i, i + stride)
                stride //= 2
            if 2 * step >= ng:
                return v, p
            for j in range(k):
                sv_ref[g, j] = v[j]
                sp_ref[g, j] = p[j]
        step *= 2


def _select_kernel(s_ref, eidx_ref, gate_ref, sv_ref, sp_ref):
    nk = PEER_NKEYS
    k = PEER_TOPK
    sub = 8
    shape = (sub, LANES)
    ebits = (nk * nk - 1).bit_length()

    def key(z, j):
        return s_ref[z, j // sub, pl.ds(j % sub, sub, stride=sub), :], jnp.full(shape, j, I32)

    tops = []
    for z in range(2):
        tops.append(_top16(functools.partial(key, z), nk, sv_ref, sp_ref))
    (tv0, ti0), (tv1, ti1) = tops

    pairs = _hyperbola()
    npad = -len(pairs) % k

    def cand(j):
        if j >= len(pairs):
            return jnp.full(shape, -jnp.inf, F32), jnp.full(shape, (k * k) << ebits, I32)
        i, kk = pairs[j]
        return tv0[i] + tv1[kk], ((i * k + kk) << ebits) + ti0[i] * nk + ti1[kk]

    top, ids = _top16(cand, len(pairs) + npad, sv_ref, sp_ref)
    ex = [jnp.exp(t - top[0]) for t in top]
    zsum = ex[0]
    for r in range(1, k):
        zsum = zsum + ex[r]
    for r in range(k):
        eidx_ref[0, r] = ids[r] & ((1 << ebits) - 1)
        gate_ref[0, r] = ex[r] / zsum


def _select(scores):
    r = 8
    nk = PEER_NKEYS
    tb = scores.shape[2] // r
    assert tb % r == 0
    oshape = (PEER_HEADS, PEER_TOPK, tb, LANES)
    return pl.pallas_call(
        _select_kernel,
        grid=(tb // r, PEER_HEADS),
        in_specs=[pl.BlockSpec((2, nk // r, r * r, LANES), lambda i, h: (h, 0, i, 0))],
        out_specs=[pl.BlockSpec((1, PEER_TOPK, r, LANES), lambda i, h: (h, 0, i, 0)),
                   pl.BlockSpec((1, PEER_TOPK, r, LANES), lambda i, h: (h, 0, i, 0))],
        out_shape=[jax.ShapeDtypeStruct(oshape, I32), jax.ShapeDtypeStruct(oshape, F32)],
        scratch_shapes=[pltpu.VMEM((nk // PEER_TOPK, PEER_TOPK, r, LANES), F32),
                        pltpu.VMEM((nk // PEER_TOPK, PEER_TOPK, r, LANES), I32)],
        compiler_params=_cparams(("arbitrary", "arbitrary")),
        name="select",
    )(scores)


def _peer_kernel(h_ref, sh_ref, sc_ref, g2_ref, ng_ref, fg_ref, eidx_ref, gate_ref, uv_ref,
                 out_ref, n2_ref, act_ref, wv_ref, w3_ref, acc_ref, *, ns):
    nk = PEER_NKEYS
    half = nk // 2
    hi_mask = jnp.uint32(0xFFFF0000)
    tm = h_ref.shape[0]
    eb = uv_ref.shape[0]
    nblk = eb // nk
    nblk2 = nblk // 2
    s = pl.program_id(1)

    @pl.when(s == 0)
    def _():
        n2_ref[...] = _rms_mod(h_ref[...], ng_ref[...], sh_ref[0], sc_ref[0]).astype(BF16)
        act_ref[...] = jnp.zeros_like(act_ref)

    @pl.when(s < ns)
    def _():
        e = eidx_ref[...]
        ai = jnp.right_shift(e, nk.bit_length() - 1)
        bi = jnp.bitwise_and(e, nk - 1)
        n2 = n2_ref[...]
        for a in range(0, nblk, 2):
            sc = lax.dot_general(n2, uv_ref[a * nk:(a + 2) * nk, :], (((1,), (1,)), ((), ())),
                                 preferred_element_type=F32)
            act = act_ref[...]
            for j in range(2):
                got = jnp.take_along_axis(sc[:, j * nk:(j + 1) * nk], bi, axis=1)
                act = jnp.where(ai == s * nblk + a + j, got, act)
            act_ref[...] = act

    @pl.when(s == ns - 1)
    def _():
        act = act_ref[...]
        gelu = 0.5 * act * (1.0 + lax.erf(act * (2.0 ** -0.5)))
        wv_ref[...] = gate_ref[...] * gelu
        r = lax.broadcasted_iota(I32, (nk, LANES), 0)
        sub = r.astype(F32).astype(BF16)
        rowkey = jnp.where(r < half, 2 * r, 2 * (r - half) + 1).astype(F32).astype(BF16)
        zero = jnp.zeros((nk, LANES), BF16)
        one = jnp.ones((nk, LANES), BF16)

        def tok(t, carry):
            e = eidx_ref[pl.ds(t, 1), :]
            w = wv_ref[pl.ds(t, 1), :].astype(BF16)
            ar = jnp.right_shift(e, nk.bit_length() - 1).astype(F32).astype(BF16)
            br = jnp.bitwise_and(e, nk - 1).astype(F32).astype(BF16)
            pt = jnp.where(rowkey == ar, w, zero)
            qt = jnp.where(sub == br, one, zero)
            wt = lax.dot_general(pt, qt, (((1,), (1,)), ((), ())), preferred_element_type=F32)
            bits = lax.bitcast_convert_type(wt, jnp.uint32)
            word = jnp.right_shift(bits[:half], 16) | (bits[half:] & hi_mask)
            w3_ref[pl.ds(pl.multiple_of(t * W3_PITCH, 8), half), :] = word
            return carry
        lax.fori_loop(0, tm, tok, 0, unroll=32)

    @pl.when(s >= ns)
    def _():
        base = (s - ns) * nblk2
        words = [w3_ref[pl.ds(base + a, tm, stride=W3_PITCH), :] for a in range(nblk2)]
        lo = [lax.bitcast_convert_type(jnp.left_shift(w, 16), F32).astype(BF16) for w in words]
        hi = [lax.bitcast_convert_type(w & hi_mask, F32).astype(BF16) for w in words]
        cols = [c for pair in zip(lo, hi) for c in pair]
        part = jnp.dot(jnp.concatenate(cols, axis=1), uv_ref[...], preferred_element_type=F32)

        @pl.when(s == ns)
        def _():
            acc_ref[...] = part

        @pl.when(s > ns)
        def _():
            acc_ref[...] += part

    @pl.when(s == 2 * ns - 1)
    def _():
        hh = h_ref[...] + g2_ref[0] * acc_ref[...]
        ms = jnp.mean(hh * hh, axis=-1, keepdims=True)
        out_ref[...] = hh * lax.rsqrt(ms + EPS) * fg_ref[...]


def _peer(h, sh2, sc2, g2, ng2, fg, eidx_t, gate_t, u, v, seq):
    t, d = h.shape
    ne = u.shape[0]
    tm = min(512, seq)
    eb = 4096
    ns = ne // eb
    nj = eidx_t.shape[1]
    uv = jnp.concatenate([u, v], axis=0).astype(BF16)
    bidx = lambda i, s: ((i * tm) // seq, 0, 0)
    row = lambda i, s: (i, 0)
    const2 = lambda i, s: (0, 0)
    return pl.pallas_call(
        functools.partial(_peer_kernel, ns=ns),
        grid=(t // tm, 2 * ns),
        in_specs=[pl.BlockSpec((tm, d), row),
                  pl.BlockSpec((1, 1, d), bidx), pl.BlockSpec((1, 1, d), bidx),
                  pl.BlockSpec((1, 1, d), bidx),
                  pl.BlockSpec((1, d), const2), pl.BlockSpec((1, d), const2),
                  pl.BlockSpec((tm, nj), row), pl.BlockSpec((tm, nj), row),
                  pl.BlockSpec((eb, d), lambda i, s: (s, 0))],
        out_specs=pl.BlockSpec((tm, d), row),
        out_shape=jax.ShapeDtypeStruct((t, d), F32),
        scratch_shapes=[pltpu.VMEM((tm, d), BF16),
                        pltpu.VMEM((tm, nj), F32),
                        pltpu.VMEM((tm, nj), F32),
                        pltpu.VMEM((tm * W3_PITCH, PEER_NKEYS), jnp.uint32),
                        pltpu.VMEM((tm, d), F32)],
        compiler_params=_cparams(("arbitrary", "arbitrary")),
        name="peer",
    )(h, sh2, sc2, g2, ng2, fg, eidx_t, gate_t, uv)


def _layer(x2, c, bsz, seq, w_ada, b_ada, norm1_g, w_in, conv_w, conv_b, dt_bias_f, dt_bias_b,
           a_log_f, a_log_b, d_skip, ssd_norm_g, w_ssd_br, na_rpb, w_na_br, w_out, norm2_g,
           peer_wq, peer_keys, peer_u, peer_v, out_g):
    t, d = x2.shape
    d_ssd = SSD_HEADS * SSD_HEADDIM
    d_xbc = d_ssd + 2 * SSD_GROUPS * SSD_STATE
    d_na = NA_HEADS * NA_HEADDIM
    assert seq % (SSD_CHUNK * SSD_SUB) == 0 and seq % (GRID_W * NA_QROWS) == 0
    assert seq // GRID_W >= NA_KROWS and d == d_ssd == d_na

    mod = _ada(c, w_ada, b_ada)
    sh1, sc1, g1, sh2, sc2, g2 = [m.reshape(bsz, 1, d) for m in jnp.split(mod, 6, axis=-1)]

    o = np.cumsum([0, d_ssd, d_xbc, SSD_HEADS, SSD_HEADS, 3 * d_na, d, d])
    w_main = jnp.concatenate([w_in[:, o[1]:o[2]], w_in[:, o[0]:o[1]], w_in[:, o[4]:o[7]]],
                             axis=1).astype(BF16)
    w_dt = jnp.pad(w_in[:, o[2]:o[4]], ((0, 0), (0, LANES - 2 * SSD_HEADS))).astype(BF16)
    z_col = d_xbc // d_ssd
    q_col = (d_xbc + d_ssd) // LANES
    gs_col = (d_xbc + d_ssd + 3 * d_na) // d
    proj, dt_raw = _inproj(x2, sh1, sc1, norm1_g.reshape(1, d), w_main, w_dt, seq)

    heads = np.arange(d_ssd) // SSD_HEADDIM

    def e2(off):
        m = np.zeros((LANES, d_ssd), np.float32)
        m[off + heads, np.arange(d_ssd)] = 1.0
        m[off + 64 + heads, np.arange(d_ssd)] = 1.0
        return jnp.asarray(m, BF16)

    def lanes16(vec, off):
        return jnp.zeros((1, LANES), F32).at[0, off:off + SSD_HEADS].set(vec.astype(F32))

    params = {
        "cw": jnp.pad(conv_w.astype(F32), ((0, 8 - CONV_W), (0, 0))),
        "cb": conv_b.astype(F32).reshape(1, d_xbc),
        "dtb_f": lanes16(dt_bias_f, 0), "dtb_b": lanes16(dt_bias_b, SSD_HEADS),
        "a_f": lanes16(-jnp.exp(a_log_f.astype(F32)), 0),
        "a_b": lanes16(-jnp.exp(a_log_b.astype(F32)), SSD_HEADS),
        "e2_f": e2(0), "e2_b": e2(SSD_HEADS),
        "dsk": jnp.repeat(d_skip.astype(F32), SSD_HEADDIM).reshape(1, d_ssd),
        "ng": ssd_norm_g.astype(F32).reshape(1, d_ssd),
    }
    fwd = _ssd(proj, dt_raw, z_col, None, params, bsz, seq, rev=False)
    ynorm = _ssd(proj, dt_raw, z_col, fwd, params, bsz, seq, rev=True)

    bias, var_of_blk, ks_of_blk = _na_bias(na_rpb, seq // GRID_W)
    attn = _na(proj, bias, var_of_blk, ks_of_blk, q_col, bsz, seq)

    h, scores = _mix(x2, ynorm, attn, proj, gs_col, g1, sh2, sc2, norm2_g.reshape(1, d),
                     w_ssd_br.astype(BF16), w_na_br.astype(BF16), w_out.astype(BF16),
                     peer_wq.astype(BF16), peer_keys.astype(BF16), seq)

    eidx, gate = _select(scores)
    nj = PEER_HEADS * PEER_TOPK
    eidx_t = eidx.reshape(nj, t).T
    gate_t = gate.reshape(nj, t).T
    return _peer(h, sh2, sc2, g2, norm2_g.reshape(1, d), out_g, eidx_t, gate_t,
                 peer_u.astype(BF16), peer_v.astype(BF16), seq)


def kernel(x, c, w_ada, b_ada, norm1_g, w_in, conv_w, conv_b, dt_bias_f, dt_bias_b, a_log_f, a_log_b,
           d_skip, ssd_norm_g, w_ssd_br, na_rpb, w_na_br, w_out, norm2_g, peer_wq, peer_keys, peer_u,
           peer_v, final_g):
    bsz, seq, d = x.shape
    depth = w_ada.shape[0]
    assert depth == 1, "the final RMSNorm is fused into the last layer's PEER kernel"
    i = 0
    out = _layer(x.reshape(bsz * seq, d), c, bsz, seq, w_ada[i], b_ada[i], norm1_g[i], w_in[i],
                 conv_w[i], conv_b[i], dt_bias_f[i], dt_bias_b[i], a_log_f[i], a_log_b[i], d_skip[i],
                 ssd_norm_g[i], w_ssd_br[i], na_rpb[i], w_na_br[i], w_out[i], norm2_g[i],
                 peer_wq[i], peer_keys[i], peer_u[i], peer_v[i], final_g.reshape(1, d))
    return out.reshape(bsz, seq, d)
```

```python
import functools

import numpy as np
import jax
import jax.numpy as jnp
from jax import lax
from jax.experimental import pallas as pl
from jax.experimental.pallas import tpu as pltpu

F32 = jnp.float32
BF16 = jnp.bfloat16
I32 = jnp.int32

EPS = 1e-6
GRID_W = 64
SSD_HEADS = 16
SSD_HEADDIM = 64
SSD_GROUPS = 4
SSD_STATE = 128
SSD_CHUNK = 128
SSD_SUB = 2
CONV_W = 5
NA_HEADS = 16
NA_HEADDIM = 64
NA_KH = 8
NA_KW = 16
NA_QROWS = 4
NA_KROWS = 12
NA_QB = 16
NA_KB = NA_QB + NA_KW
PEER_HEADS = 8
PEER_NKEYS = 128
PEER_TOPK = 16
PEER_DHALF = 128
W3_PITCH = PEER_NKEYS // 2 + 8

LANES = 128
HALO = 16
VMEM_LIMIT = 56 * 1024 * 1024


def _cparams(sem):
    return pltpu.CompilerParams(dimension_semantics=sem, vmem_limit_bytes=VMEM_LIMIT)


def _sigmoid(x):
    return 0.5 * jnp.tanh(0.5 * x) + 0.5


def _silu(x):
    h = 0.5 * x
    return h * (jnp.tanh(h) + 1.0)


def _rms_mod(x, g, shift, scale):
    ms = jnp.mean(x * x, axis=-1, keepdims=True)
    y = x * lax.rsqrt(ms + EPS) * g
    return y * (1.0 + scale) + shift


def _ada_kernel(c_ref, w_ref, b_ref, o_ref):
    c = c_ref[...]
    sc = _silu(c)
    o_ref[...] = jnp.dot(sc, w_ref[...], preferred_element_type=F32,
                         precision=lax.Precision.HIGHEST) + b_ref[...]


def _ada(c, w, b):
    bsz, d = c.shape
    n = w.shape[1]
    tn = 1024
    return pl.pallas_call(
        _ada_kernel,
        grid=(n // tn,),
        in_specs=[pl.BlockSpec((bsz, d), lambda j: (0, 0)),
                  pl.BlockSpec((d, tn), lambda j: (0, j)),
                  pl.BlockSpec((1, tn), lambda j: (0, j))],
        out_specs=pl.BlockSpec((bsz, tn), lambda j: (0, j)),
        out_shape=jax.ShapeDtypeStruct((bsz, n), F32),
        compiler_params=_cparams(("arbitrary",)),
        name="ada",
    )(c, w, b.reshape(1, n))


def _inproj_kernel(x_ref, sh_ref, sc_ref, g_ref, w_ref, wdt_ref, proj_ref, dt_ref):
    n1 = _rms_mod(x_ref[...], g_ref[...], sh_ref[0], sc_ref[0]).astype(BF16)
    dt_ref[...] = jnp.dot(n1, wdt_ref[...], preferred_element_type=F32)
    tn = 1024
    for j in range(w_ref.shape[1] // tn):
        proj_ref[:, j * tn:(j + 1) * tn] = jnp.dot(
            n1, w_ref[:, j * tn:(j + 1) * tn], preferred_element_type=F32).astype(BF16)


def _inproj(x2, sh, sc, g, w, wdt, seq):
    t, d = x2.shape
    n = w.shape[1]
    tm = min(512, seq)
    bidx = lambda i: ((i * tm) // seq, 0, 0)
    const2 = lambda i: (0, 0)
    return pl.pallas_call(
        _inproj_kernel,
        grid=(t // tm,),
        in_specs=[pl.BlockSpec((tm, d), lambda i: (i, 0)),
                  pl.BlockSpec((1, 1, d), bidx),
                  pl.BlockSpec((1, 1, d), bidx),
                  pl.BlockSpec((1, d), const2),
                  pl.BlockSpec((d, n), const2, pipeline_mode=pl.Buffered(1)),
                  pl.BlockSpec((d, LANES), const2, pipeline_mode=pl.Buffered(1))],
        out_specs=[pl.BlockSpec((tm, n), lambda i: (i, 0)),
                   pl.BlockSpec((tm, LANES), lambda i: (i, 0))],
        out_shape=[jax.ShapeDtypeStruct((t, n), BF16),
                   jax.ShapeDtypeStruct((t, LANES), F32)],
        compiler_params=_cparams(("arbitrary",)),
        name="inproj",
    )(x2, sh, sc, g, w, wdt)


def _expand_heads(v, e2_ref):
    hi = v.astype(BF16).astype(F32)
    comb = (hi + pltpu.roll(v - hi, 64, 1)).astype(BF16)
    return jnp.dot(comb, e2_ref[...], preferred_element_type=F32)


def _ssd_conv(main, ext, cw_ref, cb_ref):
    ck = main.shape[0]
    d_ssd = SSD_HEADS * SSD_HEADDIM
    erow = lax.broadcasted_iota(I32, (ck, ck + 2 * HALO), 0)
    ecol = lax.broadcasted_iota(I32, (ck, ck + 2 * HALO), 1)
    mid = CONV_W // 2
    acc = cb_ref[...] + main.astype(F32) * cw_ref[mid:mid + 1, :]
    for w in range(CONV_W):
        if w != mid:
            shift = jnp.where(ecol == erow + (HALO + w - mid), 1.0, 0.0).astype(BF16)
            acc = acc + jnp.dot(shift, ext, preferred_element_type=F32) * cw_ref[w:w + 1, :]
    xbc = _silu(acc)
    return xbc[:, :d_ssd], xbc[:, d_ssd:].astype(BF16)


def _ssd_scan(xs, bc, dt_raw, dtb_ref, a_ref, e2_ref, st_ref, rev):
    ck = SSD_CHUNK
    d_ssd = SSD_HEADS * SSD_HEADDIM
    d_bc = SSD_GROUPS * SSD_STATE
    gw = d_ssd // SSD_GROUPS
    lane = lax.broadcasted_iota(I32, (ck, LANES), 1)
    row = lax.broadcasted_iota(I32, (ck, LANES), 0)
    off = SSD_HEADS if rev else 0
    valid = (lane >= off) & (lane < off + SSD_HEADS)
    dtr = dt_raw + dtb_ref[...]
    dt = jnp.where(valid, jnp.maximum(dtr, 0.0) + jnp.log1p(jnp.exp(-jnp.abs(dtr))), 0.0)
    da = dt * a_ref[...]
    tri = (lane >= row) if rev else (lane <= row)
    trib = jnp.where(tri, 1.0, 0.0).astype(BF16)
    hi = da.astype(BF16)
    r1 = da - hi.astype(F32)
    mid = r1.astype(BF16)
    lo = (r1 - mid.astype(F32)).astype(BF16)
    cs = (jnp.dot(trib, hi, preferred_element_type=F32)
          + jnp.dot(trib, mid, preferred_element_type=F32)
          + jnp.dot(trib, lo, preferred_element_type=F32))
    cst = cs.T
    tot_row = 0 if rev else ck - 1
    tot = cs[tot_row:tot_row + 1, :]
    dec = jnp.where(valid, jnp.exp(tot - cs), 0.0)
    ecs = jnp.where(valid, jnp.exp(cs), 0.0)
    ecs_e = _expand_heads(ecs, e2_ref)
    xdt = xs * _expand_heads(dt, e2_ref)
    xdec = (xs * _expand_heads(dt * dec, e2_ref)).astype(BF16)

    lo_half = lax.broadcasted_iota(I32, (ck, LANES), 1) < SSD_HEADDIM
    hpg = SSD_HEADS // SSD_GROUPS
    ys = []
    for g in range(SSD_GROUPS):
        bg = bc[:, g * SSD_STATE:(g + 1) * SSD_STATE]
        cg = bc[:, d_bc + g * SSD_STATE:d_bc + (g + 1) * SSD_STATE]
        cbg = lax.dot_general(cg, bg, (((1,), (1,)), ((), ())), preferred_element_type=F32)
        yg = []
        for q in range(hpg // 2):
            ls = []
            for e in range(2):
                col = off + g * hpg + q * 2 + e
                dlt = cs[:, col:col + 1] - cst[col:col + 1, :]
                seg = jnp.exp(jnp.where(tri, dlt, -jnp.inf))
                ls.append((cbg * seg).astype(BF16))
            blk = g * (hpg // 2) + q
            xq = xdt[:, blk * LANES:(blk + 1) * LANES]
            rhs = jnp.concatenate([jnp.where(lo_half, xq, 0.0), jnp.where(lo_half, 0.0, xq)],
                                  axis=0).astype(BF16)
            yg.append(jnp.dot(jnp.concatenate(ls, axis=1), rhs, preferred_element_type=F32))
        y_diag = jnp.concatenate(yg, axis=1)
        st = st_ref[g]
        e_g = ecs_e[:, g * gw:(g + 1) * gw]
        y_off = jnp.dot(cg, st.astype(BF16), preferred_element_type=F32) * e_g
        ys.append(y_diag + y_off)
        upd = jnp.dot(bg.astype(F32).T.astype(BF16), xdec[:, g * gw:(g + 1) * gw],
                      preferred_element_type=F32)
        st_ref[g] = st * e_g[tot_row:tot_row + 1, :] + upd
    return jnp.concatenate(ys, axis=1)


def _ssd_kernel(*refs, rev):
    if rev:
        (xs_ref, bc_ref, dt_ref, z_ref, yf_ref, dtb_ref, a_ref, e2_ref, dsk_ref, ng_ref,
         out_ref, st_ref) = refs
    else:
        (xm_ref, xp_ref, xn_ref, dt_ref, cw_ref, cb_ref, dtb_ref, a_ref, e2_ref,
         out_ref, xs_out_ref, bc_out_ref, st_ref) = refs
    ck = SSD_CHUNK
    c = pl.program_id(1)
    nc = pl.num_programs(1)
    ce = nc - 1 - c if rev else c

    @pl.when(c == 0)
    def _():
        st_ref[...] = jnp.zeros_like(st_ref)

    rows = [slice(j * ck, (j + 1) * ck) for j in range(SSD_SUB)]
    if rev:
        conv = [(xs_ref[r, :], bc_ref[r, :]) for r in rows]
    else:
        main = xm_ref[...]
        zero_halo = jnp.zeros((HALO, main.shape[1]), main.dtype)
        ext = jnp.concatenate([jnp.where(ce == 0, zero_halo, xp_ref[...]), main,
                               jnp.where(ce == nc - 1, zero_halo, xn_ref[...])], axis=0)
        conv = [_ssd_conv(main[r], ext[r.start:r.stop + 2 * HALO], cw_ref, cb_ref) for r in rows]
        for r, (xs, bc) in zip(rows, conv):
            xs_out_ref[r, :] = xs
            bc_out_ref[r, :] = bc

    for j in (reversed(range(SSD_SUB)) if rev else range(SSD_SUB)):
        r = rows[j]
        xs, bc = conv[j]
        y = _ssd_scan(xs, bc, dt_ref[r, :], dtb_ref, a_ref, e2_ref, st_ref, rev)
        if rev:
            y = yf_ref[r, :] + y + dsk_ref[...] * xs
            z = z_ref[r, :].astype(F32)
            u = y * _silu(z)
            ms = jnp.mean(u * u, axis=-1, keepdims=True)
            out_ref[r, :] = (u * lax.rsqrt(ms + EPS) * ng_ref[...]).astype(BF16)
        else:
            out_ref[r, :] = y


def _ssd(proj, dt_raw, z_col, fwd, params, bsz, seq, rev):
    t = proj.shape[0]
    ck = SSD_CHUNK * SSD_SUB
    nc = seq // ck
    d_xbc = params["cw"].shape[1]
    d_ssd = SSD_HEADS * SSD_HEADDIM
    d_bc = d_xbc - d_ssd
    per = ck // HALO
    last_halo = t // HALO - 1

    def ce_of(c):
        return nc - 1 - c if rev else c

    main = lambda b, c: (b * nc + ce_of(c), 0)
    prev = lambda b, c: (jnp.maximum((b * nc + ce_of(c)) * per - 1, 0), 0)
    nxt = lambda b, c: (jnp.minimum((b * nc + ce_of(c)) * per + per, last_halo), 0)
    const2 = lambda b, c: (0, 0)
    d = "b" if rev else "f"
    dir_specs = [pl.BlockSpec((1, LANES), const2), pl.BlockSpec((1, LANES), const2),
                 pl.BlockSpec((LANES, d_ssd), const2)]
    dir_args = [params["dtb_" + d], params["a_" + d], params["e2_" + d]]
    if rev:
        yf, xs, bc = fwd
        in_specs = [pl.BlockSpec((ck, d_ssd), main), pl.BlockSpec((ck, d_bc), main),
                    pl.BlockSpec((ck, LANES), main),
                    pl.BlockSpec((ck, d_ssd), lambda b, c: (b * nc + ce_of(c), z_col)),
                    pl.BlockSpec((ck, d_ssd), main)] + dir_specs + [
                        pl.BlockSpec((1, d_ssd), const2), pl.BlockSpec((1, d_ssd), const2)]
        args = [xs, bc, dt_raw, proj, yf] + dir_args + [params["dsk"], params["ng"]]
        out_specs = pl.BlockSpec((ck, d_ssd), main)
        out_shape = jax.ShapeDtypeStruct((t, d_ssd), BF16)
    else:
        in_specs = [pl.BlockSpec((ck, d_xbc), main), pl.BlockSpec((HALO, d_xbc), prev),
                    pl.BlockSpec((HALO, d_xbc), nxt), pl.BlockSpec((ck, LANES), main),
                    pl.BlockSpec((8, d_xbc), const2), pl.BlockSpec((1, d_xbc), const2)] + dir_specs
        args = [proj, proj, proj, dt_raw, params["cw"], params["cb"]] + dir_args
        out_specs = [pl.BlockSpec((ck, d_ssd), main), pl.BlockSpec((ck, d_ssd), main),
                     pl.BlockSpec((ck, d_bc), main)]
        out_shape = [jax.ShapeDtypeStruct((t, d_ssd), F32), jax.ShapeDtypeStruct((t, d_ssd), F32),
                     jax.ShapeDtypeStruct((t, d_bc), BF16)]
    return pl.pallas_call(
        functools.partial(_ssd_kernel, rev=rev),
        grid=(bsz, nc),
        in_specs=in_specs,
        out_specs=out_specs,
        out_shape=out_shape,
        scratch_shapes=[pltpu.VMEM((SSD_GROUPS, SSD_STATE, d_ssd // SSD_GROUPS), F32)],
        compiler_params=_cparams(("arbitrary", "arbitrary")),
        name="ssd_bwd" if rev else "ssd_fwd",
    )(*args)


def _na_plan(rows):
    nblk = rows // NA_QROWS
    variants, var_of_blk, ks_of_blk = [], [], []
    for i in range(nblk):
        ks = int(np.clip(i * NA_QROWS - NA_KH // 2, 0, rows - NA_KROWS))
        drow = np.zeros((NA_QROWS, NA_KROWS), np.int64)
        ok = np.zeros((NA_QROWS, NA_KROWS), bool)
        for dr in range(NA_QROWS):
            r = i * NA_QROWS + dr
            rs = int(np.clip(r - NA_KH // 2, 0, rows - NA_KH))
            for j in range(NA_KH):
                w = rs + j - ks
                drow[dr, w] = rs + j - r + NA_KH - 1
                ok[dr, w] = True
        key = (drow.tobytes(), ok.tobytes())
        for vi, (k2, _, _) in enumerate(variants):
            if k2 == key:
                var_of_blk.append(vi)
                break
        else:
            var_of_blk.append(len(variants))
            variants.append((key, drow, ok))
        ks_of_blk.append(ks)
    return [(d, o) for _, d, o in variants], var_of_blk, ks_of_blk


def _na_blk_start():
    ncb = GRID_W // NA_QB
    return [int(v) for v in np.clip(np.arange(ncb) * NA_QB - NA_KW // 2, 0, GRID_W - NA_KB)]


def _na_bias(rpb, rows):
    variants, var_of_blk, ks_of_blk = _na_plan(rows)
    nh = rpb.shape[0]
    cols = np.arange(GRID_W)
    win_start = np.clip(cols - NA_KW // 2, 0, GRID_W - NA_KW)
    in_win = (cols[None, :] >= win_start[:, None]) & (cols[None, :] < win_start[:, None] + NA_KW)
    dcol = np.clip(cols[None, :] - cols[:, None] + NA_KW - 1, 0, 2 * NA_KW - 2)
    onehot = (dcol[:, :, None] == np.arange(2 * NA_KW - 1)).astype(np.float32)
    band = jnp.einsum("hab,qkb->haqk", rpb.astype(F32), onehot, precision=lax.Precision.HIGHEST)
    band = jnp.where(in_win[None, None], band, -jnp.inf)
    band = band.reshape((nh // 2, 2) + band.shape[1:])
    neg = jnp.full((nh // 2, 2, NA_QB, NA_KB), -jnp.inf, F32)
    mq, nk = NA_QROWS * NA_QB, NA_KROWS * NA_KB
    tabs = []
    for drow, ok in variants:
        per_cb = []
        for m, bs in enumerate(_na_blk_start()):
            qrows = [jnp.concatenate(
                [band[:, :, drow[dr, w], m * NA_QB:(m + 1) * NA_QB, bs:bs + NA_KB] if ok[dr, w]
                 else neg for w in range(NA_KROWS)], axis=3) for dr in range(NA_QROWS)]
            per_cb.append(jnp.concatenate(qrows, axis=2).reshape(nh // 2, 2 * mq, nk))
        tabs.append(jnp.stack(per_cb, axis=1))
    return jnp.stack(tabs), var_of_blk, ks_of_blk


def _na_kernel(var_ref, ks_ref, q_ref, k_ref, v_ref, bias_ref, o_ref, kf_ref, vf_ref):
    mq = NA_QROWS * NA_QB
    nblk = q_ref.shape[0] // (NA_QROWS * GRID_W)
    lo_half = lax.broadcasted_iota(I32, (mq, LANES), 1) < NA_HEADDIM
    kf_ref[...] = k_ref[...].astype(F32)
    vf_ref[...] = v_ref[...].astype(F32)

    def blk(i, carry):
        var = var_ref[i]
        ks = ks_ref[i]
        starts = _na_blk_start()
        q_at = [[pl.multiple_of((i * NA_QROWS + dr) * GRID_W + m * NA_QB, NA_QB)
                 for dr in range(NA_QROWS)] for m in range(len(starts))]
        k_at = [[pl.multiple_of((ks + w) * GRID_W + bs, 8) for w in range(NA_KROWS)]
                for bs in starts]
        scores = []
        for m in range(len(starts)):
            qb = jnp.concatenate([q_ref[pl.ds(o, NA_QB), :] for o in q_at[m]], axis=0)
            qb = qb * (NA_HEADDIM ** -0.5)
            zero = jnp.zeros_like(qb)
            q2 = jnp.concatenate([jnp.where(lo_half, qb, zero), jnp.where(lo_half, zero, qb)], axis=0)
            kb = jnp.concatenate([kf_ref[pl.ds(o, NA_KB), :] for o in k_at[m]], axis=0).astype(BF16)
            s = lax.dot_general(q2, kb, (((1,), (1,)), ((), ())), preferred_element_type=F32)
            scores.append(s + bias_ref[var, 0, m])
        outs = []
        for m, s in enumerate(scores):
            p = jnp.exp(s - jnp.max(s, axis=-1, keepdims=True))
            l = jnp.sum(p, axis=-1, keepdims=True)
            vb = jnp.concatenate([vf_ref[pl.ds(o, NA_KB), :] for o in k_at[m]], axis=0).astype(BF16)
            outs.append(jnp.dot(p.astype(BF16), vb, preferred_element_type=F32) / l)
        for m, o2 in enumerate(outs):
            ob = jnp.where(lo_half, o2[:mq], o2[mq:]).astype(BF16)
            for dr, o in enumerate(q_at[m]):
                o_ref[pl.ds(o, NA_QB), :] = ob[dr * NA_QB:(dr + 1) * NA_QB]
        return carry

    lax.fori_loop(0, nblk, blk, 0, unroll=4)


def _na(proj, bias, var_of_blk, ks_of_blk, q_col, bsz, seq):
    t = proj.shape[0]
    npair = NA_HEADS // 2
    grid_spec = pltpu.PrefetchScalarGridSpec(
        num_scalar_prefetch=2,
        grid=(bsz, npair),
        in_specs=[pl.BlockSpec((seq, LANES), lambda b, p, *_: (b, q_col + p)),
                  pl.BlockSpec((seq, LANES), lambda b, p, *_: (b, q_col + npair + p)),
                  pl.BlockSpec((seq, LANES), lambda b, p, *_: (b, q_col + 2 * npair + p)),
                  pl.BlockSpec((bias.shape[0], 1) + bias.shape[2:], lambda b, p, *_: (0, p, 0, 0, 0))],
        out_specs=pl.BlockSpec((seq, LANES), lambda b, p, *_: (b, p)),
        scratch_shapes=[pltpu.VMEM((seq, LANES), F32), pltpu.VMEM((seq, LANES), F32)],
    )
    return pl.pallas_call(
        _na_kernel,
        grid_spec=grid_spec,
        out_shape=jax.ShapeDtypeStruct((t, NA_HEADS * NA_HEADDIM), BF16),
        compiler_params=_cparams(("arbitrary", "arbitrary")),
        name="na",
    )(jnp.asarray(var_of_blk, I32), jnp.asarray(ks_of_blk, I32), proj, proj, proj, bias)


def _mix_kernel(x_ref, yn_ref, at_ref, gs_ref, gn_ref, g1_ref, sh_ref, sc_ref, ng_ref,
                ws_ref, wn_ref, wo_ref, wq_ref, keys_ref, h_ref, s_ref):
    y1 = jnp.dot(yn_ref[...], ws_ref[...], preferred_element_type=F32)
    y2 = jnp.dot(at_ref[...], wn_ref[...], preferred_element_type=F32)
    mixed = _sigmoid(gs_ref[...].astype(F32)) * y1 + _sigmoid(gn_ref[...].astype(F32)) * y2
    h = x_ref[...] + g1_ref[0] * jnp.dot(mixed.astype(BF16), wo_ref[...], preferred_element_type=F32)
    h_ref[...] = h
    n2 = _rms_mod(h, ng_ref[...], sh_ref[0], sc_ref[0]).astype(BF16)
    q = jnp.dot(n2, wq_ref[...], preferred_element_type=F32).astype(BF16)
    sub = 8
    for hz in range(2 * PEER_HEADS):
        qs = q[:, hz * PEER_DHALF:(hz + 1) * PEER_DHALF]
        sc = lax.dot_general(keys_ref[hz % 2], qs, (((1,), (1,)), ((), ())),
                             preferred_element_type=F32)
        for kq in range(PEER_NKEYS // sub):
            for tb in range(sc.shape[1] // LANES):
                s_ref[hz, kq, tb * sub:(tb + 1) * sub, :] = sc[kq * sub:(kq + 1) * sub,
                                                               tb * LANES:(tb + 1) * LANES]


def _mix(x2, ynorm, attn, proj, gs_col, g1, sh2, sc2, ng2, ws, wn, wo, wq, keys, seq):
    t, d = x2.shape
    tm = min(512, seq)
    bidx = lambda i: ((i * tm) // seq, 0, 0)
    row = lambda i: (i, 0)
    const2 = lambda i: (0, 0)
    return pl.pallas_call(
        _mix_kernel,
        grid=(t // tm,),
        in_specs=[pl.BlockSpec((tm, d), row), pl.BlockSpec((tm, d), row), pl.BlockSpec((tm, d), row),
                  pl.BlockSpec((tm, d), lambda i: (i, gs_col)),
                  pl.BlockSpec((tm, d), lambda i: (i, gs_col + 1)),
                  pl.BlockSpec((1, 1, d), bidx), pl.BlockSpec((1, 1, d), bidx),
                  pl.BlockSpec((1, 1, d), bidx), pl.BlockSpec((1, d), const2),
                  pl.BlockSpec((d, d), const2, pipeline_mode=pl.Buffered(1)),
                  pl.BlockSpec((d, d), const2, pipeline_mode=pl.Buffered(1)),
                  pl.BlockSpec((d, d), const2, pipeline_mode=pl.Buffered(1)),
                  pl.BlockSpec(wq.shape, const2, pipeline_mode=pl.Buffered(1)),
                  pl.BlockSpec(keys.shape, lambda i: (0, 0, 0), pipeline_mode=pl.Buffered(1))],
        out_specs=[pl.BlockSpec((tm, d), row),
                   pl.BlockSpec((2 * PEER_HEADS, PEER_NKEYS // 8, tm // LANES * 8, LANES),
                                lambda i: (0, 0, i, 0))],
        out_shape=[jax.ShapeDtypeStruct((t, d), F32),
                   jax.ShapeDtypeStruct((2 * PEER_HEADS, PEER_NKEYS // 8, t // LANES * 8, LANES), F32)],
        compiler_params=_cparams(("arbitrary",)),
        name="mix",
    )(x2, ynorm, attn, proj, proj, g1, sh2, sc2, ng2, ws, wn, wo, wq, keys)


def _hyperbola():
    return [(i, k) for i in range(PEER_TOPK) for k in range(PEER_TOPK)
            if (i + 1) * (k + 1) <= PEER_TOPK]


def _sort_network(n):
    pairs = []
    p = 1
    while p < n:
        k = p
        while k >= 1:
            for j in range(k % p, n - k, 2 * k):
                for i in range(min(k, n - j - k)):
                    if (i + j) // (2 * p) == (i + j + k) // (2 * p):
                        pairs.append((i + j, i + j + k))
            k //= 2
        p *= 2
    return pairs


def _precedes(va, pa, vb, pb):
    return (va > vb) | ((va == vb) & (pa < pb))


def _compare_exchange(v, p, i, j):
    c = _precedes(v[i], p[i], v[j], p[j])
    v[i], v[j] = jnp.where(c, v[i], v[j]), jnp.where(c, v[j], v[i])
    p[i], p[j] = jnp.where(c, p[i], p[j]), jnp.where(c, p[j], p[i])


def _top16(get, n, sv_ref, sp_ref):
    k = PEER_TOPK
    ng = n // k
    net = _sort_network(k)
    for g in range(ng):
        items = [get(g * k + j) for j in range(k)]
        v = [it[0] for it in items]
        p = [it[1] for it in items]
        for i, j in net:
            _compare_exchange(v, p, i, j)
        if ng == 1:
            return v, p
        for j in range(k):
            sv_ref[g, j] = v[j]
            sp_ref[g, j] = p[j]
    step = 1
    while True:
        for g in range(0, ng, 2 * step):
            v, p = [], []
            for j in range(k):
                xv, xp = sv_ref[g, j], sp_ref[g, j]
                yv, yp = sv_ref[g + step, k - 1 - j], sp_ref[g + step, k - 1 - j]
                c = _precedes(xv, xp, yv, yp)
                v.append(jnp.where(c, xv, yv))
                p.append(jnp.where(c, xp, yp))
            stride = k // 2
            while stride >= 1:
                for i in range(k):
                    if i & stride == 0:
                        _compare_exchange(v, p, i, i + stride)
                stride //= 2
            if 2 * step >= ng:
                return v, p
            for j in range(k):
                sv_ref[g, j] = v[j]
                sp_ref[g, j] = p[j]
        step *= 2


def _select_kernel(s_ref, eidx_ref, gate_ref, sv_ref, sp_ref):
    nk = PEER_NKEYS
    k = PEER_TOPK
    sub = 8
    shape = (sub, LANES)
    ebits = (nk * nk - 1).bit_length()

    def key(z, j):
        return s_ref[z, j // sub, pl.ds(j % sub, sub, stride=sub), :], jnp.full(shape, j, I32)

    tops = []
    for z in range(2):
        tops.append(_top16(functools.partial(key, z), nk, sv_ref, sp_ref))
    (tv0, ti0), (tv1, ti1) = tops

    pairs = _hyperbola()
    npad = -len(pairs) % k

    def cand(j):
        if j >= len(pairs):
            return jnp.full(shape, -jnp.inf, F32), jnp.full(shape, (k * k) << ebits, I32)
        i, kk = pairs[j]
        return tv0[i] + tv1[kk], ((i * k + kk) << ebits) + ti0[i] * nk + ti1[kk]

    top, ids = _top16(cand, len(pairs) + npad, sv_ref, sp_ref)
    ex = [jnp.exp(t - top[0]) for t in top]
    zsum = ex[0]
    for r in range(1, k):
        zsum = zsum + ex[r]
    for r in range(k):
        eidx_ref[0, r] = ids[r] & ((1 << ebits) - 1)
        gate_ref[0, r] = ex[r] / zsum


def _select(scores):
    r = 8
    nk = PEER_NKEYS
    tb = scores.shape[2] // r
    assert tb % r == 0
    oshape = (PEER_HEADS, PEER_TOPK, tb, LANES)
    return pl.pallas_call(
        _select_kernel,
        grid=(tb // r, PEER_HEADS),
        in_specs=[pl.BlockSpec((2, nk // r, r * r, LANES), lambda i, h: (h, 0, i, 0))],
        out_specs=[pl.BlockSpec((1, PEER_TOPK, r, LANES), lambda i, h: (h, 0, i, 0)),
                   pl.BlockSpec((1, PEER_TOPK, r, LANES), lambda i, h: (h, 0, i, 0))],
        out_shape=[jax.ShapeDtypeStruct(oshape, I32), jax.ShapeDtypeStruct(oshape, F32)],
        scratch_shapes=[pltpu.VMEM((nk // PEER_TOPK, PEER_TOPK, r, LANES), F32),
                        pltpu.VMEM((nk // PEER_TOPK, PEER_TOPK, r, LANES), I32)],
        compiler_params=_cparams(("arbitrary", "arbitrary")),
        name="select",
    )(scores)


def _peer_kernel(h_ref, sh_ref, sc_ref, g2_ref, ng_ref, fg_ref, eidx_ref, gate_ref, uv_ref,
                 out_ref, n2_ref, act_ref, wv_ref, w3_ref, acc_ref, *, ns):
    nk = PEER_NKEYS
    half = nk // 2
    hi_mask = jnp.uint32(0xFFFF0000)
    tm = h_ref.shape[0]
    eb = uv_ref.shape[0]
    nblk = eb // nk
    nblk2 = nblk // 2
    s = pl.program_id(1)

    @pl.when(s == 0)
    def _():
        n2_ref[...] = _rms_mod(h_ref[...], ng_ref[...], sh_ref[0], sc_ref[0]).astype(BF16)
        act_ref[...] = jnp.zeros_like(act_ref)

    @pl.when(s < ns)
    def _():
        e = eidx_ref[...]
        ai = jnp.right_shift(e, nk.bit_length() - 1)
        bi = jnp.bitwise_and(e, nk - 1)
        n2 = n2_ref[...]
        for a in range(0, nblk, 2):
            sc = lax.dot_general(n2, uv_ref[a * nk:(a + 2) * nk, :], (((1,), (1,)), ((), ())),
                                 preferred_element_type=F32)
            act = act_ref[...]
            for j in range(2):
                got = jnp.take_along_axis(sc[:, j * nk:(j + 1) * nk], bi, axis=1)
                act = jnp.where(ai == s * nblk + a + j, got, act)
            act_ref[...] = act

    @pl.when(s == ns - 1)
    def _():
        act = act_ref[...]
        gelu = 0.5 * act * (1.0 + lax.erf(act * (2.0 ** -0.5)))
        wv_ref[...] = gate_ref[...] * gelu
        r = lax.broadcasted_iota(I32, (nk, LANES), 0)
        sub = r.astype(F32).astype(BF16)
        rowkey = jnp.where(r < half, 2 * r, 2 * (r - half) + 1).astype(F32).astype(BF16)
        zero = jnp.zeros((nk, LANES), BF16)
        one = jnp.ones((nk, LANES), BF16)

        def tok(t, carry):
            e = eidx_ref[pl.ds(t, 1), :]
            w = wv_ref[pl.ds(t, 1), :].astype(BF16)
            ar = jnp.right_shift(e, nk.bit_length() - 1).astype(F32).astype(BF16)
            br = jnp.bitwise_and(e, nk - 1).astype(F32).astype(BF16)
            pt = jnp.where(rowkey == ar, w, zero)
            qt = jnp.where(sub == br, one, zero)
            wt = lax.dot_general(pt, qt, (((1,), (1,)), ((), ())), preferred_element_type=F32)
            bits = lax.bitcast_convert_type(wt, jnp.uint32)
            word = jnp.right_shift(bits[:half], 16) | (bits[half:] & hi_mask)
            w3_ref[pl.ds(pl.multiple_of(t * W3_PITCH, 8), half), :] = word
            return carry
        lax.fori_loop(0, tm, tok, 0, unroll=32)

    @pl.when(s >= ns)
    def _():
        base = (s - ns) * nblk2
        words = [w3_ref[pl.ds(base + a, tm, stride=W3_PITCH), :] for a in range(nblk2)]
        lo = [lax.bitcast_convert_type(jnp.left_shift(w, 16), F32).astype(BF16) for w in words]
        hi = [lax.bitcast_convert_type(w & hi_mask, F32).astype(BF16) for w in words]
        cols = [c for pair in zip(lo, hi) for c in pair]
        part = jnp.dot(jnp.concatenate(cols, axis=1), uv_ref[...], preferred_element_type=F32)

        @pl.when(s == ns)
        def _():
            acc_ref[...] = part

        @pl.when(s > ns)
        def _():
            acc_ref[...] += part

    @pl.when(s == 2 * ns - 1)
    def _():
        hh = h_ref[...] + g2_ref[0] * acc_ref[...]
        ms = jnp.mean(hh * hh, axis=-1, keepdims=True)
        out_ref[...] = hh * lax.rsqrt(ms + EPS) * fg_ref[...]


def _peer(h, sh2, sc2, g2, ng2, fg, eidx_t, gate_t, u, v, seq):
    t, d = h.shape
    ne = u.shape[0]
    tm = min(512, seq)
    eb = 4096
    ns = ne // eb
    nj = eidx_t.shape[1]
    uv = jnp.concatenate([u, v], axis=0).astype(BF16)
    bidx = lambda i, s: ((i * tm) // seq, 0, 0)
    row = lambda i, s: (i, 0)
    const2 = lambda i, s: (0, 0)
    return pl.pallas_call(
        functools.partial(_peer_kernel, ns=ns),
        grid=(t // tm, 2 * ns),
        in_specs=[pl.BlockSpec((tm, d), row),
                  pl.BlockSpec((1, 1, d), bidx), pl.BlockSpec((1, 1, d), bidx),
                  pl.BlockSpec((1, 1, d), bidx),
                  pl.BlockSpec((1, d), const2), pl.BlockSpec((1, d), const2),
                  pl.BlockSpec((tm, nj), row), pl.BlockSpec((tm, nj), row),
                  pl.BlockSpec((eb, d), lambda i, s: (s, 0))],
        out_specs=pl.BlockSpec((tm, d), row),
        out_shape=jax.ShapeDtypeStruct((t, d), F32),
        scratch_shapes=[pltpu.VMEM((tm, d), BF16),
                        pltpu.VMEM((tm, nj), F32),
                        pltpu.VMEM((tm, nj), F32),
                        pltpu.VMEM((tm * W3_PITCH, PEER_NKEYS), jnp.uint32),
                        pltpu.VMEM((tm, d), F32)],
        compiler_params=_cparams(("arbitrary", "arbitrary")),
        name="peer",
    )(h, sh2, sc2, g2, ng2, fg, eidx_t, gate_t, uv)


def _layer(x2, c, bsz, seq, w_ada, b_ada, norm1_g, w_in, conv_w, conv_b, dt_bias_f, dt_bias_b,
           a_log_f, a_log_b, d_skip, ssd_norm_g, w_ssd_br, na_rpb, w_na_br, w_out, norm2_g,
           peer_wq, peer_keys, peer_u, peer_v, out_g):
    t, d = x2.shape
    d_ssd = SSD_HEADS * SSD_HEADDIM
    d_xbc = d_ssd + 2 * SSD_GROUPS * SSD_STATE
    d_na = NA_HEADS * NA_HEADDIM
    assert seq % (SSD_CHUNK * SSD_SUB) == 0 and seq % (GRID_W * NA_QROWS) == 0
    assert seq // GRID_W >= NA_KROWS and d == d_ssd == d_na

    mod = _ada(c, w_ada, b_ada)
    sh1, sc1, g1, sh2, sc2, g2 = [m.reshape(bsz, 1, d) for m in jnp.split(mod, 6, axis=-1)]

    o = np.cumsum([0, d_ssd, d_xbc, SSD_HEADS, SSD_HEADS, 3 * d_na, d, d])
    w_main = jnp.concatenate([w_in[:, o[1]:o[2]], w_in[:, o[0]:o[1]], w_in[:, o[4]:o[7]]],
                             axis=1).astype(BF16)
    w_dt = jnp.pad(w_in[:, o[2]:o[4]], ((0, 0), (0, LANES - 2 * SSD_HEADS))).astype(BF16)
    z_col = d_xbc // d_ssd
    q_col = (d_xbc + d_ssd) // LANES
    gs_col = (d_xbc + d_ssd + 3 * d_na) // d
    proj, dt_raw = _inproj(x2, sh1, sc1, norm1_g.reshape(1, d), w_main, w_dt, seq)

    heads = np.arange(d_ssd) // SSD_HEADDIM

    def e2(off):
        m = np.zeros((LANES, d_ssd), np.float32)
        m[off + heads, np.arange(d_ssd)] = 1.0
        m[off + 64 + heads, np.arange(d_ssd)] = 1.0
        return jnp.asarray(m, BF16)

    def lanes16(vec, off):
        return jnp.zeros((1, LANES), F32).at[0, off:off + SSD_HEADS].set(vec.astype(F32))

    params = {
        "cw": jnp.pad(conv_w.astype(F32), ((0, 8 - CONV_W), (0, 0))),
        "cb": conv_b.astype(F32).reshape(1, d_xbc),
        "dtb_f": lanes16(dt_bias_f, 0), "dtb_b": lanes16(dt_bias_b, SSD_HEADS),
        "a_f": lanes16(-jnp.exp(a_log_f.astype(F32)), 0),
        "a_b": lanes16(-jnp.exp(a_log_b.astype(F32)), SSD_HEADS),
        "e2_f": e2(0), "e2_b": e2(SSD_HEADS),
        "dsk": jnp.repeat(d_skip.astype(F32), SSD_HEADDIM).reshape(1, d_ssd),
        "ng": ssd_norm_g.astype(F32).reshape(1, d_ssd),
    }
    fwd = _ssd(proj, dt_raw, z_col, None, params, bsz, seq, rev=False)
    ynorm = _ssd(proj, dt_raw, z_col, fwd, params, bsz, seq, rev=True)

    bias, var_of_blk, ks_of_blk = _na_bias(na_rpb, seq // GRID_W)
    attn = _na(proj, bias, var_of_blk, ks_of_blk, q_col, bsz, seq)

    h, scores = _mix(x2, ynorm, attn, proj, gs_col, g1, sh2, sc2, norm2_g.reshape(1, d),
                     w_ssd_br.astype(BF16), w_na_br.astype(BF16), w_out.astype(BF16),
                     peer_wq.astype(BF16), peer_keys.astype(BF16), seq)

    eidx, gate = _select(scores)
    nj = PEER_HEADS * PEER_TOPK
    eidx_t = eidx.reshape(nj, t).T
    gate_t = gate.reshape(nj, t).T
    return _peer(h, sh2, sc2, g2, norm2_g.reshape(1, d), out_g, eidx_t, gate_t,
                 peer_u.astype(BF16), peer_v.astype(BF16), seq)


def kernel(x, c, w_ada, b_ada, norm1_g, w_in, conv_w, conv_b, dt_bias_f, dt_bias_b, a_log_f, a_log_b,
           d_skip, ssd_norm_g, w_ssd_br, na_rpb, w_na_br, w_out, norm2_g, peer_wq, peer_keys, peer_u,
           peer_v, final_g):
    bsz, seq, d = x.shape
    depth = w_ada.shape[0]
    assert depth == 1, "the final RMSNorm is fused into the last layer's PEER kernel"
    i = 0
    out = _layer(x.reshape(bsz * seq, d), c, bsz, seq, w_ada[i], b_ada[i], norm1_g[i], w_in[i],
                 conv_w[i], conv_b[i], dt_bias_f[i], dt_bias_b[i], a_log_f[i], a_log_b[i], d_skip[i],
                 ssd_norm_g[i], w_ssd_br[i], na_rpb[i], w_na_br[i], w_out[i], norm2_g[i],
                 peer_wq[i], peer_keys[i], peer_u[i], peer_v[i], final_g.reshape(1, d))
    return out.reshape(bsz, seq, d)
```

```python
import functools

import numpy as np
import jax
import jax.numpy as jnp
from jax import lax
from jax.experimental import pallas as pl
from jax.experimental.pallas import tpu as pltpu

F32 = jnp.float32
BF16 = jnp.bfloat16
I32 = jnp.int32

EPS = 1e-6
GRID_W = 64
SSD_HEADS = 16
SSD_HEADDIM = 64
SSD_GROUPS = 4
SSD_STATE = 128
SSD_CHUNK = 128
SSD_SUB = 2
CONV_W = 5
NA_HEADS = 16
NA_HEADDIM = 64
NA_KH = 8
NA_KW = 16
NA_QROWS = 4
NA_KROWS = 12
NA_QB = 16
NA_KB = NA_QB + NA_KW
PEER_HEADS = 8
PEER_NKEYS = 128
PEER_TOPK = 16
PEER_DHALF = 128
W3_PITCH = PEER_NKEYS // 2 + 8

LANES = 128
HALO = 16
VMEM_LIMIT = 56 * 1024 * 1024


def _cparams(sem):
    return pltpu.CompilerParams(dimension_semantics=sem, vmem_limit_bytes=VMEM_LIMIT)


def _sigmoid(x):
    return 0.5 * jnp.tanh(0.5 * x) + 0.5


def _silu(x):
    h = 0.5 * x
    return h * (jnp.tanh(h) + 1.0)


def _rms_mod(x, g, shift, scale):
    ms = jnp.mean(x * x, axis=-1, keepdims=True)
    y = x * lax.rsqrt(ms + EPS) * g
    return y * (1.0 + scale) + shift


def _ada_kernel(c_ref, w_ref, b_ref, o_ref):
    c = c_ref[...]
    sc = _silu(c)
    o_ref[...] = jnp.dot(sc, w_ref[...], preferred_element_type=F32,
                         precision=lax.Precision.HIGHEST) + b_ref[...]


def _ada(c, w, b):
    bsz, d = c.shape
    n = w.shape[1]
    tn = 1024
    return pl.pallas_call(
        _ada_kernel,
        grid=(n // tn,),
        in_specs=[pl.BlockSpec((bsz, d), lambda j: (0, 0)),
                  pl.BlockSpec((d, tn), lambda j: (0, j)),
                  pl.BlockSpec((1, tn), lambda j: (0, j))],
        out_specs=pl.BlockSpec((bsz, tn), lambda j: (0, j)),
        out_shape=jax.ShapeDtypeStruct((bsz, n), F32),
        compiler_params=_cparams(("arbitrary",)),
        name="ada",
    )(c, w, b.reshape(1, n))


def _inproj_kernel(x_ref, sh_ref, sc_ref, g_ref, w_ref, wdt_ref, proj_ref, dt_ref):
    n1 = _rms_mod(x_ref[...], g_ref[...], sh_ref[0], sc_ref[0]).astype(BF16)
    dt_ref[...] = jnp.dot(n1, wdt_ref[...], preferred_element_type=F32)
    tn = 1024
    for j in range(w_ref.shape[1] // tn):
        proj_ref[:, j * tn:(j + 1) * tn] = jnp.dot(
            n1, w_ref[:, j * tn:(j + 1) * tn], preferred_element_type=F32).astype(BF16)


def _inproj(x2, sh, sc, g, w, wdt, seq):
    t, d = x2.shape
    n = w.shape[1]
    tm = min(512, seq)
    bidx = lambda i: ((i * tm) // seq, 0, 0)
    const2 = lambda i: (0, 0)
    return pl.pallas_call(
        _inproj_kernel,
        grid=(t // tm,),
        in_specs=[pl.BlockSpec((tm, d), lambda i: (i, 0)),
                  pl.BlockSpec((1, 1, d), bidx),
                  pl.BlockSpec((1, 1, d), bidx),
                  pl.BlockSpec((1, d), const2),
                  pl.BlockSpec((d, n), const2, pipeline_mode=pl.Buffered(1)),
                  pl.BlockSpec((d, LANES), const2, pipeline_mode=pl.Buffered(1))],
        out_specs=[pl.BlockSpec((tm, n), lambda i: (i, 0)),
                   pl.BlockSpec((tm, LANES), lambda i: (i, 0))],
        out_shape=[jax.ShapeDtypeStruct((t, n), BF16),
                   jax.ShapeDtypeStruct((t, LANES), F32)],
        compiler_params=_cparams(("arbitrary",)),
        name="inproj",
    )(x2, sh, sc, g, w, wdt)


def _expand_heads(v, e2_ref):
    hi = v.astype(BF16).astype(F32)
    comb = (hi + pltpu.roll(v - hi, 64, 1)).astype(BF16)
    return jnp.dot(comb, e2_ref[...], preferred_element_type=F32)


def _ssd_conv(main, ext, cw_ref, cb_ref):
    ck = main.shape[0]
    d_ssd = SSD_HEADS * SSD_HEADDIM
    erow = lax.broadcasted_iota(I32, (ck, ck + 2 * HALO), 0)
    ecol = lax.broadcasted_iota(I32, (ck, ck + 2 * HALO), 1)
    mid = CONV_W // 2
    acc = cb_ref[...] + main.astype(F32) * cw_ref[mid:mid + 1, :]
    for w in range(CONV_W):
        if w != mid:
            shift = jnp.where(ecol == erow + (HALO + w - mid), 1.0, 0.0).astype(BF16)
            acc = acc + jnp.dot(shift, ext, preferred_element_type=F32) * cw_ref[w:w + 1, :]
    xbc = _silu(acc)
    return xbc[:, :d_ssd], xbc[:, d_ssd:].astype(BF16)


def _ssd_scan(xs, bc, dt_raw, dtb_ref, a_ref, e2_ref, st_ref, rev):
    ck = SSD_CHUNK
    d_ssd = SSD_HEADS * SSD_HEADDIM
    d_bc = SSD_GROUPS * SSD_STATE
    gw = d_ssd // SSD_GROUPS
    lane = lax.broadcasted_iota(I32, (ck, LANES), 1)
    row = lax.broadcasted_iota(I32, (ck, LANES), 0)
    off = SSD_HEADS if rev else 0
    valid = (lane >= off) & (lane < off + SSD_HEADS)
    dtr = dt_raw + dtb_ref[...]
    dt = jnp.where(valid, jnp.maximum(dtr, 0.0) + jnp.log1p(jnp.exp(-jnp.abs(dtr))), 0.0)
    da = dt * a_ref[...]
    tri = (lane >= row) if rev else (lane <= row)
    trib = jnp.where(tri, 1.0, 0.0).astype(BF16)
    hi = da.astype(BF16)
    r1 = da - hi.astype(F32)
    mid = r1.astype(BF16)
    lo = (r1 - mid.astype(F32)).astype(BF16)
    cs = (jnp.dot(trib, hi, preferred_element_type=F32)
          + jnp.dot(trib, mid, preferred_element_type=F32)
          + jnp.dot(trib, lo, preferred_element_type=F32))
    cst = cs.T
    tot_row = 0 if rev else ck - 1
    tot = cs[tot_row:tot_row + 1, :]
    dec = jnp.where(valid, jnp.exp(tot - cs), 0.0)
    ecs = jnp.where(valid, jnp.exp(cs), 0.0)
    ecs_e = _expand_heads(ecs, e2_ref)
    xdt = xs * _expand_heads(dt, e2_ref)
    xdec = (xs * _expand_heads(dt * dec, e2_ref)).astype(BF16)

    lo_half = lax.broadcasted_iota(I32, (ck, LANES), 1) < SSD_HEADDIM
    hpg = SSD_HEADS // SSD_GROUPS
    ys = []
    for g in range(SSD_GROUPS):
        bg = bc[:, g * SSD_STATE:(g + 1) * SSD_STATE]
        cg = bc[:, d_bc + g * SSD_STATE:d_bc + (g + 1) * SSD_STATE]
        cbg = lax.dot_general(cg, bg, (((1,), (1,)), ((), ())), preferred_element_type=F32)
        yg = []
        for q in range(hpg // 2):
            ls = []
            for e in range(2):
                col = off + g * hpg + q * 2 + e
                dlt = cs[:, col:col + 1] - cst[col:col + 1, :]
                seg = jnp.exp(jnp.where(tri, dlt, -jnp.inf))
                ls.append((cbg * seg).astype(BF16))
            blk = g * (hpg // 2) + q
            xq = xdt[:, blk * LANES:(blk + 1) * LANES]
            rhs = jnp.concatenate([jnp.where(lo_half, xq, 0.0), jnp.where(lo_half, 0.0, xq)],
                                  axis=0).astype(BF16)
            yg.append(jnp.dot(jnp.concatenate(ls, axis=1), rhs, preferred_element_type=F32))
        y_diag = jnp.concatenate(yg, axis=1)
        st = st_ref[g]
        e_g = ecs_e[:, g * gw:(g + 1) * gw]
        y_off = jnp.dot(cg, st.astype(BF16), preferred_element_type=F32) * e_g
        ys.append(y_diag + y_off)
        upd = jnp.dot(bg.astype(F32).T.astype(BF16), xdec[:, g * gw:(g + 1) * gw],
                      preferred_element_type=F32)
        st_ref[g] = st * e_g[tot_row:tot_row + 1, :] + upd
    return jnp.concatenate(ys, axis=1)


def _ssd_kernel(*refs, rev):
    if rev:
        (xs_ref, bc_ref, dt_ref, z_ref, yf_ref, dtb_ref, a_ref, e2_ref, dsk_ref, ng_ref,
         out_ref, st_ref) = refs
    else:
        (xm_ref, xp_ref, xn_ref, dt_ref, cw_ref, cb_ref, dtb_ref, a_ref, e2_ref,
         out_ref, xs_out_ref, bc_out_ref, st_ref) = refs
    ck = SSD_CHUNK
    c = pl.program_id(1)
    nc = pl.num_programs(1)
    ce = nc - 1 - c if rev else c

    @pl.when(c == 0)
    def _():
        st_ref[...] = jnp.zeros_like(st_ref)

    rows = [slice(j * ck, (j + 1) * ck) for j in range(SSD_SUB)]
    if rev:
        conv = [(xs_ref[r, :], bc_ref[r, :]) for r in rows]
    else:
        main = xm_ref[...]
        zero_halo = jnp.zeros((HALO, main.shape[1]), main.dtype)
        ext = jnp.concatenate([jnp.where(ce == 0, zero_halo, xp_ref[...]), main,
                               jnp.where(ce == nc - 1, zero_halo, xn_ref[...])], axis=0)
        conv = [_ssd_conv(main[r], ext[r.start:r.stop + 2 * HALO], cw_ref, cb_ref) for r in rows]
        for r, (xs, bc) in zip(rows, conv):
            xs_out_ref[r, :] = xs
            bc_out_ref[r, :] = bc

    for j in (reversed(range(SSD_SUB)) if rev else range(SSD_SUB)):
        r = rows[j]
        xs, bc = conv[j]
        y = _ssd_scan(xs, bc, dt_ref[r, :], dtb_ref, a_ref, e2_ref, st_ref, rev)
        if rev:
            y = yf_ref[r, :] + y + dsk_ref[...] * xs
            z = z_ref[r, :].astype(F32)
            u = y * _silu(z)
            ms = jnp.mean(u * u, axis=-1, keepdims=True)
            out_ref[r, :] = (u * lax.rsqrt(ms + EPS) * ng_ref[...]).astype(BF16)
        else:
            out_ref[r, :] = y


def _ssd(proj, dt_raw, z_col, fwd, params, bsz, seq, rev):
    t = proj.shape[0]
    ck = SSD_CHUNK * SSD_SUB
    nc = seq // ck
    d_xbc = params["cw"].shape[1]
    d_ssd = SSD_HEADS * SSD_HEADDIM
    d_bc = d_xbc - d_ssd
    per = ck // HALO
    last_halo = t // HALO - 1

    def ce_of(c):
        return nc - 1 - c if rev else c

    main = lambda b, c: (b * nc + ce_of(c), 0)
    prev = lambda b, c: (jnp.maximum((b * nc + ce_of(c)) * per - 1, 0), 0)
    nxt = lambda b, c: (jnp.minimum((b * nc + ce_of(c)) * per + per, last_halo), 0)
    const2 = lambda b, c: (0, 0)
    d = "b" if rev else "f"
    dir_specs = [pl.BlockSpec((1, LANES), const2), pl.BlockSpec((1, LANES), const2),
                 pl.BlockSpec((LANES, d_ssd), const2)]
    dir_args = [params["dtb_" + d], params["a_" + d], params["e2_" + d]]
    if rev:
        yf, xs, bc = fwd
        in_specs = [pl.BlockSpec((ck, d_ssd), main), pl.BlockSpec((ck, d_bc), main),
                    pl.BlockSpec((ck, LANES), main),
                    pl.BlockSpec((ck, d_ssd), lambda b, c: (b * nc + ce_of(c), z_col)),
                    pl.BlockSpec((ck, d_ssd), main)] + dir_specs + [
                        pl.BlockSpec((1, d_ssd), const2), pl.BlockSpec((1, d_ssd), const2)]
        args = [xs, bc, dt_raw, proj, yf] + dir_args + [params["dsk"], params["ng"]]
        out_specs = pl.BlockSpec((ck, d_ssd), main)
        out_shape = jax.ShapeDtypeStruct((t, d_ssd), BF16)
    else:
        in_specs = [pl.BlockSpec((ck, d_xbc), main), pl.BlockSpec((HALO, d_xbc), prev),
                    pl.BlockSpec((HALO, d_xbc), nxt), pl.BlockSpec((ck, LANES), main),
                    pl.BlockSpec((8, d_xbc), const2), pl.BlockSpec((1, d_xbc), const2)] + dir_specs
        args = [proj, proj, proj, dt_raw, params["cw"], params["cb"]] + dir_args
        out_specs = [pl.BlockSpec((ck, d_ssd), main), pl.BlockSpec((ck, d_ssd), main),
                     pl.BlockSpec((ck, d_bc), main)]
        out_shape = [jax.ShapeDtypeStruct((t, d_ssd), F32), jax.ShapeDtypeStruct((t, d_ssd), F32),
                     jax.ShapeDtypeStruct((t, d_bc), BF16)]
    return pl.pallas_call(
        functools.partial(_ssd_kernel, rev=rev),
        grid=(bsz, nc),
        in_specs=in_specs,
        out_specs=out_specs,
        out_shape=out_shape,
        scratch_shapes=[pltpu.VMEM((SSD_GROUPS, SSD_STATE, d_ssd // SSD_GROUPS), F32)],
        compiler_params=_cparams(("arbitrary", "arbitrary")),
        name="ssd_bwd" if rev else "ssd_fwd",
    )(*args)


def _na_plan(rows):
    nblk = rows // NA_QROWS
    variants, var_of_blk, ks_of_blk = [], [], []
    for i in range(nblk):
        ks = int(np.clip(i * NA_QROWS - NA_KH // 2, 0, rows - NA_KROWS))
        drow = np.zeros((NA_QROWS, NA_KROWS), np.int64)
        ok = np.zeros((NA_QROWS, NA_KROWS), bool)
        for dr in range(NA_QROWS):
            r = i * NA_QROWS + dr
            rs = int(np.clip(r - NA_KH // 2, 0, rows - NA_KH))
            for j in range(NA_KH):
                w = rs + j - ks
                drow[dr, w] = rs + j - r + NA_KH - 1
                ok[dr, w] = True
        key = (drow.tobytes(), ok.tobytes())
        for vi, (k2, _, _) in enumerate(variants):
            if k2 == key:
                var_of_blk.append(vi)
                break
        else:
            var_of_blk.append(len(variants))
            variants.append((key, drow, ok))
        ks_of_blk.append(ks)
    return [(d, o) for _, d, o in variants], var_of_blk, ks_of_blk


def _na_blk_start():
    ncb = GRID_W // NA_QB
    return [int(v) for v in np.clip(np.arange(ncb) * NA_QB - NA_KW // 2, 0, GRID_W - NA_KB)]


def _na_bias(rpb, rows):
    variants, var_of_blk, ks_of_blk = _na_plan(rows)
    nh = rpb.shape[0]
    cols = np.arange(GRID_W)
    win_start = np.clip(cols - NA_KW // 2, 0, GRID_W - NA_KW)
    in_win = (cols[None, :] >= win_start[:, None]) & (cols[None, :] < win_start[:, None] + NA_KW)
    dcol = np.clip(cols[None, :] - cols[:, None] + NA_KW - 1, 0, 2 * NA_KW - 2)
    onehot = (dcol[:, :, None] == np.arange(2 * NA_KW - 1)).astype(np.float32)
    band = jnp.einsum("hab,qkb->haqk", rpb.astype(F32), onehot, precision=lax.Precision.HIGHEST)
    band = jnp.where(in_win[None, None], band, -jnp.inf)
    band = band.reshape((nh // 2, 2) + band.shape[1:])
    neg = jnp.full((nh // 2, 2, NA_QB, NA_KB), -jnp.inf, F32)
    mq, nk = NA_QROWS * NA_QB, NA_KROWS * NA_KB
    tabs = []
    for drow, ok in variants:
        per_cb = []
        for m, bs in enumerate(_na_blk_start()):
            qrows = [jnp.concatenate(
                [band[:, :, drow[dr, w], m * NA_QB:(m + 1) * NA_QB, bs:bs + NA_KB] if ok[dr, w]
                 else neg for w in range(NA_KROWS)], axis=3) for dr in range(NA_QROWS)]
            per_cb.append(jnp.concatenate(qrows, axis=2).reshape(nh // 2, 2 * mq, nk))
        tabs.append(jnp.stack(per_cb, axis=1))
    return jnp.stack(tabs), var_of_blk, ks_of_blk


def _na_kernel(var_ref, ks_ref, q_ref, k_ref, v_ref, bias_ref, o_ref, kf_ref, vf_ref):
    mq = NA_QROWS * NA_QB
    nblk = q_ref.shape[0] // (NA_QROWS * GRID_W)
    lo_half = lax.broadcasted_iota(I32, (mq, LANES), 1) < NA_HEADDIM
    kf_ref[...] = k_ref[...].astype(F32)
    vf_ref[...] = v_ref[...].astype(F32)

    def blk(i, carry):
        var = var_ref[i]
        ks = ks_ref[i]
        starts = _na_blk_start()
        q_at = [[pl.multiple_of((i * NA_QROWS + dr) * GRID_W + m * NA_QB, NA_QB)
                 for dr in range(NA_QROWS)] for m in range(len(starts))]
        k_at = [[pl.multiple_of((ks + w) * GRID_W + bs, 8) for w in range(NA_KROWS)]
                for bs in starts]
        scores = []
        for m in range(len(starts)):
            qb = jnp.concatenate([q_ref[pl.ds(o, NA_QB), :] for o in q_at[m]], axis=0)
            qb = qb * (NA_HEADDIM ** -0.5)
            zero = jnp.zeros_like(qb)
            q2 = jnp.concatenate([jnp.where(lo_half, qb, zero), jnp.where(lo_half, zero, qb)], axis=0)
            kb = jnp.concatenate([kf_ref[pl.ds(o, NA_KB), :] for o in k_at[m]], axis=0).astype(BF16)
            s = lax.dot_general(q2, kb, (((1,), (1,)), ((), ())), preferred_element_type=F32)
            scores.append(s + bias_ref[var, 0, m])
        outs = []
        for m, s in enumerate(scores):
            p = jnp.exp(s - jnp.max(s, axis=-1, keepdims=True))
            l = jnp.sum(p, axis=-1, keepdims=True)
            vb = jnp.concatenate([vf_ref[pl.ds(o, NA_KB), :] for o in k_at[m]], axis=0).astype(BF16)
            outs.append(jnp.dot(p.astype(BF16), vb, preferred_element_type=F32) / l)
        for m, o2 in enumerate(outs):
            ob = jnp.where(lo_half, o2[:mq], o2[mq:]).astype(BF16)
            for dr, o in enumerate(q_at[m]):
                o_ref[pl.ds(o, NA_QB), :] = ob[dr * NA_QB:(dr + 1) * NA_QB]
        return carry

    lax.fori_loop(0, nblk, blk, 0, unroll=8)


def _na(proj, bias, var_of_blk, ks_of_blk, q_col, bsz, seq):
    t = proj.shape[0]
    npair = NA_HEADS // 2
    grid_spec = pltpu.PrefetchScalarGridSpec(
        num_scalar_prefetch=2,
        grid=(bsz, npair),
        in_specs=[pl.BlockSpec((seq, LANES), lambda b, p, *_: (b, q_col + p)),
                  pl.BlockSpec((seq, LANES), lambda b, p, *_: (b, q_col + npair + p)),
                  pl.BlockSpec((seq, LANES), lambda b, p, *_: (b, q_col + 2 * npair + p)),
                  pl.BlockSpec((bias.shape[0], 1) + bias.shape[2:], lambda b, p, *_: (0, p, 0, 0, 0))],
        out_specs=pl.BlockSpec((seq, LANES), lambda b, p, *_: (b, p)),
        scratch_shapes=[pltpu.VMEM((seq, LANES), F32), pltpu.VMEM((seq, LANES), F32)],
    )
    return pl.pallas_call(
        _na_kernel,
        grid_spec=grid_spec,
        out_shape=jax.ShapeDtypeStruct((t, NA_HEADS * NA_HEADDIM), BF16),
        compiler_params=_cparams(("arbitrary", "arbitrary")),
        name="na",
    )(jnp.asarray(var_of_blk, I32), jnp.asarray(ks_of_blk, I32), proj, proj, proj, bias)


def _mix_kernel(x_ref, yn_ref, at_ref, gs_ref, gn_ref, g1_ref, sh_ref, sc_ref, ng_ref,
                ws_ref, wn_ref, wo_ref, wq_ref, keys_ref, h_ref, s_ref):
    y1 = jnp.dot(yn_ref[...], ws_ref[...], preferred_element_type=F32)
    y2 = jnp.dot(at_ref[...], wn_ref[...], preferred_element_type=F32)
    mixed = _sigmoid(gs_ref[...].astype(F32)) * y1 + _sigmoid(gn_ref[...].astype(F32)) * y2
    h = x_ref[...] + g1_ref[0] * jnp.dot(mixed.astype(BF16), wo_ref[...], preferred_element_type=F32)
    h_ref[...] = h
    n2 = _rms_mod(h, ng_ref[...], sh_ref[0], sc_ref[0]).astype(BF16)
    q = jnp.dot(n2, wq_ref[...], preferred_element_type=F32).astype(BF16)
    sub = 8
    for hz in range(2 * PEER_HEADS):
        qs = q[:, hz * PEER_DHALF:(hz + 1) * PEER_DHALF]
        sc = lax.dot_general(keys_ref[hz % 2], qs, (((1,), (1,)), ((), ())),
                             preferred_element_type=F32)
        for kq in range(PEER_NKEYS // sub):
            for tb in range(sc.shape[1] // LANES):
                s_ref[hz, kq, tb * sub:(tb + 1) * sub, :] = sc[kq * sub:(kq + 1) * sub,
                                                               tb * LANES:(tb + 1) * LANES]


def _mix(x2, ynorm, attn, proj, gs_col, g1, sh2, sc2, ng2, ws, wn, wo, wq, keys, seq):
    t, d = x2.shape
    tm = min(512, seq)
    bidx = lambda i: ((i * tm) // seq, 0, 0)
    row = lambda i: (i, 0)
    const2 = lambda i: (0, 0)
    return pl.pallas_call(
        _mix_kernel,
        grid=(t // tm,),
        in_specs=[pl.BlockSpec((tm, d), row), pl.BlockSpec((tm, d), row), pl.BlockSpec((tm, d), row),
                  pl.BlockSpec((tm, d), lambda i: (i, gs_col)),
                  pl.BlockSpec((tm, d), lambda i: (i, gs_col + 1)),
                  pl.BlockSpec((1, 1, d), bidx), pl.BlockSpec((1, 1, d), bidx),
                  pl.BlockSpec((1, 1, d), bidx), pl.BlockSpec((1, d), const2),
                  pl.BlockSpec((d, d), const2, pipeline_mode=pl.Buffered(1)),
                  pl.BlockSpec((d, d), const2, pipeline_mode=pl.Buffered(1)),
                  pl.BlockSpec((d, d), const2, pipeline_mode=pl.Buffered(1)),
                  pl.BlockSpec(wq.shape, const2, pipeline_mode=pl.Buffered(1)),
                  pl.BlockSpec(keys.shape, lambda i: (0, 0, 0), pipeline_mode=pl.Buffered(1))],
        out_specs=[pl.BlockSpec((tm, d), row),
                   pl.BlockSpec((2 * PEER_HEADS, PEER_NKEYS // 8, tm // LANES * 8, LANES),
                                lambda i: (0, 0, i, 0))],
        out_shape=[jax.ShapeDtypeStruct((t, d), F32),
                   jax.ShapeDtypeStruct((2 * PEER_HEADS, PEER_NKEYS // 8, t // LANES * 8, LANES), F32)],
        compiler_params=_cparams(("arbitrary",)),
        name="mix",
    )(x2, ynorm, attn, proj, proj, g1, sh2, sc2, ng2, ws, wn, wo, wq, keys)


def _hyperbola():
    return [(i, k) for i in range(PEER_TOPK) for k in range(PEER_TOPK)
            if (i + 1) * (k + 1) <= PEER_TOPK]


def _sort_network(n):
    pairs = []
    p = 1
    while p < n:
        k = p
        while k >= 1:
            for j in range(k % p, n - k, 2 * k):
                for i in range(min(k, n - j - k)):
                    if (i + j) // (2 * p) == (i + j + k) // (2 * p):
                        pairs.append((i + j, i + j + k))
            k //= 2
        p *= 2
    return pairs


def _precedes(va, pa, vb, pb):
    return (va > vb) | ((va == vb) & (pa < pb))


def _compare_exchange(v, p, i, j):
    c = _precedes(v[i], p[i], v[j], p[j])
    v[i], v[j] = jnp.where(c, v[i], v[j]), jnp.where(c, v[j], v[i])
    p[i], p[j] = jnp.where(c, p[i], p[j]), jnp.where(c, p[j], p[i])


def _top16(get, n, sv_ref, sp_ref):
    k = PEER_TOPK
    ng = n // k
    net = _sort_network(k)
    for g in range(ng):
        items = [get(g * k + j) for j in range(k)]
        v = [it[0] for it in items]
        p = [it[1] for it in items]
        for i, j in net:
            _compare_exchange(v, p, i, j)
        if ng == 1:
            return v, p
        for j in range(k):
            sv_ref[g, j] = v[j]
            sp_ref[g, j] = p[j]
    step = 1
    while True:
        for g in range(0, ng, 2 * step):
            v, p = [], []
            for j in range(k):
                xv, xp = sv_ref[g, j], sp_ref[g, j]
                yv, yp = sv_ref[g + step, k - 1 - j], sp_ref[g + step, k - 1 - j]
                c = _precedes(xv, xp, yv, yp)
                v.append(jnp.where(c, xv, yv))
                p.append(jnp.where(c, xp, yp))
            stride = k // 2
            while stride >= 1:
                for i in range(k):
                    if i & stride == 0:
                        _compare_exchange(v, p, i, i + stride)
                stride //= 2
            if 2 * step >= ng:
                return v, p
            for j in range(k):
                sv_ref[g, j] = v[j]
                sp_ref[g, j] = p[j]
        step *= 2


def _select_kernel(s_ref, eidx_ref, gate_ref, sv_ref, sp_ref):
    nk = PEER_NKEYS
    k = PEER_TOPK
    sub = 8
    shape = (sub, LANES)
    ebits = (nk * nk - 1).bit_length()

    def key(z, j):
        return s_ref[z, j // sub, pl.ds(j % sub, sub, stride=sub), :], jnp.full(shape, j, I32)

    tops = []
    for z in range(2):
        tops.append(_top16(functools.partial(key, z), nk, sv_ref, sp_ref))
    (tv0, ti0), (tv1, ti1) = tops

    pairs = _hyperbola()
    npad = -len(pairs) % k

    def cand(j):
        if j >= len(pairs):
            return jnp.full(shape, -jnp.inf, F32), jnp.full(shape, (k * k) << ebits, I32)
        i, kk = pairs[j]
        return tv0[i] + tv1[kk], ((i * k + kk) << ebits) + ti0[i] * nk + ti1[kk]

    top, ids = _top16(cand, len(pairs) + npad, sv_ref, sp_ref)
    ex = [jnp.exp(t - top[0]) for t in top]
    zsum = ex[0]
    for r in range(1, k):
        zsum = zsum + ex[r]
    for r in range(k):
        eidx_ref[0, r] = ids[r] & ((1 << ebits) - 1)
        gate_ref[0, r] = ex[r] / zsum


def _select(scores):
    r = 8
    nk = PEER_NKEYS
    tb = scores.shape[2] // r
    assert tb % r == 0
    oshape = (PEER_HEADS, PEER_TOPK, tb, LANES)
    return pl.pallas_call(
        _select_kernel,
        grid=(tb // r, PEER_HEADS),
        in_specs=[pl.BlockSpec((2, nk // r, r * r, LANES), lambda i, h: (h, 0, i, 0))],
        out_specs=[pl.BlockSpec((1, PEER_TOPK, r, LANES), lambda i, h: (h, 0, i, 0)),
                   pl.BlockSpec((1, PEER_TOPK, r, LANES), lambda i, h: (h, 0, i, 0))],
        out_shape=[jax.ShapeDtypeStruct(oshape, I32), jax.ShapeDtypeStruct(oshape, F32)],
        scratch_shapes=[pltpu.VMEM((nk // PEER_TOPK, PEER_TOPK, r, LANES), F32),
                        pltpu.VMEM((nk // PEER_TOPK, PEER_TOPK, r, LANES), I32)],
        compiler_params=_cparams(("arbitrary", "arbitrary")),
        name="select",
    )(scores)


def _peer_kernel(h_ref, sh_ref, sc_ref, g2_ref, ng_ref, fg_ref, eidx_ref, gate_ref, uv_ref,
                 out_ref, n2_ref, act_ref, wv_ref, w3_ref, acc_ref, *, ns):
    nk = PEER_NKEYS
    half = nk // 2
    hi_mask = jnp.uint32(0xFFFF0000)
    tm = h_ref.shape[0]
    eb = uv_ref.shape[0]
    nblk = eb // nk
    nblk2 = nblk // 2
    s = pl.program_id(1)

    @pl.when(s == 0)
    def _():
        n2_ref[...] = _rms_mod(h_ref[...], ng_ref[...], sh_ref[0], sc_ref[0]).astype(BF16)
        act_ref[...] = jnp.zeros_like(act_ref)

    @pl.when(s < ns)
    def _():
        e = eidx_ref[...]
        ai = jnp.right_shift(e, nk.bit_length() - 1)
        bi = jnp.bitwise_and(e, nk - 1)
        n2 = n2_ref[...]
        for a in range(0, nblk, 2):
            sc = lax.dot_general(n2, uv_ref[a * nk:(a + 2) * nk, :], (((1,), (1,)), ((), ())),
                                 preferred_element_type=F32)
            act = act_ref[...]
            for j in range(2):
                got = jnp.take_along_axis(sc[:, j * nk:(j + 1) * nk], bi, axis=1)
                act = jnp.where(ai == s * nblk + a + j, got, act)
            act_ref[...] = act

    @pl.when(s == ns - 1)
    def _():
        act = act_ref[...]
        gelu = 0.5 * act * (1.0 + lax.erf(act * (2.0 ** -0.5)))
        wv_ref[...] = gate_ref[...] * gelu
        r = lax.broadcasted_iota(I32, (nk, LANES), 0)
        sub = r.astype(F32).astype(BF16)
        rowkey = jnp.where(r < half, 2 * r, 2 * (r - half) + 1).astype(F32).astype(BF16)
        zero = jnp.zeros((nk, LANES), BF16)
        one = jnp.ones((nk, LANES), BF16)

        def tok(t, carry):
            e = eidx_ref[pl.ds(t, 1), :]
            w = wv_ref[pl.ds(t, 1), :].astype(BF16)
            ar = jnp.right_shift(e, nk.bit_length() - 1).astype(F32).astype(BF16)
            br = jnp.bitwise_and(e, nk - 1).astype(F32).astype(BF16)
            pt = jnp.where(rowkey == ar, w, zero)
            qt = jnp.where(sub == br, one, zero)
            wt = lax.dot_general(pt, qt, (((1,), (1,)), ((), ())), preferred_element_type=F32)
            bits = lax.bitcast_convert_type(wt, jnp.uint32)
            word = jnp.right_shift(bits[:half], 16) | (bits[half:] & hi_mask)
            w3_ref[pl.ds(pl.multiple_of(t * W3_PITCH, 8), half), :] = word
            return carry
        lax.fori_loop(0, tm, tok, 0, unroll=64)

    @pl.when(s >= ns)
    def _():
        base = (s - ns) * nblk2
        words = [w3_ref[pl.ds(base + a, tm, stride=W3_PITCH), :] for a in range(nblk2)]
        lo = [lax.bitcast_convert_type(jnp.left_shift(w, 16), F32).astype(BF16) for w in words]
        hi = [lax.bitcast_convert_type(w & hi_mask, F32).astype(BF16) for w in words]
        cols = [c for pair in zip(lo, hi) for c in pair]
        part = jnp.dot(jnp.concatenate(cols, axis=1), uv_ref[...], preferred_element_type=F32)

        @pl.when(s == ns)
        def _():
            acc_ref[...] = part

        @pl.when(s > ns)
        def _():
            acc_ref[...] += part

    @pl.when(s == 2 * ns - 1)
    def _():
        hh = h_ref[...] + g2_ref[0] * acc_ref[...]
        ms = jnp.mean(hh * hh, axis=-1, keepdims=True)
        out_ref[...] = hh * lax.rsqrt(ms + EPS) * fg_ref[...]


def _peer(h, sh2, sc2, g2, ng2, fg, eidx_t, gate_t, u, v, seq):
    t, d = h.shape
    ne = u.shape[0]
    tm = min(512, seq)
    eb = 4096
    ns = ne // eb
    nj = eidx_t.shape[1]
    uv = jnp.concatenate([u, v], axis=0).astype(BF16)
    bidx = lambda i, s: ((i * tm) // seq, 0, 0)
    row = lambda i, s: (i, 0)
    const2 = lambda i, s: (0, 0)
    return pl.pallas_call(
        functools.partial(_peer_kernel, ns=ns),
        grid=(t // tm, 2 * ns),
        in_specs=[pl.BlockSpec((tm, d), row),
                  pl.BlockSpec((1, 1, d), bidx), pl.BlockSpec((1, 1, d), bidx),
                  pl.BlockSpec((1, 1, d), bidx),
                  pl.BlockSpec((1, d), const2), pl.BlockSpec((1, d), const2),
                  pl.BlockSpec((tm, nj), row), pl.BlockSpec((tm, nj), row),
                  pl.BlockSpec((eb, d), lambda i, s: (s, 0))],
        out_specs=pl.BlockSpec((tm, d), row),
        out_shape=jax.ShapeDtypeStruct((t, d), F32),
        scratch_shapes=[pltpu.VMEM((tm, d), BF16),
                        pltpu.VMEM((tm, nj), F32),
                        pltpu.VMEM((tm, nj), F32),
                        pltpu.VMEM((tm * W3_PITCH, PEER_NKEYS), jnp.uint32),
                        pltpu.VMEM((tm, d), F32)],
        compiler_params=_cparams(("arbitrary", "arbitrary")),
        name="peer",
    )(h, sh2, sc2, g2, ng2, fg, eidx_t, gate_t, uv)


def _layer(x2, c, bsz, seq, w_ada, b_ada, norm1_g, w_in, conv_w, conv_b, dt_bias_f, dt_bias_b,
           a_log_f, a_log_b, d_skip, ssd_norm_g, w_ssd_br, na_rpb, w_na_br, w_out, norm2_g,
           peer_wq, peer_keys, peer_u, peer_v, out_g):
    t, d = x2.shape
    d_ssd = SSD_HEADS * SSD_HEADDIM
    d_xbc = d_ssd + 2 * SSD_GROUPS * SSD_STATE
    d_na = NA_HEADS * NA_HEADDIM
    assert seq % (SSD_CHUNK * SSD_SUB) == 0 and seq % (GRID_W * NA_QROWS) == 0
    assert seq // GRID_W >= NA_KROWS and d == d_ssd == d_na

    mod = _ada(c, w_ada, b_ada)
    sh1, sc1, g1, sh2, sc2, g2 = [m.reshape(bsz, 1, d) for m in jnp.split(mod, 6, axis=-1)]

    o = np.cumsum([0, d_ssd, d_xbc, SSD_HEADS, SSD_HEADS, 3 * d_na, d, d])
    w_main = jnp.concatenate([w_in[:, o[1]:o[2]], w_in[:, o[0]:o[1]], w_in[:, o[4]:o[7]]],
                             axis=1).astype(BF16)
    w_dt = jnp.pad(w_in[:, o[2]:o[4]], ((0, 0), (0, LANES - 2 * SSD_HEADS))).astype(BF16)
    z_col = d_xbc // d_ssd
    q_col = (d_xbc + d_ssd) // LANES
    gs_col = (d_xbc + d_ssd + 3 * d_na) // d
    proj, dt_raw = _inproj(x2, sh1, sc1, norm1_g.reshape(1, d), w_main, w_dt, seq)

    heads = np.arange(d_ssd) // SSD_HEADDIM

    def e2(off):
        m = np.zeros((LANES, d_ssd), np.float32)
        m[off + heads, np.arange(d_ssd)] = 1.0
        m[off + 64 + heads, np.arange(d_ssd)] = 1.0
        return jnp.asarray(m, BF16)

    def lanes16(vec, off):
        return jnp.zeros((1, LANES), F32).at[0, off:off + SSD_HEADS].set(vec.astype(F32))

    params = {
        "cw": jnp.pad(conv_w.astype(F32), ((0, 8 - CONV_W), (0, 0))),
        "cb": conv_b.astype(F32).reshape(1, d_xbc),
        "dtb_f": lanes16(dt_bias_f, 0), "dtb_b": lanes16(dt_bias_b, SSD_HEADS),
        "a_f": lanes16(-jnp.exp(a_log_f.astype(F32)), 0),
        "a_b": lanes16(-jnp.exp(a_log_b.astype(F32)), SSD_HEADS),
        "e2_f": e2(0), "e2_b": e2(SSD_HEADS),
        "dsk": jnp.repeat(d_skip.astype(F32), SSD_HEADDIM).reshape(1, d_ssd),
        "ng": ssd_norm_g.astype(F32).reshape(1, d_ssd),
    }
    fwd = _ssd(proj, dt_raw, z_col, None, params, bsz, seq, rev=False)
    ynorm = _ssd(proj, dt_raw, z_col, fwd, params, bsz, seq, rev=True)

    bias, var_of_blk, ks_of_blk = _na_bias(na_rpb, seq // GRID_W)
    attn = _na(proj, bias, var_of_blk, ks_of_blk, q_col, bsz, seq)

    h, scores = _mix(x2, ynorm, attn, proj, gs_col, g1, sh2, sc2, norm2_g.reshape(1, d),
                     w_ssd_br.astype(BF16), w_na_br.astype(BF16), w_out.astype(BF16),
                     peer_wq.astype(BF16), peer_keys.astype(BF16), seq)

    eidx, gate = _select(scores)
    nj = PEER_HEADS * PEER_TOPK
    eidx_t = eidx.reshape(nj, t).T
    gate_t = gate.reshape(nj, t).T
    return _peer(h, sh2, sc2, g2, norm2_g.reshape(1, d), out_g, eidx_t, gate_t,
                 peer_u.astype(BF16), peer_v.astype(BF16), seq)


def kernel(x, c, w_ada, b_ada, norm1_g, w_in, conv_w, conv_b, dt_bias_f, dt_bias_b, a_log_f, a_log_b,
           d_skip, ssd_norm_g, w_ssd_br, na_rpb, w_na_br, w_out, norm2_g, peer_wq, peer_keys, peer_u,
           peer_v, final_g):
    bsz, seq, d = x.shape
    depth = w_ada.shape[0]
    assert depth == 1, "the final RMSNorm is fused into the last layer's PEER kernel"
    i = 0
    out = _layer(x.reshape(bsz * seq, d), c, bsz, seq, w_ada[i], b_ada[i], norm1_g[i], w_in[i],
                 conv_w[i], conv_b[i], dt_bias_f[i], dt_bias_b[i], a_log_f[i], a_log_b[i], d_skip[i],
                 ssd_norm_g[i], w_ssd_br[i], na_rpb[i], w_na_br[i], w_out[i], norm2_g[i],
                 peer_wq[i], peer_keys[i], peer_u[i], peer_v[i], final_g.reshape(1, d))
    return out.reshape(bsz, seq, d)
```

```python
import functools

import numpy as np
import jax
import jax.numpy as jnp
from jax import lax
from jax.experimental import pallas as pl
from jax.experimental.pallas import tpu as pltpu

F32 = jnp.float32
BF16 = jnp.bfloat16
I32 = jnp.int32

EPS = 1e-6
GRID_W = 64
SSD_HEADS = 16
SSD_HEADDIM = 64
SSD_GROUPS = 4
SSD_STATE = 128
SSD_CHUNK = 128
SSD_SUB = 2
CONV_W = 5
NA_HEADS = 16
NA_HEADDIM = 64
NA_KH = 8
NA_KW = 16
NA_QROWS = 4
NA_KROWS = 12
NA_QB = 16
NA_KB = NA_QB + NA_KW
PEER_HEADS = 8
PEER_NKEYS = 128
PEER_TOPK = 16
PEER_DHALF = 128
W3_PITCH = PEER_NKEYS // 2 + 8

LANES = 128
HALO = 16
VMEM_LIMIT = 56 * 1024 * 1024


def _cparams(sem):
    return pltpu.CompilerParams(dimension_semantics=sem, vmem_limit_bytes=VMEM_LIMIT)


def _sigmoid(x):
    return 0.5 * jnp.tanh(0.5 * x) + 0.5


def _silu(x):
    h = 0.5 * x
    return h * (jnp.tanh(h) + 1.0)


def _rms_mod(x, g, shift, scale):
    ms = jnp.mean(x * x, axis=-1, keepdims=True)
    y = x * lax.rsqrt(ms + EPS) * g
    return y * (1.0 + scale) + shift


def _ada_kernel(c_ref, w_ref, b_ref, o_ref):
    c = c_ref[...]
    sc = _silu(c)
    o_ref[...] = jnp.dot(sc, w_ref[...], preferred_element_type=F32,
                         precision=lax.Precision.HIGHEST) + b_ref[...]


def _ada(c, w, b):
    bsz, d = c.shape
    n = w.shape[1]
    tn = 1024
    return pl.pallas_call(
        _ada_kernel,
        grid=(n // tn,),
        in_specs=[pl.BlockSpec((bsz, d), lambda j: (0, 0)),
                  pl.BlockSpec((d, tn), lambda j: (0, j)),
                  pl.BlockSpec((1, tn), lambda j: (0, j))],
        out_specs=pl.BlockSpec((bsz, tn), lambda j: (0, j)),
        out_shape=jax.ShapeDtypeStruct((bsz, n), F32),
        compiler_params=_cparams(("arbitrary",)),
        name="ada",
    )(c, w, b.reshape(1, n))


def _inproj_kernel(x_ref, sh_ref, sc_ref, g_ref, w_ref, wdt_ref, proj_ref, dt_ref):
    n1 = _rms_mod(x_ref[...], g_ref[...], sh_ref[0], sc_ref[0]).astype(BF16)
    dt_ref[...] = jnp.dot(n1, wdt_ref[...], preferred_element_type=F32)
    tn = 1024
    for j in range(w_ref.shape[1] // tn):
        proj_ref[:, j * tn:(j + 1) * tn] = jnp.dot(
            n1, w_ref[:, j * tn:(j + 1) * tn], preferred_element_type=F32).astype(BF16)


def _inproj(x2, sh, sc, g, w, wdt, seq):
    t, d = x2.shape
    n = w.shape[1]
    tm = min(512, seq)
    bidx = lambda i: ((i * tm) // seq, 0, 0)
    const2 = lambda i: (0, 0)
    return pl.pallas_call(
        _inproj_kernel,
        grid=(t // tm,),
        in_specs=[pl.BlockSpec((tm, d), lambda i: (i, 0)),
                  pl.BlockSpec((1, 1, d), bidx),
                  pl.BlockSpec((1, 1, d), bidx),
                  pl.BlockSpec((1, d), const2),
                  pl.BlockSpec((d, n), const2, pipeline_mode=pl.Buffered(1)),
                  pl.BlockSpec((d, LANES), const2, pipeline_mode=pl.Buffered(1))],
        out_specs=[pl.BlockSpec((tm, n), lambda i: (i, 0)),
                   pl.BlockSpec((tm, LANES), lambda i: (i, 0))],
        out_shape=[jax.ShapeDtypeStruct((t, n), BF16),
                   jax.ShapeDtypeStruct((t, LANES), F32)],
        compiler_params=_cparams(("arbitrary",)),
        name="inproj",
    )(x2, sh, sc, g, w, wdt)


def _expand_heads(v, e2_ref):
    hi = v.astype(BF16).astype(F32)
    comb = (hi + pltpu.roll(v - hi, 64, 1)).astype(BF16)
    return jnp.dot(comb, e2_ref[...], preferred_element_type=F32)


def _ssd_conv(main, ext, cw_ref, cb_ref):
    ck = main.shape[0]
    d_ssd = SSD_HEADS * SSD_HEADDIM
    erow = lax.broadcasted_iota(I32, (ck, ck + 2 * HALO), 0)
    ecol = lax.broadcasted_iota(I32, (ck, ck + 2 * HALO), 1)
    mid = CONV_W // 2
    acc = cb_ref[...] + main.astype(F32) * cw_ref[mid:mid + 1, :]
    for w in range(CONV_W):
        if w != mid:
            shift = jnp.where(ecol == erow + (HALO + w - mid), 1.0, 0.0).astype(BF16)
            acc = acc + jnp.dot(shift, ext, preferred_element_type=F32) * cw_ref[w:w + 1, :]
    xbc = _silu(acc)
    return xbc[:, :d_ssd], xbc[:, d_ssd:].astype(BF16)


def _ssd_scan(xs, bc, dt_raw, dtb_ref, a_ref, e2_ref, st_ref, rev):
    ck = SSD_CHUNK
    d_ssd = SSD_HEADS * SSD_HEADDIM
    d_bc = SSD_GROUPS * SSD_STATE
    gw = d_ssd // SSD_GROUPS
    lane = lax.broadcasted_iota(I32, (ck, LANES), 1)
    row = lax.broadcasted_iota(I32, (ck, LANES), 0)
    off = SSD_HEADS if rev else 0
    valid = (lane >= off) & (lane < off + SSD_HEADS)
    dtr = dt_raw + dtb_ref[...]
    dt = jnp.where(valid, jnp.maximum(dtr, 0.0) + jnp.log1p(jnp.exp(-jnp.abs(dtr))), 0.0)
    da = dt * a_ref[...]
    tri = (lane >= row) if rev else (lane <= row)
    trib = jnp.where(tri, 1.0, 0.0).astype(BF16)
    hi = da.astype(BF16)
    r1 = da - hi.astype(F32)
    mid = r1.astype(BF16)
    lo = (r1 - mid.astype(F32)).astype(BF16)
    cs = (jnp.dot(trib, hi, preferred_element_type=F32)
          + jnp.dot(trib, mid, preferred_element_type=F32)
          + jnp.dot(trib, lo, preferred_element_type=F32))
    cst = cs.T
    tot_row = 0 if rev else ck - 1
    tot = cs[tot_row:tot_row + 1, :]
    dec = jnp.where(valid, jnp.exp(tot - cs), 0.0)
    ecs = jnp.where(valid, jnp.exp(cs), 0.0)
    ecs_e = _expand_heads(ecs, e2_ref)
    xdt = xs * _expand_heads(dt, e2_ref)
    xdec = (xs * _expand_heads(dt * dec, e2_ref)).astype(BF16)

    lo_half = lax.broadcasted_iota(I32, (ck, LANES), 1) < SSD_HEADDIM
    hpg = SSD_HEADS // SSD_GROUPS
    ys = []
    for g in range(SSD_GROUPS):
        bg = bc[:, g * SSD_STATE:(g + 1) * SSD_STATE]
        cg = bc[:, d_bc + g * SSD_STATE:d_bc + (g + 1) * SSD_STATE]
        cbg = lax.dot_general(cg, bg, (((1,), (1,)), ((), ())), preferred_element_type=F32)
        yg = []
        for q in range(hpg // 2):
            ls = []
            for e in range(2):
                col = off + g * hpg + q * 2 + e
                dlt = cs[:, col:col + 1] - cst[col:col + 1, :]
                seg = jnp.exp(jnp.where(tri, dlt, -jnp.inf))
                ls.append((cbg * seg).astype(BF16))
            blk = g * (hpg // 2) + q
            xq = xdt[:, blk * LANES:(blk + 1) * LANES]
            rhs = jnp.concatenate([jnp.where(lo_half, xq, 0.0), jnp.where(lo_half, 0.0, xq)],
                                  axis=0).astype(BF16)
            yg.append(jnp.dot(jnp.concatenate(ls, axis=1), rhs, preferred_element_type=F32))
        y_diag = jnp.concatenate(yg, axis=1)
        st = st_ref[g]
        e_g = ecs_e[:, g * gw:(g + 1) * gw]
        y_off = jnp.dot(cg, st.astype(BF16), preferred_element_type=F32) * e_g
        ys.append(y_diag + y_off)
        upd = jnp.dot(bg.astype(F32).T.astype(BF16), xdec[:, g * gw:(g + 1) * gw],
                      preferred_element_type=F32)
        st_ref[g] = st * e_g[tot_row:tot_row + 1, :] + upd
    return jnp.concatenate(ys, axis=1)


def _ssd_kernel(*refs, rev):
    if rev:
        (xs_ref, bc_ref, dt_ref, z_ref, yf_ref, dtb_ref, a_ref, e2_ref, dsk_ref, ng_ref,
         out_ref, st_ref) = refs
    else:
        (xm_ref, xp_ref, xn_ref, dt_ref, cw_ref, cb_ref, dtb_ref, a_ref, e2_ref,
         out_ref, xs_out_ref, bc_out_ref, st_ref) = refs
    ck = SSD_CHUNK
    c = pl.program_id(1)
    nc = pl.num_programs(1)
    ce = nc - 1 - c if rev else c

    @pl.when(c == 0)
    def _():
        st_ref[...] = jnp.zeros_like(st_ref)

    rows = [slice(j * ck, (j + 1) * ck) for j in range(SSD_SUB)]
    if rev:
        conv = [(xs_ref[r, :], bc_ref[r, :]) for r in rows]
    else:
        main = xm_ref[...]
        zero_halo = jnp.zeros((HALO, main.shape[1]), main.dtype)
        ext = jnp.concatenate([jnp.where(ce == 0, zero_halo, xp_ref[...]), main,
                               jnp.where(ce == nc - 1, zero_halo, xn_ref[...])], axis=0)
        conv = [_ssd_conv(main[r], ext[r.start:r.stop + 2 * HALO], cw_ref, cb_ref) for r in rows]
        for r, (xs, bc) in zip(rows, conv):
            xs_out_ref[r, :] = xs
            bc_out_ref[r, :] = bc

    for j in (reversed(range(SSD_SUB)) if rev else range(SSD_SUB)):
        r = rows[j]
        xs, bc = conv[j]
        y = _ssd_scan(xs, bc, dt_ref[r, :], dtb_ref, a_ref, e2_ref, st_ref, rev)
        if rev:
            y = yf_ref[r, :] + y + dsk_ref[...] * xs
            z = z_ref[r, :].astype(F32)
            u = y * _silu(z)
            ms = jnp.mean(u * u, axis=-1, keepdims=True)
            out_ref[r, :] = (u * lax.rsqrt(ms + EPS) * ng_ref[...]).astype(BF16)
        else:
            out_ref[r, :] = y


def _ssd(proj, dt_raw, z_col, fwd, params, bsz, seq, rev):
    t = proj.shape[0]
    ck = SSD_CHUNK * SSD_SUB
    nc = seq // ck
    d_xbc = params["cw"].shape[1]
    d_ssd = SSD_HEADS * SSD_HEADDIM
    d_bc = d_xbc - d_ssd
    per = ck // HALO
    last_halo = t // HALO - 1

    def ce_of(c):
        return nc - 1 - c if rev else c

    main = lambda b, c: (b * nc + ce_of(c), 0)
    prev = lambda b, c: (jnp.maximum((b * nc + ce_of(c)) * per - 1, 0), 0)
    nxt = lambda b, c: (jnp.minimum((b * nc + ce_of(c)) * per + per, last_halo), 0)
    const2 = lambda b, c: (0, 0)
    d = "b" if rev else "f"
    dir_specs = [pl.BlockSpec((1, LANES), const2), pl.BlockSpec((1, LANES), const2),
                 pl.BlockSpec((LANES, d_ssd), const2)]
    dir_args = [params["dtb_" + d], params["a_" + d], params["e2_" + d]]
    if rev:
        yf, xs, bc = fwd
        in_specs = [pl.BlockSpec((ck, d_ssd), main), pl.BlockSpec((ck, d_bc), main),
                    pl.BlockSpec((ck, LANES), main),
                    pl.BlockSpec((ck, d_ssd), lambda b, c: (b * nc + ce_of(c), z_col)),
                    pl.BlockSpec((ck, d_ssd), main)] + dir_specs + [
                        pl.BlockSpec((1, d_ssd), const2), pl.BlockSpec((1, d_ssd), const2)]
        args = [xs, bc, dt_raw, proj, yf] + dir_args + [params["dsk"], params["ng"]]
        out_specs = pl.BlockSpec((ck, d_ssd), main)
        out_shape = jax.ShapeDtypeStruct((t, d_ssd), BF16)
    else:
        in_specs = [pl.BlockSpec((ck, d_xbc), main), pl.BlockSpec((HALO, d_xbc), prev),
                    pl.BlockSpec((HALO, d_xbc), nxt), pl.BlockSpec((ck, LANES), main),
                    pl.BlockSpec((8, d_xbc), const2), pl.BlockSpec((1, d_xbc), const2)] + dir_specs
        args = [proj, proj, proj, dt_raw, params["cw"], params["cb"]] + dir_args
        out_specs = [pl.BlockSpec((ck, d_ssd), main), pl.BlockSpec((ck, d_ssd), main),
                     pl.BlockSpec((ck, d_bc), main)]
        out_shape = [jax.ShapeDtypeStruct((t, d_ssd), F32), jax.ShapeDtypeStruct((t, d_ssd), F32),
                     jax.ShapeDtypeStruct((t, d_bc), BF16)]
    return pl.pallas_call(
        functools.partial(_ssd_kernel, rev=rev),
        grid=(bsz, nc),
        in_specs=in_specs,
        out_specs=out_specs,
        out_shape=out_shape,
        scratch_shapes=[pltpu.VMEM((SSD_GROUPS, SSD_STATE, d_ssd // SSD_GROUPS), F32)],
        compiler_params=_cparams(("arbitrary", "arbitrary")),
        name="ssd_bwd" if rev else "ssd_fwd",
    )(*args)


def _na_plan(rows):
    nblk = rows // NA_QROWS
    variants, var_of_blk, ks_of_blk = [], [], []
    for i in range(nblk):
        ks = int(np.clip(i * NA_QROWS - NA_KH // 2, 0, rows - NA_KROWS))
        drow = np.zeros((NA_QROWS, NA_KROWS), np.int64)
        ok = np.zeros((NA_QROWS, NA_KROWS), bool)
        for dr in range(NA_QROWS):
            r = i * NA_QROWS + dr
            rs = int(np.clip(r - NA_KH // 2, 0, rows - NA_KH))
            for j in range(NA_KH):
                w = rs + j - ks
                drow[dr, w] = rs + j - r + NA_KH - 1
                ok[dr, w] = True
        key = (drow.tobytes(), ok.tobytes())
        for vi, (k2, _, _) in enumerate(variants):
            if k2 == key:
                var_of_blk.append(vi)
                break
        else:
            var_of_blk.append(len(variants))
            variants.append((key, drow, ok))
        ks_of_blk.append(ks)
    return [(d, o) for _, d, o in variants], var_of_blk, ks_of_blk


def _na_blk_start():
    ncb = GRID_W // NA_QB
    return [int(v) for v in np.clip(np.arange(ncb) * NA_QB - NA_KW // 2, 0, GRID_W - NA_KB)]


def _na_bias(rpb, rows):
    variants, var_of_blk, ks_of_blk = _na_plan(rows)
    nh = rpb.shape[0]
    cols = np.arange(GRID_W)
    win_start = np.clip(cols - NA_KW // 2, 0, GRID_W - NA_KW)
    in_win = (cols[None, :] >= win_start[:, None]) & (cols[None, :] < win_start[:, None] + NA_KW)
    dcol = np.clip(cols[None, :] - cols[:, None] + NA_KW - 1, 0, 2 * NA_KW - 2)
    onehot = (dcol[:, :, None] == np.arange(2 * NA_KW - 1)).astype(np.float32)
    band = jnp.einsum("hab,qkb->haqk", rpb.astype(F32), onehot, precision=lax.Precision.HIGHEST)
    band = jnp.where(in_win[None, None], band, -jnp.inf)
    band = band.reshape((nh // 2, 2) + band.shape[1:])
    neg = jnp.full((nh // 2, 2, NA_QB, NA_KB), -jnp.inf, F32)
    mq, nk = NA_QROWS * NA_QB, NA_KROWS * NA_KB
    tabs = []
    for drow, ok in variants:
        per_cb = []
        for m, bs in enumerate(_na_blk_start()):
            qrows = [jnp.concatenate(
                [band[:, :, drow[dr, w], m * NA_QB:(m + 1) * NA_QB, bs:bs + NA_KB] if ok[dr, w]
                 else neg for w in range(NA_KROWS)], axis=3) for dr in range(NA_QROWS)]
            per_cb.append(jnp.concatenate(qrows, axis=2).reshape(nh // 2, 2 * mq, nk))
        tabs.append(jnp.stack(per_cb, axis=1))
    return jnp.stack(tabs), var_of_blk, ks_of_blk


def _na_kernel(var_ref, ks_ref, q_ref, k_ref, v_ref, bias_ref, o_ref, kf_ref, vf_ref):
    mq = NA_QROWS * NA_QB
    nblk = q_ref.shape[0] // (NA_QROWS * GRID_W)
    lo_half = lax.broadcasted_iota(I32, (mq, LANES), 1) < NA_HEADDIM
    kf_ref[...] = k_ref[...].astype(F32)
    vf_ref[...] = v_ref[...].astype(F32)

    def blk(i, carry):
        var = var_ref[i]
        ks = ks_ref[i]
        starts = _na_blk_start()
        q_at = [[pl.multiple_of((i * NA_QROWS + dr) * GRID_W + m * NA_QB, NA_QB)
                 for dr in range(NA_QROWS)] for m in range(len(starts))]
        k_at = [[pl.multiple_of((ks + w) * GRID_W + bs, 8) for w in range(NA_KROWS)]
                for bs in starts]
        scores = []
        for m in range(len(starts)):
            qb = jnp.concatenate([q_ref[pl.ds(o, NA_QB), :] for o in q_at[m]], axis=0)
            qb = qb * (NA_HEADDIM ** -0.5)
            zero = jnp.zeros_like(qb)
            q2 = jnp.concatenate([jnp.where(lo_half, qb, zero), jnp.where(lo_half, zero, qb)], axis=0)
            kb = jnp.concatenate([kf_ref[pl.ds(o, NA_KB), :] for o in k_at[m]], axis=0).astype(BF16)
            s = lax.dot_general(q2, kb, (((1,), (1,)), ((), ())), preferred_element_type=F32)
            scores.append(s + bias_ref[var, 0, m])
        outs = []
        for m, s in enumerate(scores):
            p = jnp.exp(s - jnp.max(s, axis=-1, keepdims=True))
            l = jnp.sum(p, axis=-1, keepdims=True)
            vb = jnp.concatenate([vf_ref[pl.ds(o, NA_KB), :] for o in k_at[m]], axis=0).astype(BF16)
            outs.append(jnp.dot(p.astype(BF16), vb, preferred_element_type=F32) / l)
        for m, o2 in enumerate(outs):
            ob = jnp.where(lo_half, o2[:mq], o2[mq:]).astype(BF16)
            for dr, o in enumerate(q_at[m]):
                o_ref[pl.ds(o, NA_QB), :] = ob[dr * NA_QB:(dr + 1) * NA_QB]
        return carry

    lax.fori_loop(0, nblk, blk, 0, unroll=8)


def _na(proj, bias, var_of_blk, ks_of_blk, q_col, bsz, seq):
    t = proj.shape[0]
    npair = NA_HEADS // 2
    grid_spec = pltpu.PrefetchScalarGridSpec(
        num_scalar_prefetch=2,
        grid=(bsz, npair),
        in_specs=[pl.BlockSpec((seq, LANES), lambda b, p, *_: (b, q_col + p)),
                  pl.BlockSpec((seq, LANES), lambda b, p, *_: (b, q_col + npair + p)),
                  pl.BlockSpec((seq, LANES), lambda b, p, *_: (b, q_col + 2 * npair + p)),
                  pl.BlockSpec((bias.shape[0], 1) + bias.shape[2:], lambda b, p, *_: (0, p, 0, 0, 0))],
        out_specs=pl.BlockSpec((seq, LANES), lambda b, p, *_: (b, p)),
        scratch_shapes=[pltpu.VMEM((seq, LANES), F32), pltpu.VMEM((seq, LANES), F32)],
    )
    return pl.pallas_call(
        _na_kernel,
        grid_spec=grid_spec,
        out_shape=jax.ShapeDtypeStruct((t, NA_HEADS * NA_HEADDIM), BF16),
        compiler_params=_cparams(("arbitrary", "arbitrary")),
        name="na",
    )(jnp.asarray(var_of_blk, I32), jnp.asarray(ks_of_blk, I32), proj, proj, proj, bias)


def _mix_kernel(x_ref, yn_ref, at_ref, gs_ref, gn_ref, g1_ref, sh_ref, sc_ref, ng_ref,
                ws_ref, wn_ref, wo_ref, wq_ref, keys_ref, h_ref, s_ref):
    y1 = jnp.dot(yn_ref[...], ws_ref[...], preferred_element_type=F32)
    y2 = jnp.dot(at_ref[...], wn_ref[...], preferred_element_type=F32)
    mixed = _sigmoid(gs_ref[...].astype(F32)) * y1 + _sigmoid(gn_ref[...].astype(F32)) * y2
    h = x_ref[...] + g1_ref[0] * jnp.dot(mixed.astype(BF16), wo_ref[...], preferred_element_type=F32)
    h_ref[...] = h
    n2 = _rms_mod(h, ng_ref[...], sh_ref[0], sc_ref[0]).astype(BF16)
    q = jnp.dot(n2, wq_ref[...], preferred_element_type=F32).astype(BF16)
    sub = 8
    for hz in range(2 * PEER_HEADS):
        qs = q[:, hz * PEER_DHALF:(hz + 1) * PEER_DHALF]
        sc = lax.dot_general(keys_ref[hz % 2], qs, (((1,), (1,)), ((), ())),
                             preferred_element_type=F32)
        for kq in range(PEER_NKEYS // sub):
            for tb in range(sc.shape[1] // LANES):
                s_ref[hz, kq, tb * sub:(tb + 1) * sub, :] = sc[kq * sub:(kq + 1) * sub,
                                                               tb * LANES:(tb + 1) * LANES]


def _mix(x2, ynorm, attn, proj, gs_col, g1, sh2, sc2, ng2, ws, wn, wo, wq, keys, seq):
    t, d = x2.shape
    tm = min(512, seq)
    bidx = lambda i: ((i * tm) // seq, 0, 0)
    row = lambda i: (i, 0)
    const2 = lambda i: (0, 0)
    return pl.pallas_call(
        _mix_kernel,
        grid=(t // tm,),
        in_specs=[pl.BlockSpec((tm, d), row), pl.BlockSpec((tm, d), row), pl.BlockSpec((tm, d), row),
                  pl.BlockSpec((tm, d), lambda i: (i, gs_col)),
                  pl.BlockSpec((tm, d), lambda i: (i, gs_col + 1)),
                  pl.BlockSpec((1, 1, d), bidx), pl.BlockSpec((1, 1, d), bidx),
                  pl.BlockSpec((1, 1, d), bidx), pl.BlockSpec((1, d), const2),
                  pl.BlockSpec((d, d), const2, pipeline_mode=pl.Buffered(1)),
                  pl.BlockSpec((d, d), const2, pipeline_mode=pl.Buffered(1)),
                  pl.BlockSpec((d, d), const2, pipeline_mode=pl.Buffered(1)),
                  pl.BlockSpec(wq.shape, const2, pipeline_mode=pl.Buffered(1)),
                  pl.BlockSpec(keys.shape, lambda i: (0, 0, 0), pipeline_mode=pl.Buffered(1))],
        out_specs=[pl.BlockSpec((tm, d), row),
                   pl.BlockSpec((2 * PEER_HEADS, PEER_NKEYS // 8, tm // LANES * 8, LANES),
                                lambda i: (0, 0, i, 0))],
        out_shape=[jax.ShapeDtypeStruct((t, d), F32),
                   jax.ShapeDtypeStruct((2 * PEER_HEADS, PEER_NKEYS // 8, t // LANES * 8, LANES), F32)],
        compiler_params=_cparams(("arbitrary",)),
        name="mix",
    )(x2, ynorm, attn, proj, proj, g1, sh2, sc2, ng2, ws, wn, wo, wq, keys)


def _hyperbola():
    return [(i, k) for i in range(PEER_TOPK) for k in range(PEER_TOPK)
            if (i + 1) * (k + 1) <= PEER_TOPK]


def _sort_network(n):
    pairs = []
    p = 1
    while p < n:
        k = p
        while k >= 1:
            for j in range(k % p, n - k, 2 * k):
                for i in range(min(k, n - j - k)):
                    if (i + j) // (2 * p) == (i + j + k) // (2 * p):
                        pairs.append((i + j, i + j + k))
            k //= 2
        p *= 2
    return pairs


def _precedes(va, pa, vb, pb):
    return (va > vb) | ((va == vb) & (pa < pb))


def _compare_exchange(v, p, i, j):
    c = _precedes(v[i], p[i], v[j], p[j])
    v[i], v[j] = jnp.where(c, v[i], v[j]), jnp.where(c, v[j], v[i])
    p[i], p[j] = jnp.where(c, p[i], p[j]), jnp.where(c, p[j], p[i])


def _top16(get, n, sv_ref, sp_ref):
    k = PEER_TOPK
    ng = n // k
    net = _sort_network(k)
    for g in range(ng):
        items = [get(g * k + j) for j in range(k)]
        v = [it[0] for it in items]
        p = [it[1] for it in items]
        for i, j in net:
            _compare_exchange(v, p, i, j)
        if ng == 1:
            return v, p
        for j in range(k):
            sv_ref[g, j] = v[j]
            sp_ref[g, j] = p[j]
    step = 1
    while True:
        for g in range(0, ng, 2 * step):
            v, p = [], []
            for j in range(k):
                xv, xp = sv_ref[g, j], sp_ref[g, j]
                yv, yp = sv_ref[g + step, k - 1 - j], sp_ref[g + step, k - 1 - j]
                c = _precedes(xv, xp, yv, yp)
                v.append(jnp.where(c, xv, yv))
                p.append(jnp.where(c, xp, yp))
            stride = k // 2
            while stride >= 1:
                for i in range(k):
                    if i & stride == 0:
                        _compare_exchange(v, p, i, i + stride)
                stride //= 2
            if 2 * step >= ng:
                return v, p
            for j in range(k):
                sv_ref[g, j] = v[j]
                sp_ref[g, j] = p[j]
        step *= 2


def _select_kernel(s_ref, eidx_ref, gate_ref, sv_ref, sp_ref):
    nk = PEER_NKEYS
    k = PEER_TOPK
    sub = 8
    shape = (sub, LANES)
    ebits = (nk * nk - 1).bit_length()

    def key(z, j):
        return s_ref[z, j // sub, pl.ds(j % sub, sub, stride=sub), :], jnp.full(shape, j, I32)

    tops = []
    for z in range(2):
        tops.append(_top16(functools.partial(key, z), nk, sv_ref, sp_ref))
    (tv0, ti0), (tv1, ti1) = tops

    pairs = _hyperbola()
    npad = -len(pairs) % k

    def cand(j):
        if j >= len(pairs):
            return jnp.full(shape, -jnp.inf, F32), jnp.full(shape, (k * k) << ebits, I32)
        i, kk = pairs[j]
        return tv0[i] + tv1[kk], ((i * k + kk) << ebits) + ti0[i] * nk + ti1[kk]

    top, ids = _top16(cand, len(pairs) + npad, sv_ref, sp_ref)
    ex = [jnp.exp(t - top[0]) for t in top]
    zsum = ex[0]
    for r in range(1, k):
        zsum = zsum + ex[r]
    for r in range(k):
        eidx_ref[0, r] = ids[r] & ((1 << ebits) - 1)
        gate_ref[0, r] = ex[r] / zsum


def _select(scores):
    r = 8
    nk = PEER_NKEYS
    tb = scores.shape[2] // r
    assert tb % r == 0
    oshape = (PEER_HEADS, PEER_TOPK, tb, LANES)
    return pl.pallas_call(
        _select_kernel,
        grid=(tb // r, PEER_HEADS),
        in_specs=[pl.BlockSpec((2, nk // r, r * r, LANES), lambda i, h: (h, 0, i, 0))],
        out_specs=[pl.BlockSpec((1, PEER_TOPK, r, LANES), lambda i, h: (h, 0, i, 0)),
                   pl.BlockSpec((1, PEER_TOPK, r, LANES), lambda i, h: (h, 0, i, 0))],
        out_shape=[jax.ShapeDtypeStruct(oshape, I32), jax.ShapeDtypeStruct(oshape, F32)],
        scratch_shapes=[pltpu.VMEM((nk // PEER_TOPK, PEER_TOPK, r, LANES), F32),
                        pltpu.VMEM((nk // PEER_TOPK, PEER_TOPK, r, LANES), I32)],
        compiler_params=_cparams(("arbitrary", "arbitrary")),
        name="select",
    )(scores)


def _peer_kernel(h_ref, sh_ref, sc_ref, g2_ref, ng_ref, fg_ref, eidx_ref, gate_ref, uv_ref,
                 out_ref, n2_ref, act_ref, wv_ref, w3_ref, acc_ref, *, ns):
    nk = PEER_NKEYS
    half = nk // 2
    hi_mask = jnp.uint32(0xFFFF0000)
    tm = h_ref.shape[0]
    eb = uv_ref.shape[0]
    nblk = eb // nk
    nblk2 = nblk // 2
    s = pl.program_id(1)

    @pl.when(s == 0)
    def _():
        n2_ref[...] = _rms_mod(h_ref[...], ng_ref[...], sh_ref[0], sc_ref[0]).astype(BF16)
        act_ref[...] = jnp.zeros_like(act_ref)

    @pl.when(s < ns)
    def _():
        e = eidx_ref[...]
        ai = jnp.right_shift(e, nk.bit_length() - 1)
        bi = jnp.bitwise_and(e, nk - 1)
        n2 = n2_ref[...]
        for a in range(0, nblk, 2):
            sc = lax.dot_general(n2, uv_ref[a * nk:(a + 2) * nk, :], (((1,), (1,)), ((), ())),
                                 preferred_element_type=F32)
            act = act_ref[...]
            for j in range(2):
                got = jnp.take_along_axis(sc[:, j * nk:(j + 1) * nk], bi, axis=1)
                act = jnp.where(ai == s * nblk + a + j, got, act)
            act_ref[...] = act

    @pl.when(s == ns - 1)
    def _():
        act = act_ref[...]
        gelu = 0.5 * act * (1.0 + lax.erf(act * (2.0 ** -0.5)))
        wv_ref[...] = gate_ref[...] * gelu
        r = lax.broadcasted_iota(I32, (nk, LANES), 0)
        sub = r.astype(F32).astype(BF16)
        rowkey = jnp.where(r < half, 2 * r, 2 * (r - half) + 1).astype(F32).astype(BF16)
        zero = jnp.zeros((nk, LANES), BF16)
        one = jnp.ones((nk, LANES), BF16)

        def tok(t, carry):
            e = eidx_ref[pl.ds(t, 1), :]
            w = wv_ref[pl.ds(t, 1), :].astype(BF16)
            ar = jnp.right_shift(e, nk.bit_length() - 1).astype(F32).astype(BF16)
            br = jnp.bitwise_and(e, nk - 1).astype(F32).astype(BF16)
            pt = jnp.where(rowkey == ar, w, zero)
            qt = jnp.where(sub == br, one, zero)
            wt = lax.dot_general(pt, qt, (((1,), (1,)), ((), ())), preferred_element_type=F32)
            bits = lax.bitcast_convert_type(wt, jnp.uint32)
            word = jnp.right_shift(bits[:half], 16) | (bits[half:] & hi_mask)
            w3_ref[pl.ds(pl.multiple_of(t * W3_PITCH, 8), half), :] = word
            return carry
        lax.fori_loop(0, tm, tok, 0, unroll=128)

    @pl.when(s >= ns)
    def _():
        base = (s - ns) * nblk2
        words = [w3_ref[pl.ds(base + a, tm, stride=W3_PITCH), :] for a in range(nblk2)]
        lo = [lax.bitcast_convert_type(jnp.left_shift(w, 16), F32).astype(BF16) for w in words]
        hi = [lax.bitcast_convert_type(w & hi_mask, F32).astype(BF16) for w in words]
        cols = [c for pair in zip(lo, hi) for c in pair]
        part = jnp.dot(jnp.concatenate(cols, axis=1), uv_ref[...], preferred_element_type=F32)

        @pl.when(s == ns)
        def _():
            acc_ref[...] = part

        @pl.when(s > ns)
        def _():
            acc_ref[...] += part

    @pl.when(s == 2 * ns - 1)
    def _():
        hh = h_ref[...] + g2_ref[0] * acc_ref[...]
        ms = jnp.mean(hh * hh, axis=-1, keepdims=True)
        out_ref[...] = hh * lax.rsqrt(ms + EPS) * fg_ref[...]


def _peer(h, sh2, sc2, g2, ng2, fg, eidx_t, gate_t, u, v, seq):
    t, d = h.shape
    ne = u.shape[0]
    tm = min(512, seq)
    eb = 4096
    ns = ne // eb
    nj = eidx_t.shape[1]
    uv = jnp.concatenate([u, v], axis=0).astype(BF16)
    bidx = lambda i, s: ((i * tm) // seq, 0, 0)
    row = lambda i, s: (i, 0)
    const2 = lambda i, s: (0, 0)
    return pl.pallas_call(
        functools.partial(_peer_kernel, ns=ns),
        grid=(t // tm, 2 * ns),
        in_specs=[pl.BlockSpec((tm, d), row),
                  pl.BlockSpec((1, 1, d), bidx), pl.BlockSpec((1, 1, d), bidx),
                  pl.BlockSpec((1, 1, d), bidx),
                  pl.BlockSpec((1, d), const2), pl.BlockSpec((1, d), const2),
                  pl.BlockSpec((tm, nj), row), pl.BlockSpec((tm, nj), row),
                  pl.BlockSpec((eb, d), lambda i, s: (s, 0))],
        out_specs=pl.BlockSpec((tm, d), row),
        out_shape=jax.ShapeDtypeStruct((t, d), F32),
        scratch_shapes=[pltpu.VMEM((tm, d), BF16),
                        pltpu.VMEM((tm, nj), F32),
                        pltpu.VMEM((tm, nj), F32),
                        pltpu.VMEM((tm * W3_PITCH, PEER_NKEYS), jnp.uint32),
                        pltpu.VMEM((tm, d), F32)],
        compiler_params=_cparams(("arbitrary", "arbitrary")),
        name="peer",
    )(h, sh2, sc2, g2, ng2, fg, eidx_t, gate_t, uv)


def _layer(x2, c, bsz, seq, w_ada, b_ada, norm1_g, w_in, conv_w, conv_b, dt_bias_f, dt_bias_b,
           a_log_f, a_log_b, d_skip, ssd_norm_g, w_ssd_br, na_rpb, w_na_br, w_out, norm2_g,
           peer_wq, peer_keys, peer_u, peer_v, out_g):
    t, d = x2.shape
    d_ssd = SSD_HEADS * SSD_HEADDIM
    d_xbc = d_ssd + 2 * SSD_GROUPS * SSD_STATE
    d_na = NA_HEADS * NA_HEADDIM
    assert seq % (SSD_CHUNK * SSD_SUB) == 0 and seq % (GRID_W * NA_QROWS) == 0
    assert seq // GRID_W >= NA_KROWS and d == d_ssd == d_na

    mod = _ada(c, w_ada, b_ada)
    sh1, sc1, g1, sh2, sc2, g2 = [m.reshape(bsz, 1, d) for m in jnp.split(mod, 6, axis=-1)]

    o = np.cumsum([0, d_ssd, d_xbc, SSD_HEADS, SSD_HEADS, 3 * d_na, d, d])
    w_main = jnp.concatenate([w_in[:, o[1]:o[2]], w_in[:, o[0]:o[1]], w_in[:, o[4]:o[7]]],
                             axis=1).astype(BF16)
    w_dt = jnp.pad(w_in[:, o[2]:o[4]], ((0, 0), (0, LANES - 2 * SSD_HEADS))).astype(BF16)
    z_col = d_xbc // d_ssd
    q_col = (d_xbc + d_ssd) // LANES
    gs_col = (d_xbc + d_ssd + 3 * d_na) // d
    proj, dt_raw = _inproj(x2, sh1, sc1, norm1_g.reshape(1, d), w_main, w_dt, seq)

    heads = np.arange(d_ssd) // SSD_HEADDIM

    def e2(off):
        m = np.zeros((LANES, d_ssd), np.float32)
        m[off + heads, np.arange(d_ssd)] = 1.0
        m[off + 64 + heads, np.arange(d_ssd)] = 1.0
        return jnp.asarray(m, BF16)

    def lanes16(vec, off):
        return jnp.zeros((1, LANES), F32).at[0, off:off + SSD_HEADS].set(vec.astype(F32))

    params = {
        "cw": jnp.pad(conv_w.astype(F32), ((0, 8 - CONV_W), (0, 0))),
        "cb": conv_b.astype(F32).reshape(1, d_xbc),
        "dtb_f": lanes16(dt_bias_f, 0), "dtb_b": lanes16(dt_bias_b, SSD_HEADS),
        "a_f": lanes16(-jnp.exp(a_log_f.astype(F32)), 0),
        "a_b": lanes16(-jnp.exp(a_log_b.astype(F32)), SSD_HEADS),
        "e2_f": e2(0), "e2_b": e2(SSD_HEADS),
        "dsk": jnp.repeat(d_skip.astype(F32), SSD_HEADDIM).reshape(1, d_ssd),
        "ng": ssd_norm_g.astype(F32).reshape(1, d_ssd),
    }
    fwd = _ssd(proj, dt_raw, z_col, None, params, bsz, seq, rev=False)
    ynorm = _ssd(proj, dt_raw, z_col, fwd, params, bsz, seq, rev=True)

    bias, var_of_blk, ks_of_blk = _na_bias(na_rpb, seq // GRID_W)
    attn = _na(proj, bias, var_of_blk, ks_of_blk, q_col, bsz, seq)

    h, scores = _mix(x2, ynorm, attn, proj, gs_col, g1, sh2, sc2, norm2_g.reshape(1, d),
                     w_ssd_br.astype(BF16), w_na_br.astype(BF16), w_out.astype(BF16),
                     peer_wq.astype(BF16), peer_keys.astype(BF16), seq)

    eidx, gate = _select(scores)
    nj = PEER_HEADS * PEER_TOPK
    eidx_t = eidx.reshape(nj, t).T
    gate_t = gate.reshape(nj, t).T
    return _peer(h, sh2, sc2, g2, norm2_g.reshape(1, d), out_g, eidx_t, gate_t,
                 peer_u.astype(BF16), peer_v.astype(BF16), seq)


def kernel(x, c, w_ada, b_ada, norm1_g, w_in, conv_w, conv_b, dt_bias_f, dt_bias_b, a_log_f, a_log_b,
           d_skip, ssd_norm_g, w_ssd_br, na_rpb, w_na_br, w_out, norm2_g, peer_wq, peer_keys, peer_u,
           peer_v, final_g):
    bsz, seq, d = x.shape
    depth = w_ada.shape[0]
    assert depth == 1, "the final RMSNorm is fused into the last layer's PEER kernel"
    i = 0
    out = _layer(x.reshape(bsz * seq, d), c, bsz, seq, w_ada[i], b_ada[i], norm1_g[i], w_in[i],
                 conv_w[i], conv_b[i], dt_bias_f[i], dt_bias_b[i], a_log_f[i], a_log_b[i], d_skip[i],
                 ssd_norm_g[i], w_ssd_br[i], na_rpb[i], w_na_br[i], w_out[i], norm2_g[i],
                 peer_wq[i], peer_keys[i], peer_u[i], peer_v[i], final_g.reshape(1, d))
    return out.reshape(bsz, seq, d)
```

```python
import functools

import numpy as np
import jax
import jax.numpy as jnp
from jax import lax
from jax.experimental import pallas as pl
from jax.experimental.pallas import tpu as pltpu

F32 = jnp.float32
BF16 = jnp.bfloat16
I32 = jnp.int32

EPS = 1e-6
GRID_W = 64
SSD_HEADS = 16
SSD_HEADDIM = 64
SSD_GROUPS = 4
SSD_STATE = 128
SSD_CHUNK = 128
SSD_SUB = 4
CONV_W = 5
NA_HEADS = 16
NA_HEADDIM = 64
NA_KH = 8
NA_KW = 16
NA_QROWS = 4
NA_KROWS = 12
NA_QB = 16
NA_KB = NA_QB + NA_KW
PEER_HEADS = 8
PEER_NKEYS = 128
PEER_TOPK = 16
PEER_DHALF = 128
W3_PITCH = PEER_NKEYS // 2 + 8

LANES = 128
HALO = 16
VMEM_LIMIT = 56 * 1024 * 1024


def _cparams(sem):
    return pltpu.CompilerParams(dimension_semantics=sem, vmem_limit_bytes=VMEM_LIMIT)


def _sigmoid(x):
    return 0.5 * jnp.tanh(0.5 * x) + 0.5


def _silu(x):
    h = 0.5 * x
    return h * (jnp.tanh(h) + 1.0)


def _rms_mod(x, g, shift, scale):
    ms = jnp.mean(x * x, axis=-1, keepdims=True)
    y = x * lax.rsqrt(ms + EPS) * g
    return y * (1.0 + scale) + shift


def _ada_kernel(c_ref, w_ref, b_ref, o_ref):
    c = c_ref[...]
    sc = _silu(c)
    o_ref[...] = jnp.dot(sc, w_ref[...], preferred_element_type=F32,
                         precision=lax.Precision.HIGHEST) + b_ref[...]


def _ada(c, w, b):
    bsz, d = c.shape
    n = w.shape[1]
    tn = 1024
    return pl.pallas_call(
        _ada_kernel,
        grid=(n // tn,),
        in_specs=[pl.BlockSpec((bsz, d), lambda j: (0, 0)),
                  pl.BlockSpec((d, tn), lambda j: (0, j)),
                  pl.BlockSpec((1, tn), lambda j: (0, j))],
        out_specs=pl.BlockSpec((bsz, tn), lambda j: (0, j)),
        out_shape=jax.ShapeDtypeStruct((bsz, n), F32),
        compiler_params=_cparams(("arbitrary",)),
        name="ada",
    )(c, w, b.reshape(1, n))


def _inproj_kernel(x_ref, sh_ref, sc_ref, g_ref, w_ref, wdt_ref, proj_ref, dt_ref):
    n1 = _rms_mod(x_ref[...], g_ref[...], sh_ref[0], sc_ref[0]).astype(BF16)
    dt_ref[...] = jnp.dot(n1, wdt_ref[...], preferred_element_type=F32)
    tn = 1024
    for j in range(w_ref.shape[1] // tn):
        proj_ref[:, j * tn:(j + 1) * tn] = jnp.dot(
            n1, w_ref[:, j * tn:(j + 1) * tn], preferred_element_type=F32).astype(BF16)


def _inproj(x2, sh, sc, g, w, wdt, seq):
    t, d = x2.shape
    n = w.shape[1]
    tm = min(512, seq)
    bidx = lambda i: ((i * tm) // seq, 0, 0)
    const2 = lambda i: (0, 0)
    return pl.pallas_call(
        _inproj_kernel,
        grid=(t // tm,),
        in_specs=[pl.BlockSpec((tm, d), lambda i: (i, 0)),
                  pl.BlockSpec((1, 1, d), bidx),
                  pl.BlockSpec((1, 1, d), bidx),
                  pl.BlockSpec((1, d), const2),
                  pl.BlockSpec((d, n), const2, pipeline_mode=pl.Buffered(1)),
                  pl.BlockSpec((d, LANES), const2, pipeline_mode=pl.Buffered(1))],
        out_specs=[pl.BlockSpec((tm, n), lambda i: (i, 0)),
                   pl.BlockSpec((tm, LANES), lambda i: (i, 0))],
        out_shape=[jax.ShapeDtypeStruct((t, n), BF16),
                   jax.ShapeDtypeStruct((t, LANES), F32)],
        compiler_params=_cparams(("arbitrary",)),
        name="inproj",
    )(x2, sh, sc, g, w, wdt)


def _expand_heads(v, e2_ref):
    hi = v.astype(BF16).astype(F32)
    comb = (hi + pltpu.roll(v - hi, 64, 1)).astype(BF16)
    return jnp.dot(comb, e2_ref[...], preferred_element_type=F32)


def _ssd_conv(main, ext, cw_ref, cb_ref):
    ck = main.shape[0]
    d_ssd = SSD_HEADS * SSD_HEADDIM
    erow = lax.broadcasted_iota(I32, (ck, ck + 2 * HALO), 0)
    ecol = lax.broadcasted_iota(I32, (ck, ck + 2 * HALO), 1)
    mid = CONV_W // 2
    acc = cb_ref[...] + main.astype(F32) * cw_ref[mid:mid + 1, :]
    for w in range(CONV_W):
        if w != mid:
            shift = jnp.where(ecol == erow + (HALO + w - mid), 1.0, 0.0).astype(BF16)
            acc = acc + jnp.dot(shift, ext, preferred_element_type=F32) * cw_ref[w:w + 1, :]
    xbc = _silu(acc)
    return xbc[:, :d_ssd], xbc[:, d_ssd:].astype(BF16)


def _ssd_scan(xs, bc, dt_raw, dtb_ref, a_ref, e2_ref, st_ref, rev):
    ck = SSD_CHUNK
    d_ssd = SSD_HEADS * SSD_HEADDIM
    d_bc = SSD_GROUPS * SSD_STATE
    gw = d_ssd // SSD_GROUPS
    lane = lax.broadcasted_iota(I32, (ck, LANES), 1)
    row = lax.broadcasted_iota(I32, (ck, LANES), 0)
    off = SSD_HEADS if rev else 0
    valid = (lane >= off) & (lane < off + SSD_HEADS)
    dtr = dt_raw + dtb_ref[...]
    dt = jnp.where(valid, jnp.maximum(dtr, 0.0) + jnp.log1p(jnp.exp(-jnp.abs(dtr))), 0.0)
    da = dt * a_ref[...]
    tri = (lane >= row) if rev else (lane <= row)
    trib = jnp.where(tri, 1.0, 0.0).astype(BF16)
    hi = da.astype(BF16)
    r1 = da - hi.astype(F32)
    mid = r1.astype(BF16)
    lo = (r1 - mid.astype(F32)).astype(BF16)
    cs = (jnp.dot(trib, hi, preferred_element_type=F32)
          + jnp.dot(trib, mid, preferred_element_type=F32)
          + jnp.dot(trib, lo, preferred_element_type=F32))
    cst = cs.T
    tot_row = 0 if rev else ck - 1
    tot = cs[tot_row:tot_row + 1, :]
    dec = jnp.where(valid, jnp.exp(tot - cs), 0.0)
    ecs = jnp.where(valid, jnp.exp(cs), 0.0)
    ecs_e = _expand_heads(ecs, e2_ref)
    xdt = xs * _expand_heads(dt, e2_ref)
    xdec = (xs * _expand_heads(dt * dec, e2_ref)).astype(BF16)

    lo_half = lax.broadcasted_iota(I32, (ck, LANES), 1) < SSD_HEADDIM
    hpg = SSD_HEADS // SSD_GROUPS
    ys = []
    for g in range(SSD_GROUPS):
        bg = bc[:, g * SSD_STATE:(g + 1) * SSD_STATE]
        cg = bc[:, d_bc + g * SSD_STATE:d_bc + (g + 1) * SSD_STATE]
        cbg = lax.dot_general(cg, bg, (((1,), (1,)), ((), ())), preferred_element_type=F32)
        yg = []
        for q in range(hpg // 2):
            ls = []
            for e in range(2):
                col = off + g * hpg + q * 2 + e
                dlt = cs[:, col:col + 1] - cst[col:col + 1, :]
                seg = jnp.exp(jnp.where(tri, dlt, -jnp.inf))
                ls.append((cbg * seg).astype(BF16))
            blk = g * (hpg // 2) + q
            xq = xdt[:, blk * LANES:(blk + 1) * LANES]
            rhs = jnp.concatenate([jnp.where(lo_half, xq, 0.0), jnp.where(lo_half, 0.0, xq)],
                                  axis=0).astype(BF16)
            yg.append(jnp.dot(jnp.concatenate(ls, axis=1), rhs, preferred_element_type=F32))
        y_diag = jnp.concatenate(yg, axis=1)
        st = st_ref[g]
        e_g = ecs_e[:, g * gw:(g + 1) * gw]
        y_off = jnp.dot(cg, st.astype(BF16), preferred_element_type=F32) * e_g
        ys.append(y_diag + y_off)
        upd = jnp.dot(bg.astype(F32).T.astype(BF16), xdec[:, g * gw:(g + 1) * gw],
                      preferred_element_type=F32)
        st_ref[g] = st * e_g[tot_row:tot_row + 1, :] + upd
    return jnp.concatenate(ys, axis=1)


def _ssd_kernel(*refs, rev):
    if rev:
        (xs_ref, bc_ref, dt_ref, z_ref, yf_ref, dtb_ref, a_ref, e2_ref, dsk_ref, ng_ref,
         out_ref, st_ref) = refs
    else:
        (xm_ref, xp_ref, xn_ref, dt_ref, cw_ref, cb_ref, dtb_ref, a_ref, e2_ref,
         out_ref, xs_out_ref, bc_out_ref, st_ref) = refs
    ck = SSD_CHUNK
    c = pl.program_id(1)
    nc = pl.num_programs(1)
    ce = nc - 1 - c if rev else c

    @pl.when(c == 0)
    def _():
        st_ref[...] = jnp.zeros_like(st_ref)

    rows = [slice(j * ck, (j + 1) * ck) for j in range(SSD_SUB)]
    if rev:
        conv = [(xs_ref[r, :], bc_ref[r, :]) for r in rows]
    else:
        main = xm_ref[...]
        zero_halo = jnp.zeros((HALO, main.shape[1]), main.dtype)
        ext = jnp.concatenate([jnp.where(ce == 0, zero_halo, xp_ref[...]), main,
                               jnp.where(ce == nc - 1, zero_halo, xn_ref[...])], axis=0)
        conv = [_ssd_conv(main[r], ext[r.start:r.stop + 2 * HALO], cw_ref, cb_ref) for r in rows]
        for r, (xs, bc) in zip(rows, conv):
            xs_out_ref[r, :] = xs
            bc_out_ref[r, :] = bc

    for j in (reversed(range(SSD_SUB)) if rev else range(SSD_SUB)):
        r = rows[j]
        xs, bc = conv[j]
        y = _ssd_scan(xs, bc, dt_ref[r, :], dtb_ref, a_ref, e2_ref, st_ref, rev)
        if rev:
            y = yf_ref[r, :] + y + dsk_ref[...] * xs
            z = z_ref[r, :].astype(F32)
            u = y * _silu(z)
            ms = jnp.mean(u * u, axis=-1, keepdims=True)
            out_ref[r, :] = (u * lax.rsqrt(ms + EPS) * ng_ref[...]).astype(BF16)
        else:
            out_ref[r, :] = y


def _ssd(proj, dt_raw, z_col, fwd, params, bsz, seq, rev):
    t = proj.shape[0]
    ck = SSD_CHUNK * SSD_SUB
    nc = seq // ck
    d_xbc = params["cw"].shape[1]
    d_ssd = SSD_HEADS * SSD_HEADDIM
    d_bc = d_xbc - d_ssd
    per = ck // HALO
    last_halo = t // HALO - 1

    def ce_of(c):
        return nc - 1 - c if rev else c

    main = lambda b, c: (b * nc + ce_of(c), 0)
    prev = lambda b, c: (jnp.maximum((b * nc + ce_of(c)) * per - 1, 0), 0)
    nxt = lambda b, c: (jnp.minimum((b * nc + ce_of(c)) * per + per, last_halo), 0)
    const2 = lambda b, c: (0, 0)
    d = "b" if rev else "f"
    dir_specs = [pl.BlockSpec((1, LANES), const2), pl.BlockSpec((1, LANES), const2),
                 pl.BlockSpec((LANES, d_ssd), const2)]
    dir_args = [params["dtb_" + d], params["a_" + d], params["e2_" + d]]
    if rev:
        yf, xs, bc = fwd
        in_specs = [pl.BlockSpec((ck, d_ssd), main), pl.BlockSpec((ck, d_bc), main),
                    pl.BlockSpec((ck, LANES), main),
                    pl.BlockSpec((ck, d_ssd), lambda b, c: (b * nc + ce_of(c), z_col)),
                    pl.BlockSpec((ck, d_ssd), main)] + dir_specs + [
                        pl.BlockSpec((1, d_ssd), const2), pl.BlockSpec((1, d_ssd), const2)]
        args = [xs, bc, dt_raw, proj, yf] + dir_args + [params["dsk"], params["ng"]]
        out_specs = pl.BlockSpec((ck, d_ssd), main)
        out_shape = jax.ShapeDtypeStruct((t, d_ssd), BF16)
    else:
        in_specs = [pl.BlockSpec((ck, d_xbc), main), pl.BlockSpec((HALO, d_xbc), prev),
                    pl.BlockSpec((HALO, d_xbc), nxt), pl.BlockSpec((ck, LANES), main),
                    pl.BlockSpec((8, d_xbc), const2), pl.BlockSpec((1, d_xbc), const2)] + dir_specs
        args = [proj, proj, proj, dt_raw, params["cw"], params["cb"]] + dir_args
        out_specs = [pl.BlockSpec((ck, d_ssd), main), pl.BlockSpec((ck, d_ssd), main),
                     pl.BlockSpec((ck, d_bc), main)]
        out_shape = [jax.ShapeDtypeStruct((t, d_ssd), F32), jax.ShapeDtypeStruct((t, d_ssd), F32),
                     jax.ShapeDtypeStruct((t, d_bc), BF16)]
    return pl.pallas_call(
        functools.partial(_ssd_kernel, rev=rev),
        grid=(bsz, nc),
        in_specs=in_specs,
        out_specs=out_specs,
        out_shape=out_shape,
        scratch_shapes=[pltpu.VMEM((SSD_GROUPS, SSD_STATE, d_ssd // SSD_GROUPS), F32)],
        compiler_params=_cparams(("arbitrary", "arbitrary")),
        name="ssd_bwd" if rev else "ssd_fwd",
    )(*args)


def _na_plan(rows):
    nblk = rows // NA_QROWS
    variants, var_of_blk, ks_of_blk = [], [], []
    for i in range(nblk):
        ks = int(np.clip(i * NA_QROWS - NA_KH // 2, 0, rows - NA_KROWS))
        drow = np.zeros((NA_QROWS, NA_KROWS), np.int64)
        ok = np.zeros((NA_QROWS, NA_KROWS), bool)
        for dr in range(NA_QROWS):
            r = i * NA_QROWS + dr
            rs = int(np.clip(r - NA_KH // 2, 0, rows - NA_KH))
            for j in range(NA_KH):
                w = rs + j - ks
                drow[dr, w] = rs + j - r + NA_KH - 1
                ok[dr, w] = True
        key = (drow.tobytes(), ok.tobytes())
        for vi, (k2, _, _) in enumerate(variants):
            if k2 == key:
                var_of_blk.append(vi)
                break
        else:
            var_of_blk.append(len(variants))
            variants.append((key, drow, ok))
        ks_of_blk.append(ks)
    return [(d, o) for _, d, o in variants], var_of_blk, ks_of_blk


def _na_blk_start():
    ncb = GRID_W // NA_QB
    return [int(v) for v in np.clip(np.arange(ncb) * NA_QB - NA_KW // 2, 0, GRID_W - NA_KB)]


def _na_bias(rpb, rows):
    variants, var_of_blk, ks_of_blk = _na_plan(rows)
    nh = rpb.shape[0]
    cols = np.arange(GRID_W)
    win_start = np.clip(cols - NA_KW // 2, 0, GRID_W - NA_KW)
    in_win = (cols[None, :] >= win_start[:, None]) & (cols[None, :] < win_start[:, None] + NA_KW)
    dcol = np.clip(cols[None, :] - cols[:, None] + NA_KW - 1, 0, 2 * NA_KW - 2)
    onehot = (dcol[:, :, None] == np.arange(2 * NA_KW - 1)).astype(np.float32)
    band = jnp.einsum("hab,qkb->haqk", rpb.astype(F32), onehot, precision=lax.Precision.HIGHEST)
    band = jnp.where(in_win[None, None], band, -jnp.inf)
    band = band.reshape((nh // 2, 2) + band.shape[1:])
    neg = jnp.full((nh // 2, 2, NA_QB, NA_KB), -jnp.inf, F32)
    mq, nk = NA_QROWS * NA_QB, NA_KROWS * NA_KB
    tabs = []
    for drow, ok in variants:
        per_cb = []
        for m, bs in enumerate(_na_blk_start()):
            qrows = [jnp.concatenate(
                [band[:, :, drow[dr, w], m * NA_QB:(m + 1) * NA_QB, bs:bs + NA_KB] if ok[dr, w]
                 else neg for w in range(NA_KROWS)], axis=3) for dr in range(NA_QROWS)]
            per_cb.append(jnp.concatenate(qrows, axis=2).reshape(nh // 2, 2 * mq, nk))
        tabs.append(jnp.stack(per_cb, axis=1))
    return jnp.stack(tabs), var_of_blk, ks_of_blk


def _na_kernel(var_ref, ks_ref, q_ref, k_ref, v_ref, bias_ref, o_ref, kf_ref, vf_ref):
    mq = NA_QROWS * NA_QB
    nblk = q_ref.shape[0] // (NA_QROWS * GRID_W)
    lo_half = lax.broadcasted_iota(I32, (mq, LANES), 1) < NA_HEADDIM
    kf_ref[...] = k_ref[...].astype(F32)
    vf_ref[...] = v_ref[...].astype(F32)

    def blk(i, carry):
        var = var_ref[i]
        ks = ks_ref[i]
        starts = _na_blk_start()
        q_at = [[pl.multiple_of((i * NA_QROWS + dr) * GRID_W + m * NA_QB, NA_QB)
                 for dr in range(NA_QROWS)] for m in range(len(starts))]
        k_at = [[pl.multiple_of((ks + w) * GRID_W + bs, 8) for w in range(NA_KROWS)]
                for bs in starts]
        scores = []
        for m in range(len(starts)):
            qb = jnp.concatenate([q_ref[pl.ds(o, NA_QB), :] for o in q_at[m]], axis=0)
            qb = qb * (NA_HEADDIM ** -0.5)
            zero = jnp.zeros_like(qb)
            q2 = jnp.concatenate([jnp.where(lo_half, qb, zero), jnp.where(lo_half, zero, qb)], axis=0)
            kb = jnp.concatenate([kf_ref[pl.ds(o, NA_KB), :] for o in k_at[m]], axis=0).astype(BF16)
            s = lax.dot_general(q2, kb, (((1,), (1,)), ((), ())), preferred_element_type=F32)
            scores.append(s + bias_ref[var, 0, m])
        outs = []
        for m, s in enumerate(scores):
            p = jnp.exp(s - jnp.max(s, axis=-1, keepdims=True))
            l = jnp.sum(p, axis=-1, keepdims=True)
            vb = jnp.concatenate([vf_ref[pl.ds(o, NA_KB), :] for o in k_at[m]], axis=0).astype(BF16)
            outs.append(jnp.dot(p.astype(BF16), vb, preferred_element_type=F32) / l)
        for m, o2 in enumerate(outs):
            ob = jnp.where(lo_half, o2[:mq], o2[mq:]).astype(BF16)
            for dr, o in enumerate(q_at[m]):
                o_ref[pl.ds(o, NA_QB), :] = ob[dr * NA_QB:(dr + 1) * NA_QB]
        return carry

    lax.fori_loop(0, nblk, blk, 0, unroll=8)


def _na(proj, bias, var_of_blk, ks_of_blk, q_col, bsz, seq):
    t = proj.shape[0]
    npair = NA_HEADS // 2
    grid_spec = pltpu.PrefetchScalarGridSpec(
        num_scalar_prefetch=2,
        grid=(bsz, npair),
        in_specs=[pl.BlockSpec((seq, LANES), lambda b, p, *_: (b, q_col + p)),
                  pl.BlockSpec((seq, LANES), lambda b, p, *_: (b, q_col + npair + p)),
                  pl.BlockSpec((seq, LANES), lambda b, p, *_: (b, q_col + 2 * npair + p)),
                  pl.BlockSpec((bias.shape[0], 1) + bias.shape[2:], lambda b, p, *_: (0, p, 0, 0, 0))],
        out_specs=pl.BlockSpec((seq, LANES), lambda b, p, *_: (b, p)),
        scratch_shapes=[pltpu.VMEM((seq, LANES), F32), pltpu.VMEM((seq, LANES), F32)],
    )
    return pl.pallas_call(
        _na_kernel,
        grid_spec=grid_spec,
        out_shape=jax.ShapeDtypeStruct((t, NA_HEADS * NA_HEADDIM), BF16),
        compiler_params=_cparams(("arbitrary", "arbitrary")),
        name="na",
    )(jnp.asarray(var_of_blk, I32), jnp.asarray(ks_of_blk, I32), proj, proj, proj, bias)


def _mix_kernel(x_ref, yn_ref, at_ref, gs_ref, gn_ref, g1_ref, sh_ref, sc_ref, ng_ref,
                ws_ref, wn_ref, wo_ref, wq_ref, keys_ref, h_ref, s_ref):
    y1 = jnp.dot(yn_ref[...], ws_ref[...], preferred_element_type=F32)
    y2 = jnp.dot(at_ref[...], wn_ref[...], preferred_element_type=F32)
    mixed = _sigmoid(gs_ref[...].astype(F32)) * y1 + _sigmoid(gn_ref[...].astype(F32)) * y2
    h = x_ref[...] + g1_ref[0] * jnp.dot(mixed.astype(BF16), wo_ref[...], preferred_element_type=F32)
    h_ref[...] = h
    n2 = _rms_mod(h, ng_ref[...], sh_ref[0], sc_ref[0]).astype(BF16)
    q = jnp.dot(n2, wq_ref[...], preferred_element_type=F32).astype(BF16)
    sub = 8
    for hz in range(2 * PEER_HEADS):
        qs = q[:, hz * PEER_DHALF:(hz + 1) * PEER_DHALF]
        sc = lax.dot_general(keys_ref[hz % 2], qs, (((1,), (1,)), ((), ())),
                             preferred_element_type=F32)
        for kq in range(PEER_NKEYS // sub):
            for tb in range(sc.shape[1] // LANES):
                s_ref[hz, kq, tb * sub:(tb + 1) * sub, :] = sc[kq * sub:(kq + 1) * sub,
                                                               tb * LANES:(tb + 1) * LANES]


def _mix(x2, ynorm, attn, proj, gs_col, g1, sh2, sc2, ng2, ws, wn, wo, wq, keys, seq):
    t, d = x2.shape
    tm = min(512, seq)
    bidx = lambda i: ((i * tm) // seq, 0, 0)
    row = lambda i: (i, 0)
    const2 = lambda i: (0, 0)
    return pl.pallas_call(
        _mix_kernel,
        grid=(t // tm,),
        in_specs=[pl.BlockSpec((tm, d), row), pl.BlockSpec((tm, d), row), pl.BlockSpec((tm, d), row),
                  pl.BlockSpec((tm, d), lambda i: (i, gs_col)),
                  pl.BlockSpec((tm, d), lambda i: (i, gs_col + 1)),
                  pl.BlockSpec((1, 1, d), bidx), pl.BlockSpec((1, 1, d), bidx),
                  pl.BlockSpec((1, 1, d), bidx), pl.BlockSpec((1, d), const2),
                  pl.BlockSpec((d, d), const2, pipeline_mode=pl.Buffered(1)),
                  pl.BlockSpec((d, d), const2, pipeline_mode=pl.Buffered(1)),
                  pl.BlockSpec((d, d), const2, pipeline_mode=pl.Buffered(1)),
                  pl.BlockSpec(wq.shape, const2, pipeline_mode=pl.Buffered(1)),
                  pl.BlockSpec(keys.shape, lambda i: (0, 0, 0), pipeline_mode=pl.Buffered(1))],
        out_specs=[pl.BlockSpec((tm, d), row),
                   pl.BlockSpec((2 * PEER_HEADS, PEER_NKEYS // 8, tm // LANES * 8, LANES),
                                lambda i: (0, 0, i, 0))],
        out_shape=[jax.ShapeDtypeStruct((t, d), F32),
                   jax.ShapeDtypeStruct((2 * PEER_HEADS, PEER_NKEYS // 8, t // LANES * 8, LANES), F32)],
        compiler_params=_cparams(("arbitrary",)),
        name="mix",
    )(x2, ynorm, attn, proj, proj, g1, sh2, sc2, ng2, ws, wn, wo, wq, keys)


def _hyperbola():
    return [(i, k) for i in range(PEER_TOPK) for k in range(PEER_TOPK)
            if (i + 1) * (k + 1) <= PEER_TOPK]


def _sort_network(n):
    pairs = []
    p = 1
    while p < n:
        k = p
        while k >= 1:
            for j in range(k % p, n - k, 2 * k):
                for i in range(min(k, n - j - k)):
                    if (i + j) // (2 * p) == (i + j + k) // (2 * p):
                        pairs.append((i + j, i + j + k))
            k //= 2
        p *= 2
    return pairs


def _precedes(va, pa, vb, pb):
    return (va > vb) | ((va == vb) & (pa < pb))


def _compare_exchange(v, p, i, j):
    c = _precedes(v[i], p[i], v[j], p[j])
    v[i], v[j] = jnp.where(c, v[i], v[j]), jnp.where(c, v[j], v[i])
    p[i], p[j] = jnp.where(c, p[i], p[j]), jnp.where(c, p[j], p[i])


def _top16(get, n, sv_ref, sp_ref):
    k = PEER_TOPK
    ng = n // k
    net = _sort_network(k)
    for g in range(ng):
        items = [get(g * k + j) for j in range(k)]
        v = [it[0] for it in items]
        p = [it[1] for it in items]
        for i, j in net:
            _compare_exchange(v, p, i, j)
        if ng == 1:
            return v, p
        for j in range(k):
            sv_ref[g, j] = v[j]
            sp_ref[g, j] = p[j]
    step = 1
    while True:
        for g in range(0, ng, 2 * step):
            v, p = [], []
            for j in range(k):
                xv, xp = sv_ref[g, j], sp_ref[g, j]
                yv, yp = sv_ref[g + step, k - 1 - j], sp_ref[g + step, k - 1 - j]
                c = _precedes(xv, xp, yv, yp)
                v.append(jnp.where(c, xv, yv))
                p.append(jnp.where(c, xp, yp))
            stride = k // 2
            while stride >= 1:
                for i in range(k):
                    if i & stride == 0:
                        _compare_exchange(v, p, i, i + stride)
                stride //= 2
            if 2 * step >= ng:
                return v, p
            for j in range(k):
                sv_ref[g, j] = v[j]
                sp_ref[g, j] = p[j]
        step *= 2


def _select_kernel(s_ref, eidx_ref, gate_ref, sv_ref, sp_ref):
    nk = PEER_NKEYS
    k = PEER_TOPK
    sub = 8
    shape = (sub, LANES)
    ebits = (nk * nk - 1).bit_length()

    def key(z, j):
        return s_ref[z, j // sub, pl.ds(j % sub, sub, stride=sub), :], jnp.full(shape, j, I32)

    tops = []
    for z in range(2):
        tops.append(_top16(functools.partial(key, z), nk, sv_ref, sp_ref))
    (tv0, ti0), (tv1, ti1) = tops

    pairs = _hyperbola()
    npad = -len(pairs) % k

    def cand(j):
        if j >= len(pairs):
            return jnp.full(shape, -jnp.inf, F32), jnp.full(shape, (k * k) << ebits, I32)
        i, kk = pairs[j]
        return tv0[i] + tv1[kk], ((i * k + kk) << ebits) + ti0[i] * nk + ti1[kk]

    top, ids = _top16(cand, len(pairs) + npad, sv_ref, sp_ref)
    ex = [jnp.exp(t - top[0]) for t in top]
    zsum = ex[0]
    for r in range(1, k):
        zsum = zsum + ex[r]
    for r in range(k):
        eidx_ref[0, r] = ids[r] & ((1 << ebits) - 1)
        gate_ref[0, r] = ex[r] / zsum


def _select(scores):
    r = 8
    nk = PEER_NKEYS
    tb = scores.shape[2] // r
    assert tb % r == 0
    oshape = (PEER_HEADS, PEER_TOPK, tb, LANES)
    return pl.pallas_call(
        _select_kernel,
        grid=(tb // r, PEER_HEADS),
        in_specs=[pl.BlockSpec((2, nk // r, r * r, LANES), lambda i, h: (h, 0, i, 0))],
        out_specs=[pl.BlockSpec((1, PEER_TOPK, r, LANES), lambda i, h: (h, 0, i, 0)),
                   pl.BlockSpec((1, PEER_TOPK, r, LANES), lambda i, h: (h, 0, i, 0))],
        out_shape=[jax.ShapeDtypeStruct(oshape, I32), jax.ShapeDtypeStruct(oshape, F32)],
        scratch_shapes=[pltpu.VMEM((nk // PEER_TOPK, PEER_TOPK, r, LANES), F32),
                        pltpu.VMEM((nk // PEER_TOPK, PEER_TOPK, r, LANES), I32)],
        compiler_params=_cparams(("arbitrary", "arbitrary")),
        name="select",
    )(scores)


def _peer_kernel(h_ref, sh_ref, sc_ref, g2_ref, ng_ref, fg_ref, eidx_ref, gate_ref, uv_ref,
                 out_ref, n2_ref, act_ref, wv_ref, w3_ref, acc_ref, *, ns):
    nk = PEER_NKEYS
    half = nk // 2
    hi_mask = jnp.uint32(0xFFFF0000)
    tm = h_ref.shape[0]
    eb = uv_ref.shape[0]
    nblk = eb // nk
    nblk2 = nblk // 2
    s = pl.program_id(1)

    @pl.when(s == 0)
    def _():
        n2_ref[...] = _rms_mod(h_ref[...], ng_ref[...], sh_ref[0], sc_ref[0]).astype(BF16)
        act_ref[...] = jnp.zeros_like(act_ref)

    @pl.when(s < ns)
    def _():
        e = eidx_ref[...]
        ai = jnp.right_shift(e, nk.bit_length() - 1)
        bi = jnp.bitwise_and(e, nk - 1)
        n2 = n2_ref[...]
        for a in range(0, nblk, 2):
            sc = lax.dot_general(n2, uv_ref[a * nk:(a + 2) * nk, :], (((1,), (1,)), ((), ())),
                                 preferred_element_type=F32)
            act = act_ref[...]
            for j in range(2):
                got = jnp.take_along_axis(sc[:, j * nk:(j + 1) * nk], bi, axis=1)
                act = jnp.where(ai == s * nblk + a + j, got, act)
            act_ref[...] = act

    @pl.when(s == ns - 1)
    def _():
        act = act_ref[...]
        gelu = 0.5 * act * (1.0 + lax.erf(act * (2.0 ** -0.5)))
        wv_ref[...] = gate_ref[...] * gelu
        r = lax.broadcasted_iota(I32, (nk, LANES), 0)
        sub = r.astype(F32).astype(BF16)
        rowkey = jnp.where(r < half, 2 * r, 2 * (r - half) + 1).astype(F32).astype(BF16)
        zero = jnp.zeros((nk, LANES), BF16)
        one = jnp.ones((nk, LANES), BF16)

        def tok(t, carry):
            e = eidx_ref[pl.ds(t, 1), :]
            w = wv_ref[pl.ds(t, 1), :].astype(BF16)
            ar = jnp.right_shift(e, nk.bit_length() - 1).astype(F32).astype(BF16)
            br = jnp.bitwise_and(e, nk - 1).astype(F32).astype(BF16)
            pt = jnp.where(rowkey == ar, w, zero)
            qt = jnp.where(sub == br, one, zero)
            wt = lax.dot_general(pt, qt, (((1,), (1,)), ((), ())), preferred_element_type=F32)
            bits = lax.bitcast_convert_type(wt, jnp.uint32)
            word = jnp.right_shift(bits[:half], 16) | (bits[half:] & hi_mask)
            w3_ref[pl.ds(pl.multiple_of(t * W3_PITCH, 8), half), :] = word
            return carry
        lax.fori_loop(0, tm, tok, 0, unroll=128)

    @pl.when(s >= ns)
    def _():
        base = (s - ns) * nblk2
        words = [w3_ref[pl.ds(base + a, tm, stride=W3_PITCH), :] for a in range(nblk2)]
        lo = [lax.bitcast_convert_type(jnp.left_shift(w, 16), F32).astype(BF16) for w in words]
        hi = [lax.bitcast_convert_type(w & hi_mask, F32).astype(BF16) for w in words]
        cols = [c for pair in zip(lo, hi) for c in pair]
        part = jnp.dot(jnp.concatenate(cols, axis=1), uv_ref[...], preferred_element_type=F32)

        @pl.when(s == ns)
        def _():
            acc_ref[...] = part

        @pl.when(s > ns)
        def _():
            acc_ref[...] += part

    @pl.when(s == 2 * ns - 1)
    def _():
        hh = h_ref[...] + g2_ref[0] * acc_ref[...]
        ms = jnp.mean(hh * hh, axis=-1, keepdims=True)
        out_ref[...] = hh * lax.rsqrt(ms + EPS) * fg_ref[...]


def _peer(h, sh2, sc2, g2, ng2, fg, eidx_t, gate_t, u, v, seq):
    t, d = h.shape
    ne = u.shape[0]
    tm = min(512, seq)
    eb = 4096
    ns = ne // eb
    nj = eidx_t.shape[1]
    uv = jnp.concatenate([u, v], axis=0).astype(BF16)
    bidx = lambda i, s: ((i * tm) // seq, 0, 0)
    row = lambda i, s: (i, 0)
    const2 = lambda i, s: (0, 0)
    return pl.pallas_call(
        functools.partial(_peer_kernel, ns=ns),
        grid=(t // tm, 2 * ns),
        in_specs=[pl.BlockSpec((tm, d), row),
                  pl.BlockSpec((1, 1, d), bidx), pl.BlockSpec((1, 1, d), bidx),
                  pl.BlockSpec((1, 1, d), bidx),
                  pl.BlockSpec((1, d), const2), pl.BlockSpec((1, d), const2),
                  pl.BlockSpec((tm, nj), row), pl.BlockSpec((tm, nj), row),
                  pl.BlockSpec((eb, d), lambda i, s: (s, 0))],
        out_specs=pl.BlockSpec((tm, d), row),
        out_shape=jax.ShapeDtypeStruct((t, d), F32),
        scratch_shapes=[pltpu.VMEM((tm, d), BF16),
                        pltpu.VMEM((tm, nj), F32),
                        pltpu.VMEM((tm, nj), F32),
                        pltpu.VMEM((tm * W3_PITCH, PEER_NKEYS), jnp.uint32),
                        pltpu.VMEM((tm, d), F32)],
        compiler_params=_cparams(("arbitrary", "arbitrary")),
        name="peer",
    )(h, sh2, sc2, g2, ng2, fg, eidx_t, gate_t, uv)


def _layer(x2, c, bsz, seq, w_ada, b_ada, norm1_g, w_in, conv_w, conv_b, dt_bias_f, dt_bias_b,
           a_log_f, a_log_b, d_skip, ssd_norm_g, w_ssd_br, na_rpb, w_na_br, w_out, norm2_g,
           peer_wq, peer_keys, peer_u, peer_v, out_g):
    t, d = x2.shape
    d_ssd = SSD_HEADS * SSD_HEADDIM
    d_xbc = d_ssd + 2 * SSD_GROUPS * SSD_STATE
    d_na = NA_HEADS * NA_HEADDIM
    assert seq % (SSD_CHUNK * SSD_SUB) == 0 and seq % (GRID_W * NA_QROWS) == 0
    assert seq // GRID_W >= NA_KROWS and d == d_ssd == d_na

    mod = _ada(c, w_ada, b_ada)
    sh1, sc1, g1, sh2, sc2, g2 = [m.reshape(bsz, 1, d) for m in jnp.split(mod, 6, axis=-1)]

    o = np.cumsum([0, d_ssd, d_xbc, SSD_HEADS, SSD_HEADS, 3 * d_na, d, d])
    w_main = jnp.concatenate([w_in[:, o[1]:o[2]], w_in[:, o[0]:o[1]], w_in[:, o[4]:o[7]]],
                             axis=1).astype(BF16)
    w_dt = jnp.pad(w_in[:, o[2]:o[4]], ((0, 0), (0, LANES - 2 * SSD_HEADS))).astype(BF16)
    z_col = d_xbc // d_ssd
    q_col = (d_xbc + d_ssd) // LANES
    gs_col = (d_xbc + d_ssd + 3 * d_na) // d
    proj, dt_raw = _inproj(x2, sh1, sc1, norm1_g.reshape(1, d), w_main, w_dt, seq)

    heads = np.arange(d_ssd) // SSD_HEADDIM

    def e2(off):
        m = np.zeros((LANES, d_ssd), np.float32)
        m[off + heads, np.arange(d_ssd)] = 1.0
        m[off + 64 + heads, np.arange(d_ssd)] = 1.0
        return jnp.asarray(m, BF16)

    def lanes16(vec, off):
        return jnp.zeros((1, LANES), F32).at[0, off:off + SSD_HEADS].set(vec.astype(F32))

    params = {
        "cw": jnp.pad(conv_w.astype(F32), ((0, 8 - CONV_W), (0, 0))),
        "cb": conv_b.astype(F32).reshape(1, d_xbc),
        "dtb_f": lanes16(dt_bias_f, 0), "dtb_b": lanes16(dt_bias_b, SSD_HEADS),
        "a_f": lanes16(-jnp.exp(a_log_f.astype(F32)), 0),
        "a_b": lanes16(-jnp.exp(a_log_b.astype(F32)), SSD_HEADS),
        "e2_f": e2(0), "e2_b": e2(SSD_HEADS),
        "dsk": jnp.repeat(d_skip.astype(F32), SSD_HEADDIM).reshape(1, d_ssd),
        "ng": ssd_norm_g.astype(F32).reshape(1, d_ssd),
    }
    fwd = _ssd(proj, dt_raw, z_col, None, params, bsz, seq, rev=False)
    ynorm = _ssd(proj, dt_raw, z_col, fwd, params, bsz, seq, rev=True)

    bias, var_of_blk, ks_of_blk = _na_bias(na_rpb, seq // GRID_W)
    attn = _na(proj, bias, var_of_blk, ks_of_blk, q_col, bsz, seq)

    h, scores = _mix(x2, ynorm, attn, proj, gs_col, g1, sh2, sc2, norm2_g.reshape(1, d),
                     w_ssd_br.astype(BF16), w_na_br.astype(BF16), w_out.astype(BF16),
                     peer_wq.astype(BF16), peer_keys.astype(BF16), seq)

    eidx, gate = _select(scores)
    nj = PEER_HEADS * PEER_TOPK
    eidx_t = eidx.reshape(nj, t).T
    gate_t = gate.reshape(nj, t).T
    return _peer(h, sh2, sc2, g2, norm2_g.reshape(1, d), out_g, eidx_t, gate_t,
                 peer_u.astype(BF16), peer_v.astype(BF16), seq)


def kernel(x, c, w_ada, b_ada, norm1_g, w_in, conv_w, conv_b, dt_bias_f, dt_bias_b, a_log_f, a_log_b,
           d_skip, ssd_norm_g, w_ssd_br, na_rpb, w_na_br, w_out, norm2_g, peer_wq, peer_keys, peer_u,
           peer_v, final_g):
    bsz, seq, d = x.shape
    depth = w_ada.shape[0]
    assert depth == 1, "the final RMSNorm is fused into the last layer's PEER kernel"
    i = 0
    out = _layer(x.reshape(bsz * seq, d), c, bsz, seq, w_ada[i], b_ada[i], norm1_g[i], w_in[i],
                 conv_w[i], conv_b[i], dt_bias_f[i], dt_bias_b[i], a_log_f[i], a_log_b[i], d_skip[i],
                 ssd_norm_g[i], w_ssd_br[i], na_rpb[i], w_na_br[i], w_out[i], norm2_g[i],
                 peer_wq[i], peer_keys[i], peer_u[i], peer_v[i], final_g.reshape(1, d))
    return out.reshape(bsz, seq, d)
```
